```python
import functools
import jax, jax.numpy as jnp
from jax import lax
import numpy as np

D_MODEL = 1024
BATCH = 16
SEQ = 2048
DEPTH = 1
DEC_BATCH = 32
DEC_SEQ = 4
PAST_LEN = 16384
PAGE_SIZE = 128

HEAD_DIM = 64
ATTN_HEADS = D_MODEL // 2 // HEAD_DIM
KV_HEADS = 2
GROUP = ATTN_HEADS // KV_HEADS
IDX_HEADS = 8
IDX_DIM = 64
TOPK_MAX = 256
Q_BLOCK = 128
RET_DK = 128
RET_DV = 128
RET_HEADS = D_MODEL // 2 // RET_DV
RET_CHUNK = 128
D_FF = -(-8 * D_MODEL // (3 * 256)) * 256
PLE_DIM = 256
ROPE_THETA = 10000.0
EPS = 1e-6
GN_EPS = 1e-5
IN_SIZES = (ATTN_HEADS * HEAD_DIM, KV_HEADS * HEAD_DIM, KV_HEADS * HEAD_DIM, IDX_HEADS * IDX_DIM, IDX_DIM, IDX_HEADS,
            RET_HEADS * RET_DK, RET_HEADS * RET_DK, RET_HEADS * RET_DV, RET_HEADS * RET_DV)
IN_TOTAL = sum(IN_SIZES)

kernel_name = 'hymba_dsa_retnet_step'


def rms_norm(x, g):
    xf = x.astype(jnp.float32)
    y = xf * lax.rsqrt(jnp.mean(xf * xf, axis=-1, keepdims=True) + EPS)
    return (y * g.astype(jnp.float32)).astype(x.dtype)


def rope(x, pos):
    half = x.shape[-1] // 2
    inv = ROPE_THETA ** (-jnp.arange(half, dtype=jnp.float32) / half)
    ang = pos.astype(jnp.float32)[:, None] * inv[None, :]
    cos = jnp.cos(ang)[:, None, :]
    sin = jnp.sin(ang)[:, None, :]
    xf = x.astype(jnp.float32)
    x1, x2 = xf[..., :half], xf[..., half:]
    return jnp.concatenate([x1 * cos - x2 * sin, x1 * sin + x2 * cos], axis=-1).astype(x.dtype)


def gather_rows(rows, idx):
    return jax.vmap(lambda r, i: r[i])(rows, idx)


def project_mixers(a, w_in, pos):
    b, t, _ = a.shape
    offsets = [int(o) for o in np.cumsum(IN_SIZES)[:-1]]
    q, k, v, qi, ki, wi, rq, rk, rv, rg = jnp.split(a @ w_in, offsets, axis=-1)
    q = rope(q.reshape(b, t, ATTN_HEADS, HEAD_DIM), pos)
    k = rope(k.reshape(b, t, KV_HEADS, HEAD_DIM), pos)
    v = v.reshape(b, t, KV_HEADS, HEAD_DIM)
    qi = rope(qi.reshape(b, t, IDX_HEADS, IDX_DIM), pos)
    ki = rope(ki.reshape(b, t, 1, IDX_DIM), pos)[:, :, 0]
    wi = wi * (IDX_HEADS ** -0.5)
    rq = rope(rq.reshape(b, t, RET_HEADS, RET_DK), pos)
    rk = rope(rk.reshape(b, t, RET_HEADS, RET_DK), pos) * (RET_DK ** -0.5)
    rv = rv.reshape(b, t, RET_HEADS, RET_DV)
    return q, k, v, qi, wi, ki, rq, rk, rv, rg


def indexer_scores(qi, wi, ki, qpos, kpos):
    s = jnp.einsum('bqhd,bkd->bqhk', qi.astype(jnp.float32), ki.astype(jnp.float32)) * (IDX_DIM ** -0.5)
    s = jnp.einsum('bqhk,bqh->bqk', jax.nn.relu(s), wi.astype(jnp.float32))
    return jnp.where(kpos[None, None, :] <= qpos[None, :, None], s, -jnp.inf)


def sparse_attend(q, k_sel, v_sel, valid):
    b, nq = q.shape[:2]
    qg = q.reshape(b, nq, KV_HEADS, GROUP, HEAD_DIM).astype(jnp.float32)
    logits = jnp.einsum('bqngd,bqknd->bqngk', qg, k_sel.astype(jnp.float32)) * (HEAD_DIM ** -0.5)
    logits = jnp.where(valid[:, :, None, None, :], logits, -jnp.inf)
    probs = jax.nn.softmax(logits, axis=-1)
    o = jnp.einsum('bqngk,bqknd->bqngd', probs, v_sel.astype(jnp.float32))
    return o.reshape(b, nq, ATTN_HEADS * HEAD_DIM).astype(q.dtype)


def dsa_prompt(q, k, v, qi, wi, ki, topk):
    b, s = q.shape[:2]
    kpos = jnp.arange(s)

    def block(bi):
        start = bi * Q_BLOCK
        qb = lax.dynamic_slice_in_dim(q, start, Q_BLOCK, axis=1)
        qib = lax.dynamic_slice_in_dim(qi, start, Q_BLOCK, axis=1)
        wib = lax.dynamic_slice_in_dim(wi, start, Q_BLOCK, axis=1)
        qpos = start + jnp.arange(Q_BLOCK)
        _, idx = lax.top_k(indexer_scores(qib, wib, ki, qpos, kpos), topk)
        valid = idx <= qpos[None, :, None]
        return sparse_attend(qb, gather_rows(k, idx), gather_rows(v, idx), valid)

    out = lax.map(block, jnp.arange(s // Q_BLOCK))
    return out.transpose(1, 0, 2, 3).reshape(b, s, -1)


def dsa_sample(q, k, v, qi, wi, ki, pool_k, pool_v, pool_ik, page_table, topk):
    db, t = q.shape[:2]
    past = page_table.shape[1] * PAGE_SIZE
    ik_past = pool_ik[page_table].reshape(db, past, IDX_DIM).astype(ki.dtype)
    ik_all = jnp.concatenate([ik_past, ki], axis=1)
    qpos = past + jnp.arange(t)
    kpos = jnp.arange(past + t)
    _, idx = lax.top_k(indexer_scores(qi, wi, ik_all, qpos, kpos), topk)
    valid = idx <= qpos[None, :, None]
    is_new = idx >= past
    idx_past = jnp.minimum(idx, past - 1)
    phys = jnp.take_along_axis(page_table, (idx_past // PAGE_SIZE).reshape(db, -1), axis=1).reshape(idx.shape)
    off = idx_past % PAGE_SIZE
    idx_new = jnp.clip(idx - past, 0, t - 1)

    def pick(pool, new):
        return jnp.where(is_new[..., None, None], gather_rows(new, idx_new), pool[phys, off].astype(new.dtype))

    return sparse_attend(q, pick(pool_k, k), pick(pool_v, v), valid)


def retention_chunk(state, q, k, v, log_g):
    c = q.shape[1]
    i = jnp.arange(c, dtype=jnp.float32)
    diff = i[:, None] - i[None, :]
    decay = jnp.where(diff >= 0, jnp.exp(log_g[:, None, None] * jnp.maximum(diff, 0.0)), 0.0)
    inner = jnp.einsum('bihd,bjhd->bhij', q, k) * decay[None]
    o = jnp.einsum('bhij,bjhe->bihe', inner, v)
    q_dec = jnp.exp(log_g[None, :] * (i[:, None] + 1.0))
    o = o + jnp.einsum('bihd,bhde->bihe', q, state) * q_dec[None, :, :, None]
    k_dec = jnp.exp(log_g[None, :] * (c - 1.0 - i)[:, None])
    new_state = jnp.exp(log_g * c)[None, :, None, None] * state + jnp.einsum('bjhd,bjhe->bhde', k * k_dec[None, :, :, None], v)
    return new_state, o


def retention(rq, rk, rv, s0, chunk):
    b, t = rq.shape[:2]
    nc = t // chunk
    log_g = jnp.log1p(-jnp.exp2(-5.0 - jnp.arange(RET_HEADS, dtype=jnp.float32)))

    def to_chunks(x):
        return x.astype(jnp.float32).reshape(b, nc, chunk, *x.shape[2:]).swapaxes(0, 1)

    s_fin, o = lax.scan(lambda st, xs: retention_chunk(st, xs[0], xs[1], xs[2], log_g),
                        s0.astype(jnp.float32), (to_chunks(rq), to_chunks(rk), to_chunks(rv)))
    return o.swapaxes(0, 1).reshape(b, t, RET_HEADS, RET_DV), s_fin


def retention_out(o, rg, gain, dtype):
    mu = jnp.mean(o, axis=-1, keepdims=True)
    var = jnp.mean(jnp.square(o - mu), axis=-1, keepdims=True)
    on = ((o - mu) * lax.rsqrt(var + GN_EPS)).reshape(o.shape[0], o.shape[1], -1) * gain.astype(jnp.float32)
    return (jax.nn.silu(rg.astype(jnp.float32)) * on).astype(dtype)


def trunk_layer(h, p, pos, s0, chunk, attend, g_mix, w_in, g_retn, w_out, g_ffn, w_ffn_in, w_ffn_out,
                g_ple, w_ple_gate, w_ple_proj):
    a = rms_norm(h, g_mix)
    q, k, v, qi, wi, ki, rq, rk, rv, rg = project_mixers(a, w_in, pos)
    attn = attend(q, k, v, qi, wi, ki)
    ro, s_fin = retention(rq, rk, rv, s0, chunk)
    ret = retention_out(ro, rg, g_retn, h.dtype)
    h = h + jnp.concatenate([attn, ret], axis=-1) @ w_out
    f = rms_norm(h, g_ffn)
    gt, up = jnp.split(f @ w_ffn_in, 2, axis=-1)
    h = h + (jax.nn.silu(gt) * up) @ w_ffn_out
    gate = jax.nn.sigmoid(rms_norm(h, g_ple) @ w_ple_gate)
    h = h + gate * (p.astype(h.dtype) @ w_ple_proj)
    return h, (k, v, ki, s_fin.astype(h.dtype))


def setup_inputs(seed: int = 0) -> dict:
    key = jax.random.key(seed)
    ks = jax.random.split(key, 20)
    f32 = jnp.float32
    n_pages = PAST_LEN // PAGE_SIZE
    n_used = DEC_BATCH * n_pages
    n_phys = n_used + (n_used + 3) // 4

    def normal(k, shape, scale=1.0):
        return scale * jax.random.normal(k, shape, f32)

    def gain(k, shape):
        return 1.0 + 0.02 * jax.random.normal(k, shape, f32)

    page_table = jax.random.permutation(ks[0], n_phys)[:n_used].reshape(DEC_BATCH, n_pages).astype(jnp.int32)
    return {
        'x_prompt': normal(ks[1], (BATCH, SEQ, D_MODEL)),
        'x_sample': normal(ks[2], (DEC_BATCH, DEC_SEQ, D_MODEL)),
        'cache_k': normal(ks[3], (DEPTH, n_phys, PAGE_SIZE, KV_HEADS, HEAD_DIM)),
        'cache_v': normal(ks[4], (DEPTH, n_phys, PAGE_SIZE, KV_HEADS, HEAD_DIM)),
        'cache_idx_k': normal(ks[5], (DEPTH, n_phys, PAGE_SIZE, IDX_DIM)),
        'state_retn': normal(ks[6], (DEPTH, DEC_BATCH, RET_HEADS, RET_DK, RET_DV), 0.5),
        'page_table': page_table,
        'p_prompt': normal(ks[7], (DEPTH, BATCH, SEQ, PLE_DIM)),
        'p_sample': normal(ks[8], (DEPTH, DEC_BATCH, DEC_SEQ, PLE_DIM)),
        'g_mix': gain(ks[9], (DEPTH, D_MODEL)),
        'w_in': normal(ks[10], (DEPTH, D_MODEL, IN_TOTAL), D_MODEL ** -0.5),
        'g_retn': gain(ks[11], (DEPTH, RET_HEADS * RET_DV)),
        'w_out': normal(ks[12], (DEPTH, D_MODEL, D_MODEL), D_MODEL ** -0.5),
        'g_ffn': gain(ks[13], (DEPTH, D_MODEL)),
        'w_ffn_in': normal(ks[14], (DEPTH, D_MODEL, 2 * D_FF), D_MODEL ** -0.5),
        'w_ffn_out': normal(ks[15], (DEPTH, D_FF, D_MODEL), D_FF ** -0.5),
        'g_ple': gain(ks[16], (DEPTH, D_MODEL)),
        'w_ple_gate': normal(ks[17], (DEPTH, D_MODEL, D_MODEL), D_MODEL ** -0.5),
        'w_ple_proj': normal(ks[18], (DEPTH, PLE_DIM, D_MODEL), PLE_DIM ** -0.5),
        'g_final': gain(ks[19], (D_MODEL,)),
    }


def reference(x_prompt, x_sample, cache_k, cache_v, cache_idx_k, state_retn, page_table, p_prompt, p_sample,
              g_mix, w_in, g_retn, w_out, g_ffn, w_ffn_in, w_ffn_out, g_ple, w_ple_gate, w_ple_proj, g_final):
    seq = x_prompt.shape[1]
    dec_seq = x_sample.shape[1]
    past = page_table.shape[1] * PAGE_SIZE
    topk_prompt = min(TOPK_MAX, seq // 4)
    topk_sample = min(TOPK_MAX, (past + dec_seq) // 4)
    pos_prompt = jnp.arange(seq)
    pos_sample = past + jnp.arange(dec_seq)
    h_p, h_s = x_prompt, x_sample
    new_p, new_s = [], []
    for i in range(DEPTH):
        weights = (g_mix[i], w_in[i], g_retn[i], w_out[i], g_ffn[i], w_ffn_in[i], w_ffn_out[i],
                   g_ple[i], w_ple_gate[i], w_ple_proj[i])
        s0 = jnp.zeros((x_prompt.shape[0], RET_HEADS, RET_DK, RET_DV), jnp.float32)
        attend_p = functools.partial(dsa_prompt, topk=topk_prompt)
        h_p, st_p = trunk_layer(h_p, p_prompt[i], pos_prompt, s0, RET_CHUNK, attend_p, *weights)
        attend_s = functools.partial(dsa_sample, pool_k=cache_k[i], pool_v=cache_v[i], pool_ik=cache_idx_k[i],
                                     page_table=page_table, topk=topk_sample)
        h_s, st_s = trunk_layer(h_s, p_sample[i], pos_sample, state_retn[i], dec_seq, attend_s, *weights)
        new_p.append(st_p)
        new_s.append(st_s)
    y_prompt = rms_norm(h_p, g_final)
    y_sample = rms_norm(h_s, g_final)
    k_prompt = jnp.stack([st[0] for st in new_p])
    v_prompt = jnp.stack([st[1] for st in new_p])
    idx_k_prompt = jnp.stack([st[2] for st in new_p])
    retn_prompt = jnp.stack([st[3] for st in new_p])
    k_sample = jnp.stack([st[0] for st in new_s])
    v_sample = jnp.stack([st[1] for st in new_s])
    idx_k_sample = jnp.stack([st[2] for st in new_s])
    retn_sample = jnp.stack([st[3] for st in new_s])
    return (y_prompt, y_sample, k_prompt, v_prompt, idx_k_prompt, retn_prompt, k_sample, v_sample, idx_k_sample, retn_sample)
```

```python
import functools

import jax
import jax.numpy as jnp
import numpy as np
from jax import lax
from jax.experimental import pallas as pl
from jax.experimental.pallas import tpu as pltpu

F32 = jnp.float32
BF16 = jnp.bfloat16

HEAD_DIM = 64
KV_HEADS = 2
IDX_HEADS = 8
IDX_DIM = 64
TOPK_MAX = 256
RET_DK = 128
RET_DV = 128
RET_CHUNK = 128
PAGE_SIZE = 128
ROPE_THETA = 10000.0
EPS = 1e-6
GN_EPS = 1e-5

LANES = 128
SUBLANES = 8
VMEM_LIMIT_BYTES = 56 * 1024 * 1024

NEG = -3.0e38
POS = 3.0e38
SOFTMAX_NEG = -1.0e30


def _cparams(*sem):
    return pltpu.CompilerParams(dimension_semantics=sem, vmem_limit_bytes=VMEM_LIMIT_BYTES)


def _const_spec(shape):
    zeros = (0,) * len(shape)
    return pl.BlockSpec(shape, lambda *_: zeros, pipeline_mode=pl.Buffered(1))


def _rope_tables(pos, dim):
    half = dim // 2
    inv = ROPE_THETA ** (-jnp.arange(half, dtype=F32) / half)
    ang = pos.astype(F32)[:, None] * inv[None, :]
    cos = jnp.cos(ang)
    sin = jnp.sin(ang)
    reps = LANES // dim
    cos_t = jnp.tile(jnp.concatenate([cos, cos], axis=1), (1, reps))
    sin_t = jnp.tile(jnp.concatenate([-sin, sin], axis=1), (1, reps))
    return cos_t, sin_t


def _swap_halves_64(x):
    lane = lax.broadcasted_iota(jnp.int32, x.shape, 1)
    first = (lane % 64) < 32
    return jnp.where(first, pltpu.roll(x, 96, 1), pltpu.roll(x, 32, 1))


def _inproj_kernel(x_ref, g_ref, w_ref, c64_ref, s64_ref, c128_ref, s128_ref,
                   q_ref, k_ref, v_ref, qi_ref, ki_ref, wi_ref, rq_ref, rkt_ref, rv_ref, rg_ref,
                   kt_ref, vb_ref, kit_ref, *, n_q, n_kv, n_qi, n_ret):
    x = x_ref[...]
    ms = jnp.mean(x * x, axis=-1, keepdims=True)
    a = ((x * lax.rsqrt(ms + EPS)) * g_ref[...]).astype(BF16)
    c64, s64 = c64_ref[...], s64_ref[...]
    c128, s128 = c128_ref[...], s128_ref[...]

    def proj(col, width=LANES):
        return jnp.dot(a, w_ref[:, col:col + width], preferred_element_type=F32)

    def rope64(y):
        return y * c64 + _swap_halves_64(y) * s64

    def rope128(y):
        return y * c128 + pltpu.roll(y, 64, 1) * s128

    col = 0
    for j in range(n_q // LANES):
        q_ref[:, j * LANES:(j + 1) * LANES] = (rope64(proj(col)) * (HEAD_DIM ** -0.5)).astype(BF16)
        col += LANES
    tm = x.shape[0]
    zt = jnp.zeros((HEAD_DIM, tm), BF16)
    lane = lax.broadcasted_iota(jnp.int32, (tm, LANES), 1)
    low = lane < HEAD_DIM

    kk = rope64(proj(col))
    col += LANES
    k_ref[...] = kk
    kkt = kk.T.astype(BF16)
    for g in range(KV_HEADS):
        ktg = kkt[g * HEAD_DIM:(g + 1) * HEAD_DIM, :]
        base = g * 4 * HEAD_DIM
        kt_ref[base:base + HEAD_DIM, :] = ktg
        kt_ref[base + HEAD_DIM:base + 2 * HEAD_DIM, :] = zt
        kt_ref[base + 2 * HEAD_DIM:base + 3 * HEAD_DIM, :] = zt
        kt_ref[base + 3 * HEAD_DIM:base + 4 * HEAD_DIM, :] = ktg
    vv = proj(col)
    col += LANES
    v_ref[...] = vv
    vr = pltpu.roll(vv, HEAD_DIM, 1)
    vb_ref[:, 0 * LANES:1 * LANES] = jnp.where(low, vv, 0.0).astype(BF16)
    vb_ref[:, 1 * LANES:2 * LANES] = jnp.where(low, 0.0, vr).astype(BF16)
    vb_ref[:, 2 * LANES:3 * LANES] = jnp.where(low, vr, 0.0).astype(BF16)
    vb_ref[:, 3 * LANES:4 * LANES] = jnp.where(low, 0.0, vv).astype(BF16)
    for j in range(n_qi // LANES):
        qi_ref[:, j * LANES:(j + 1) * LANES] = (rope64(proj(col)) * (IDX_DIM ** -0.5)).astype(BF16)
        col += LANES
    y = proj(col)
    col += LANES
    ki = rope64(y)
    ki_ref[...] = ki[:, :IDX_DIM]
    kit = ki.T[:IDX_DIM, :].astype(BF16)
    kit_ref[0 * IDX_DIM:1 * IDX_DIM, :] = kit
    kit_ref[1 * IDX_DIM:2 * IDX_DIM, :] = zt
    kit_ref[2 * IDX_DIM:3 * IDX_DIM, :] = zt
    kit_ref[3 * IDX_DIM:4 * IDX_DIM, :] = kit
    wi_ref[...] = y[:, IDX_DIM:IDX_DIM + IDX_HEADS] * (IDX_HEADS ** -0.5)
    for j in range(n_ret // LANES):
        rq_ref[:, j * LANES:(j + 1) * LANES] = rope128(proj(col)).astype(BF16)
        col += LANES
    for j in range(n_ret // LANES):
        rk = rope128(proj(col)) * (RET_DK ** -0.5)
        rkt_ref[j * LANES:(j + 1) * LANES, :] = rk.T.astype(BF16)
        col += LANES
    for j in range(n_ret // LANES):
        rv_ref[:, j * LANES:(j + 1) * LANES] = proj(col).astype(BF16)
        col += LANES
    for j in range(n_ret // LANES):
        rg_ref[:, j * LANES:(j + 1) * LANES] = proj(col)
        col += LANES


def _pack_w_in(w_in, n_q, n_kv, n_qi, n_ret):
    d = w_in.shape[0]
    sizes = (n_q, n_kv, n_kv, n_qi, IDX_DIM, IDX_HEADS, n_ret, n_ret, n_ret, n_ret)
    offs = np.concatenate([[0], np.cumsum(sizes)])
    seg = [w_in[:, offs[i]:offs[i + 1]] for i in range(len(sizes))]
    pad = jnp.zeros((d, LANES - IDX_DIM - IDX_HEADS), w_in.dtype)
    packed = jnp.concatenate(seg[:4] + [seg[4], seg[5], pad] + seg[6:], axis=1)
    return packed.astype(BF16)


def _inproj(x2d, pos, g_mix, w_packed, *, seq, tm, dims):
    n_q, n_kv, n_qi, n_ret = dims
    rows, d = x2d.shape
    nb = rows // seq
    spt = seq // tm
    c64, s64 = _rope_tables(pos, HEAD_DIM)
    c128, s128 = _rope_tables(pos, RET_DK)
    wcols = w_packed.shape[1]

    row = lambda w: pl.BlockSpec((tm, w), lambda i: (i, 0))
    tab = pl.BlockSpec((tm, LANES), lambda i: (i % spt, 0))
    tr = lambda h: pl.BlockSpec((None, h, tm), lambda i: (i // spt, 0, i % spt))
    out_shape = (
        jax.ShapeDtypeStruct((rows, n_q), BF16),
        jax.ShapeDtypeStruct((rows, n_kv), F32),
        jax.ShapeDtypeStruct((rows, n_kv), F32),
        jax.ShapeDtypeStruct((rows, n_qi), BF16),
        jax.ShapeDtypeStruct((rows, IDX_DIM), F32),
        jax.ShapeDtypeStruct((rows, IDX_HEADS), F32),
        jax.ShapeDtypeStruct((rows, n_ret), BF16),
        jax.ShapeDtypeStruct((nb, n_ret, seq), BF16),
        jax.ShapeDtypeStruct((rows, n_ret), BF16),
        jax.ShapeDtypeStruct((rows, n_ret), F32),
        jax.ShapeDtypeStruct((nb, 4 * n_kv, seq), BF16),
        jax.ShapeDtypeStruct((rows, 4 * n_kv), BF16),
        jax.ShapeDtypeStruct((nb, 4 * IDX_DIM, seq), BF16),
    )
    out_specs = (row(n_q), row(n_kv), row(n_kv), row(n_qi), row(IDX_DIM), row(IDX_HEADS),
                 row(n_ret), tr(n_ret), row(n_ret), row(n_ret), tr(4 * n_kv), row(4 * n_kv), tr(4 * IDX_DIM))
    return pl.pallas_call(
        functools.partial(_inproj_kernel, n_q=n_q, n_kv=n_kv, n_qi=n_qi, n_ret=n_ret),
        out_shape=out_shape,
        grid=(rows // tm,),
        in_specs=[row(d), _const_spec((1, d)), _const_spec((d, wcols)), tab, tab, tab, tab],
        out_specs=out_specs,
        compiler_params=_cparams("parallel"),
        name="inproj",
    )(x2d, g_mix.reshape(1, d), w_packed, c64, s64, c128, s128)


def _fold_cols(s_ref, nch, cw, init, fn):
    def body(c, carry):
        base = pl.multiple_of(c * cw, cw)
        for j in range(cw // LANES):
            blk = s_ref[:, pl.ds(base + j * LANES, LANES)]
            carry = fn(carry, blk, base + j * LANES)
        return carry
    return lax.fori_loop(0, nch, body, init)


def _lane_sum(x):
    return jnp.broadcast_to(jnp.sum(x, axis=1, keepdims=True), x.shape)


def _lane_min(x):
    return jnp.broadcast_to(jnp.min(x, axis=1, keepdims=True), x.shape)


def _lane_max(x):
    return jnp.broadcast_to(jnp.max(x, axis=1, keepdims=True), x.shape)


def _any(mask):
    return jnp.max(jnp.where(mask, 1.0, 0.0)) > 0.5


def _count(s_ref, nch, cw, pred):
    rows = s_ref.shape[0]
    acc = _fold_cols(s_ref, nch, cw, jnp.zeros((rows, LANES), F32),
                     lambda a, blk, col0: a + jnp.where(pred(blk, col0), 1.0, 0.0))
    return _lane_sum(acc)


def _select_threshold(s_ref, thr_ref, nch, cw, rmin, rmax, topk, n_fast):
    rows = s_ref.shape[0]
    kf = float(topk)

    def fast_body(_, carry):
        lo, hi = carry
        mid = lo + (hi - lo) * 0.5
        up = _count(s_ref, nch, cw, lambda blk, _c: blk > mid) >= kf
        return jnp.where(up, mid, lo), jnp.where(up, hi, mid)

    lo, hi = lax.fori_loop(0, n_fast, fast_body, (rmin, rmax))
    c_lo = _count(s_ref, nch, cw, lambda blk, _c: blk >= lo)
    unres = c_lo > kf
    thr_ref[...] = lo

    @pl.when(_any(unres))
    def _exact():
        def snap(carry, blk, _c):
            a, b = carry
            return (jnp.minimum(a, jnp.where(blk >= lo, blk, POS)),
                    jnp.maximum(b, jnp.where(blk <= hi, blk, NEG)))
        a, b = _fold_cols(s_ref, nch, cw, (jnp.full((rows, LANES), POS, F32), jnp.full((rows, LANES), NEG, F32)), snap)
        lo2 = jnp.where(unres, _lane_min(a), lo)
        ub = jnp.where(unres, _lane_max(b), lo)

        def cond(carry):
            return carry[2] > 0

        def body(carry):
            lo2, ub, _ = carry
            mid = lo2 + (ub - lo2) * 0.5
            mid = jnp.where(mid >= ub, lo2, mid)

            def step(c3, blk, _c):
                cnt, a, b = c3
                gt = blk > mid
                return (cnt + jnp.where(gt, 1.0, 0.0),
                        jnp.minimum(a, jnp.where(gt, blk, POS)),
                        jnp.maximum(b, jnp.where(gt, NEG, blk)))
            cnt, a, b = _fold_cols(
                s_ref, nch, cw,
                (jnp.zeros((rows, LANES), F32), jnp.full((rows, LANES), POS, F32), jnp.full((rows, LANES), NEG, F32)),
                step)
            up = _lane_sum(cnt) >= kf
            active = lo2 < ub
            lo2n = jnp.where(active & up, _lane_min(a), lo2)
            ubn = jnp.where(active & jnp.logical_not(up), _lane_max(b), ub)
            return lo2n, ubn, _any(lo2n < ubn).astype(jnp.int32)

        v, _, _ = lax.while_loop(cond, body, (lo2, ub, _any(lo2 < ub).astype(jnp.int32)))

        need = kf - _count(s_ref, nch, cw, lambda blk, _c: blk > v)
        ncols = s_ref.shape[1]
        lane = lax.broadcasted_iota(jnp.int32, (rows, LANES), 1)

        def col_index(col0):
            return (lane + col0).astype(F32)

        def j_body(_, carry):
            lo_j, hi_j = carry
            mid_j = jnp.floor((lo_j + hi_j) * 0.5)
            ok = _count(s_ref, nch, cw, lambda blk, col0: (blk == v) & (col_index(col0) <= mid_j)) >= need
            return jnp.where(ok, lo_j, mid_j), jnp.where(ok, mid_j, hi_j)

        n_j = int(np.ceil(np.log2(ncols))) + 1
        _, j_last = lax.fori_loop(0, n_j, j_body,
                                  (jnp.full((rows, LANES), -1.0, F32), jnp.full((rows, LANES), float(ncols - 1), F32)))

        def drop_body(c, _):
            base = pl.multiple_of(c * cw, cw)
            for j in range(cw // LANES):
                sl = pl.ds(base + j * LANES, LANES)
                blk = s_ref[:, sl]
                drop = unres & (blk == v) & (col_index(base + j * LANES) > j_last)
                s_ref[:, sl] = jnp.where(drop, NEG, blk)
            return 0
        lax.fori_loop(0, nch, drop_body, 0)
        thr_ref[...] = jnp.where(unres, v, lo)


def _tile_lanes(x, width):
    return x if width == LANES else jnp.concatenate([x] * (width // LANES), axis=1)


def _dsa_prompt_kernel(qi_ref, wi_ref, q_ref, kit_ref, kt_ref, vb_ref, o_ref, s_ref, wib_ref, thr_ref,
                       *, tq, topk, n_fast, rs):
    cw = tq
    i = pl.program_id(1)
    nch = i + 1
    n_pairs = q_ref.shape[1] // LANES
    pairs_per_kv = n_pairs // KV_HEADS

    wi = wi_ref[...]
    for h in range(IDX_HEADS):
        wib_ref[h] = jnp.broadcast_to(wi[:, h:h + 1], (tq, LANES))

    row_l = lax.broadcasted_iota(jnp.int32, (rs, cw), 0)
    col_l = lax.broadcasted_iota(jnp.int32, (rs, cw), 1)

    def score_body(c, carry):
        base = pl.multiple_of(c * cw, cw)
        kit_lo = kit_ref[0:2 * IDX_DIM, pl.ds(base, cw)]
        kit_hi = kit_ref[2 * IDX_DIM:4 * IDX_DIM, pl.ds(base, cw)]
        off_diag = c < i
        out = []
        for r in range(tq // rs):
            rows = slice(r * rs, (r + 1) * rs)
            acc = jnp.zeros((rs, cw), F32)
            for p in range(IDX_HEADS // 2):
                lhs = qi_ref[rows, p * LANES:(p + 1) * LANES]
                a_e = jnp.dot(lhs, kit_lo, preferred_element_type=F32)
                a_o = jnp.dot(lhs, kit_hi, preferred_element_type=F32)
                acc = acc + jnp.maximum(a_e, 0.0) * _tile_lanes(wib_ref[2 * p, rows, :], cw)
                acc = acc + jnp.maximum(a_o, 0.0) * _tile_lanes(wib_ref[2 * p + 1, rows, :], cw)
            valid = off_diag | (col_l <= row_l + r * rs)
            s_ref[rows, pl.ds(base, cw)] = jnp.where(valid, acc, NEG)
            lo_p = jnp.where(valid, acc, POS)
            hi_p = jnp.where(valid, acc, NEG)
            mn, mx = carry[r]
            for j in range(cw // LANES):
                mn = jnp.minimum(mn, lo_p[:, j * LANES:(j + 1) * LANES])
                mx = jnp.maximum(mx, hi_p[:, j * LANES:(j + 1) * LANES])
            out.append((mn, mx))
        return tuple(out)

    init = tuple((jnp.full((rs, LANES), POS, F32), jnp.full((rs, LANES), NEG, F32)) for _ in range(tq // rs))
    mm = lax.fori_loop(0, nch, score_body, init)
    rmin = _lane_min(jnp.concatenate([m[0] for m in mm], axis=0))
    rmax = _lane_max(jnp.concatenate([m[1] for m in mm], axis=0))

    _select_threshold(s_ref, thr_ref, nch, cw, rmin, rmax, topk, n_fast)

    thr = _tile_lanes(thr_ref[...], cw)
    lane = lax.broadcasted_iota(jnp.int32, (tq, LANES), 1)
    low = lane < HEAD_DIM
    for p in range(n_pairs):
        g = p // pairs_per_kv
        qp = q_ref[:, p * LANES:(p + 1) * LANES]
        kbase = g * 4 * HEAD_DIM

        def att_body(c, carry, qp=qp, kbase=kbase, g=g):
            m_e, l_e, m_o, l_o, acc = carry
            base = pl.multiple_of(c * cw, cw)
            sel = s_ref[:, pl.ds(base, cw)] >= thr
            kt_lo = kt_ref[kbase:kbase + 2 * HEAD_DIM, pl.ds(base, cw)]
            kt_hi = kt_ref[kbase + 2 * HEAD_DIM:kbase + 4 * HEAD_DIM, pl.ds(base, cw)]
            v_lo = vb_ref[pl.ds(base, cw), (2 * g) * LANES:(2 * g + 1) * LANES]
            v_hi = vb_ref[pl.ds(base, cw), (2 * g + 1) * LANES:(2 * g + 2) * LANES]

            def head(kt, vv, m, l):
                lg = jnp.where(sel, jnp.dot(qp, kt, preferred_element_type=F32), SOFTMAX_NEG)
                m_new = jnp.maximum(m, jnp.max(lg, axis=1, keepdims=True))
                alpha = jnp.exp(m - m_new)
                pr = jnp.exp(lg - m_new)
                l_new = alpha * l + jnp.sum(pr, axis=1, keepdims=True)
                pv = jnp.dot(pr.astype(BF16), vv, preferred_element_type=F32)
                return m_new, l_new, alpha, pv

            m_e, l_e, a_e, pv_e = head(kt_lo, v_lo, m_e, l_e)
            m_o, l_o, a_o, pv_o = head(kt_hi, v_hi, m_o, l_o)
            acc = acc * jnp.where(low, a_e, a_o) + pv_e + pv_o
            return m_e, l_e, m_o, l_o, acc

        neg1 = jnp.full((tq, 1), SOFTMAX_NEG, F32)
        zero1 = jnp.zeros((tq, 1), F32)
        m_e, l_e, m_o, l_o, acc = lax.fori_loop(
            0, nch, att_body, (neg1, zero1, neg1, zero1, jnp.zeros((tq, LANES), F32)))
        o_ref[:, p * LANES:(p + 1) * LANES] = (acc / jnp.where(low, l_e, l_o)).astype(o_ref.dtype)


def _dsa_prompt(qi, wi, q, kit, kt, vb, *, nb, seq, tq, topk, n_fast):
    n_q = q.shape[1]
    spb = seq // tq
    qrow = lambda w: pl.BlockSpec((tq, w), lambda b, i: (b * spb + i, 0))
    per_b = lambda h: pl.BlockSpec((None, h, seq), lambda b, i: (b, 0, 0))
    return pl.pallas_call(
        functools.partial(_dsa_prompt_kernel, tq=tq, topk=topk, n_fast=n_fast, rs=min(tq, 64)),
        out_shape=jax.ShapeDtypeStruct((nb * seq, n_q), BF16),
        grid=(nb, spb),
        in_specs=[qrow(qi.shape[1]), qrow(IDX_HEADS), qrow(n_q),
                  per_b(kit.shape[1]), per_b(kt.shape[1]),
                  pl.BlockSpec((seq, vb.shape[1]), lambda b, i: (b, 0))],
        out_specs=qrow(n_q),
        scratch_shapes=[pltpu.VMEM((tq, seq), F32),
                        pltpu.VMEM((IDX_HEADS, tq, LANES), F32),
                        pltpu.VMEM((tq, LANES), F32)],
        compiler_params=_cparams("parallel", "arbitrary"),
        name="dsa_prompt",
    )(qi, wi, q, kit, kt, vb)


def _retention_tables(chunk, n_heads):
    log_g = jnp.log1p(-jnp.exp2(-5.0 - jnp.arange(n_heads, dtype=F32)))
    i = jnp.arange(chunk, dtype=F32)
    diff = i[:, None] - i[None, :]
    decay = jnp.where(diff >= 0, jnp.exp(log_g[:, None, None] * jnp.maximum(diff, 0.0)), 0.0)
    q_dec = jnp.exp(log_g[:, None] * (i[None, :] + 1.0))
    k_dec = jnp.exp(log_g[:, None] * (chunk - 1.0 - i)[None, :])
    c_dec = jnp.exp(log_g * chunk)
    return decay, q_dec, k_dec, c_dec


def _group_norm_gate(o, rg, gain):
    mu = jnp.mean(o, axis=-1, keepdims=True)
    var = jnp.mean(jnp.square(o - mu), axis=-1, keepdims=True)
    on = ((o - mu) * lax.rsqrt(var + GN_EPS)) * gain
    return jax.nn.silu(rg) * on


def _retention_prompt_kernel(rq_ref, rkt_ref, rv_ref, rg_ref, gain_ref, dmat_ref, qdec_ref, kdec_ref, cdec_ref,
                             o_ref, st_ref, state_ref, *, chunk, n_heads):
    j = pl.program_id(1)

    @pl.when(j == 0)
    def _():
        state_ref[...] = jnp.zeros_like(state_ref)

    for cc in range(rq_ref.shape[0] // chunk):
        rows = slice(cc * chunk, (cc + 1) * chunk)
        for h in range(n_heads):
            cols = slice(h * RET_DV, (h + 1) * RET_DV)
            q = rq_ref[rows, cols]
            kt = rkt_ref[cols, rows]
            v = rv_ref[rows, cols]
            state = state_ref[h]
            inner = jnp.dot(q, kt, preferred_element_type=F32) * dmat_ref[h]
            o = jnp.dot(inner.astype(BF16), v, preferred_element_type=F32)
            o = o + jnp.dot(q, state.astype(BF16), preferred_element_type=F32) * qdec_ref[h]
            kd = (kt.astype(F32) * kdec_ref[h]).astype(BF16)
            state_ref[h] = cdec_ref[h] * state + jnp.dot(kd, v, preferred_element_type=F32)
            o_ref[rows, cols] = _group_norm_gate(o, rg_ref[rows, cols], gain_ref[:, cols]).astype(o_ref.dtype)

    @pl.when(j == pl.num_programs(1) - 1)
    def _():
        st_ref[...] = state_ref[...]


def _retention_prompt(rq, rkt, rv, rg, g_retn, *, nb, seq, tr):
    n_ret = rq.shape[1]
    n_heads = n_ret // RET_DV
    chunk = RET_CHUNK
    decay, q_dec, k_dec, c_dec = _retention_tables(chunk, n_heads)
    qdec_b = jnp.broadcast_to(q_dec[:, :, None], (n_heads, chunk, LANES))
    kdec_b = k_dec[:, None, :]
    cdec_b = jnp.broadcast_to(c_dec[:, None, None], (n_heads, 1, LANES))
    spb = seq // tr
    row = lambda: pl.BlockSpec((tr, n_ret), lambda b, j: (b * spb + j, 0))
    return pl.pallas_call(
        functools.partial(_retention_prompt_kernel, chunk=chunk, n_heads=n_heads),
        out_shape=(jax.ShapeDtypeStruct((nb * seq, n_ret), BF16),
                   jax.ShapeDtypeStruct((nb, n_heads, RET_DK, RET_DV), F32)),
        grid=(nb, spb),
        in_specs=[row(), pl.BlockSpec((None, n_ret, tr), lambda b, j: (b, 0, j)), row(), row(),
                  _const_spec((1, n_ret)), _const_spec((n_heads, chunk, chunk)),
                  _const_spec((n_heads, chunk, LANES)), _const_spec((n_heads, 1, chunk)),
                  _const_spec((n_heads, 1, LANES))],
        out_specs=(row(), pl.BlockSpec((None, n_heads, RET_DK, RET_DV), lambda b, j: (b, 0, 0, 0))),
        scratch_shapes=[pltpu.VMEM((n_heads, RET_DK, RET_DV), F32)],
        compiler_params=_cparams("parallel", "arbitrary"),
        name="retention_prompt",
    )(rq, rkt, rv, rg, g_retn.reshape(1, n_ret), decay, qdec_b, kdec_b, cdec_b)


def _rms(x, g):
    return (x * lax.rsqrt(jnp.mean(x * x, axis=-1, keepdims=True) + EPS)) * g


def _tail_kernel(h_ref, attn_ref, ret_ref, p_ref, wo_a_ref, wo_r_ref, g_ffn_ref, w_gate_ref, w_up_ref, w_down_ref,
                 g_ple_ref, w_pg_ref, w_pp_ref, g_fin_ref, y_ref):
    dot = functools.partial(jnp.dot, preferred_element_type=F32)
    h = h_ref[...] + dot(attn_ref[...], wo_a_ref[...]) + dot(ret_ref[...], wo_r_ref[...])
    f = _rms(h, g_ffn_ref[...]).astype(BF16)
    act = (jax.nn.silu(dot(f, w_gate_ref[...])) * dot(f, w_up_ref[...])).astype(BF16)
    h = h + dot(act, w_down_ref[...])
    gate = jax.nn.sigmoid(dot(_rms(h, g_ple_ref[...]).astype(BF16), w_pg_ref[...]))
    h = h + gate * dot(p_ref[...].astype(BF16), w_pp_ref[...])
    y_ref[...] = _rms(h, g_fin_ref[...])


def _tail(h2d, attn, ret, p2d, w_out, g_ffn, w_ffn_in, w_ffn_out, g_ple, w_ple_gate, w_ple_proj, g_final, *, tm):
    rows, d = h2d.shape
    n_attn = attn.shape[1]
    d_ff = w_ffn_out.shape[0]
    ws = [w_out[:n_attn].astype(BF16), w_out[n_attn:].astype(BF16), g_ffn.reshape(1, d),
          w_ffn_in[:, :d_ff].astype(BF16), w_ffn_in[:, d_ff:].astype(BF16), w_ffn_out.astype(BF16),
          g_ple.reshape(1, d), w_ple_gate.astype(BF16), w_ple_proj.astype(BF16), g_final.reshape(1, d)]
    row = lambda w: pl.BlockSpec((tm, w), lambda i: (i, 0))
    return pl.pallas_call(
        _tail_kernel,
        out_shape=jax.ShapeDtypeStruct((rows, d), F32),
        grid=(rows // tm,),
        in_specs=[row(d), row(n_attn), row(ret.shape[1]), row(p2d.shape[1])] + [_const_spec(w.shape) for w in ws],
        out_specs=row(d),
        compiler_params=_cparams("parallel"),
        name="tail",
    )(h2d, attn, ret, p2d, *ws)


def _page_copies(pools, bufs, sems, pt_ref, b, c, slot, pages_per_chunk):
    copies = []
    for pool, buf in zip(pools, bufs):
        for p in range(pages_per_chunk):
            page = pt_ref[b, c * pages_per_chunk + p]
            copies.append(pltpu.make_async_copy(pool.at[page], buf.at[slot, pl.ds(p * PAGE_SIZE, PAGE_SIZE)],
                                                sems.at[slot]))
    return copies


def _stream_chunks(pools, bufs, sems, pt_ref, n_chunks, pages_per_chunk, compute):
    b = pl.program_id(0)
    nb = pl.num_programs(0)

    def start(bb, cc, slot):
        for cp in _page_copies(pools, bufs, sems, pt_ref, bb, cc, slot, pages_per_chunk):
            cp.start()

    @pl.when(b == 0)
    def _():
        start(0, 0, 0)

    assert n_chunks % 2 == 0
    for c in range(n_chunks):
        slot = c % 2
        if c + 1 < n_chunks:
            start(b, c + 1, 1 - slot)
        else:
            @pl.when(b + 1 < nb)
            def _():
                start(b + 1, 0, 1 - slot)
        for cp in _page_copies(pools, bufs, sems, pt_ref, b, c, slot, pages_per_chunk):
            cp.wait()
        compute(c, slot)


def _nt_dot(a, b):
    return lax.dot_general(a, b, (((1,), (1,)), ((), ())), preferred_element_type=F32)


def _head_sum(x, n_t):
    return jnp.sum(x.reshape(n_t, IDX_HEADS, x.shape[-1]), axis=1)


def _sample_scores_kernel(pt_ref, qi_ref, wi_ref, kinew_ref, pool_ref, s_ref, buf, sems,
                          *, n_t, n_chunks, pages_per_chunk):
    ck = pages_per_chunk * PAGE_SIZE
    qi = qi_ref[...]
    w = wi_ref[...]

    def compute(c, slot):
        a = _nt_dot(qi, buf[slot].astype(BF16))
        s_ref[:, c * ck:(c + 1) * ck] = _head_sum(jnp.maximum(a, 0.0) * w, n_t)

    _stream_chunks([pool_ref], [buf], sems, pt_ref, n_chunks, pages_per_chunk, compute)

    a = _nt_dot(qi, kinew_ref[...])
    s_new = _head_sum(jnp.maximum(a, 0.0) * w, n_t)
    t_q = lax.broadcasted_iota(jnp.int32, s_new.shape, 0)
    t_k = lax.broadcasted_iota(jnp.int32, s_new.shape, 1)
    s_ref[:, n_chunks * ck:] = jnp.where(t_k <= t_q, s_new, NEG)


def _sample_scores(page_table, qi_s, wi_s, kinew, pool_ik, *, n_t, pages_per_chunk):
    db, n_pages = page_table.shape
    n_chunks = n_pages // pages_per_chunk
    past = n_pages * PAGE_SIZE
    rows = n_t * IDX_HEADS
    grid_spec = pltpu.PrefetchScalarGridSpec(
        num_scalar_prefetch=1,
        grid=(db,),
        in_specs=[pl.BlockSpec((None, rows, IDX_DIM), lambda b, pt: (b, 0, 0)),
                  pl.BlockSpec((None, rows, 1), lambda b, pt: (b, 0, 0)),
                  pl.BlockSpec((None, PAGE_SIZE, IDX_DIM), lambda b, pt: (b, 0, 0)),
                  pl.BlockSpec(memory_space=pl.ANY)],
        out_specs=pl.BlockSpec((None, n_t, past + PAGE_SIZE), lambda b, pt: (b, 0, 0)),
        scratch_shapes=[pltpu.VMEM((2, pages_per_chunk * PAGE_SIZE, IDX_DIM), F32),
                        pltpu.SemaphoreType.DMA((2,))],
    )
    return pl.pallas_call(
        functools.partial(_sample_scores_kernel, n_t=n_t, n_chunks=n_chunks, pages_per_chunk=pages_per_chunk),
        out_shape=jax.ShapeDtypeStruct((db, n_t, past + PAGE_SIZE), F32),
        grid_spec=grid_spec,
        compiler_params=_cparams("arbitrary"),
        name="sample_scores",
    )(page_table, qi_s, wi_s, kinew, pool_ik)


def _sample_select_kernel(s_in_ref, s_ref, thr_ref, *, cw, topk, n_fast):
    s_ref[...] = s_in_ref[...]
    rows, ncols = s_ref.shape
    nch = ncols // cw

    def minmax(carry, blk, _c):
        valid = blk > 0.5 * NEG
        return (jnp.minimum(carry[0], jnp.where(valid, blk, POS)), jnp.maximum(carry[1], blk))
    mn, mx = _fold_cols(s_ref, nch, cw, (jnp.full((rows, LANES), POS, F32), jnp.full((rows, LANES), NEG, F32)), minmax)
    _select_threshold(s_ref, thr_ref, nch, cw, _lane_min(mn), _lane_max(mx), topk, n_fast)


def _sample_select(s2d, *, topk, n_fast):
    rows, ncols = s2d.shape
    cw = max(w for w in (LANES, 2 * LANES, 3 * LANES, 4 * LANES) if ncols % w == 0)
    full = pl.BlockSpec((rows, ncols), lambda i: (0, 0), pipeline_mode=pl.Buffered(1))
    return pl.pallas_call(
        functools.partial(_sample_select_kernel, cw=cw, topk=topk, n_fast=n_fast),
        out_shape=(jax.ShapeDtypeStruct((rows, ncols), F32), jax.ShapeDtypeStruct((rows, LANES), F32)),
        grid=(1,),
        in_specs=[full],
        out_specs=(full, pl.BlockSpec((rows, LANES), lambda i: (0, 0))),
        compiler_params=_cparams("arbitrary"),
        name="sample_select",
    )(s2d)


def _sample_attend_kernel(pt_ref, q_ref, s_ref, thr_ref, knew_ref, vnew_ref, poolk_ref, poolv_ref, o_ref,
                          kbuf, vbuf, sems, *, n_t, n_chunks, pages_per_chunk):
    ck = pages_per_chunk * PAGE_SIZE
    q = q_ref[...]
    rows = q.shape[0]
    reps = rows // n_t
    thr = jnp.concatenate([thr_ref[...]] * reps, axis=0)
    state = [jnp.full((rows, 1), SOFTMAX_NEG, F32), jnp.zeros((rows, 1), F32), jnp.zeros((rows, LANES), F32)]

    def attend(kb, vb, s_blk):
        m, l, acc = state
        width = s_blk.shape[1]
        sel = jnp.concatenate([s_blk] * reps, axis=0) >= _tile_lanes(thr, width)
        lg = jnp.where(sel, _nt_dot(q, kb), SOFTMAX_NEG)
        m_new = jnp.maximum(m, jnp.max(lg, axis=1, keepdims=True))
        alpha = jnp.exp(m - m_new)
        pr = jnp.exp(lg - m_new)
        state[0] = m_new
        state[1] = alpha * l + jnp.sum(pr, axis=1, keepdims=True)
        state[2] = alpha * acc + jnp.dot(pr.astype(BF16), vb, preferred_element_type=F32)

    def compute(c, slot):
        attend(kbuf[slot].astype(BF16), vbuf[slot].astype(BF16), s_ref[:, c * ck:(c + 1) * ck])

    _stream_chunks([poolk_ref, poolv_ref], [kbuf, vbuf], sems, pt_ref, n_chunks, pages_per_chunk, compute)
    attend(knew_ref[...], vnew_ref[...], s_ref[:, n_chunks * ck:])
    o_ref[...] = state[2] / state[1]


def _sample_attend(page_table, q_bd, s3d, thr3d, knew, vnew, pool_k, pool_v, *, n_t, pages_per_chunk):
    db, n_pages = page_table.shape
    n_chunks = n_pages // pages_per_chunk
    rows = q_bd.shape[1]
    ncols = s3d.shape[2]
    n_kv = pool_k.shape[2]
    per_b = lambda r, w: pl.BlockSpec((None, r, w), lambda b, pt: (b, 0, 0))
    grid_spec = pltpu.PrefetchScalarGridSpec(
        num_scalar_prefetch=1,
        grid=(db,),
        in_specs=[per_b(rows, n_kv), per_b(n_t, ncols), per_b(n_t, LANES), per_b(PAGE_SIZE, n_kv),
                  per_b(PAGE_SIZE, n_kv), pl.BlockSpec(memory_space=pl.ANY), pl.BlockSpec(memory_space=pl.ANY)],
        out_specs=per_b(rows, n_kv),
        scratch_shapes=[pltpu.VMEM((2, pages_per_chunk * PAGE_SIZE, n_kv), F32),
                        pltpu.VMEM((2, pages_per_chunk * PAGE_SIZE, n_kv), F32),
                        pltpu.SemaphoreType.DMA((2,))],
    )
    return pl.pallas_call(
        functools.partial(_sample_attend_kernel, n_t=n_t, n_chunks=n_chunks, pages_per_chunk=pages_per_chunk),
        out_shape=jax.ShapeDtypeStruct((db, rows, n_kv), F32),
        grid_spec=grid_spec,
        compiler_params=_cparams("arbitrary"),
        name="sample_attend",
    )(page_table, q_bd, s3d, thr3d, knew, vnew, pool_k, pool_v)


def _retention_sample_kernel(rq_ref, rkt_ref, rv_ref, rg_ref, gain_ref, dfull_ref, qdec_ref, kdec_ref, cdec_ref,
                             st_in_ref, o_ref, st_out_ref, *, n_t, n_heads):
    b = pl.program_id(0)

    @pl.when(b == 0)
    def _():
        o_ref[...] = jnp.zeros_like(o_ref)

    n_rows = rq_ref.shape[0]
    row = lax.broadcasted_iota(jnp.int32, (n_rows, RET_DV), 0)
    mine = (row >= b * n_t) & (row < (b + 1) * n_t)
    for h in range(n_heads):
        cols = slice(h * RET_DV, (h + 1) * RET_DV)
        q = jnp.where(mine, rq_ref[:, cols], 0.0).astype(BF16)
        v = rv_ref[:, cols]
        kt = rkt_ref[cols, :]
        state = st_in_ref[h]
        inner = jnp.dot(q, kt, preferred_element_type=F32) * dfull_ref[h]
        o = jnp.dot(inner.astype(BF16), v, preferred_element_type=F32)
        o = o + jnp.dot(q, state.astype(BF16), preferred_element_type=F32) * qdec_ref[h]
        vd = jnp.where(mine, v.astype(F32) * kdec_ref[h], 0.0).astype(BF16)
        st_out_ref[h] = cdec_ref[h] * state + jnp.dot(kt, vd, preferred_element_type=F32)
        gated = _group_norm_gate(o, rg_ref[:, cols], gain_ref[:, cols])
        o_ref[:, cols] = jnp.where(mine, gated, o_ref[:, cols].astype(F32)).astype(o_ref.dtype)


def _retention_sample(rq, rkt, rv, rg, g_retn, state, *, db, n_t):
    n_rows, n_ret = rq.shape
    n_heads = n_ret // RET_DV
    decay, q_dec, k_dec, c_dec = _retention_tables(n_t, n_heads)
    same = jnp.kron(jnp.eye(db, dtype=F32), jnp.ones((n_t, n_t), F32))
    dfull = same[None] * jnp.tile(decay, (1, db, db))
    qdec_b = jnp.broadcast_to(jnp.tile(q_dec, (1, db))[:, :, None], (n_heads, n_rows, RET_DV))
    kdec_b = jnp.broadcast_to(jnp.tile(k_dec, (1, db))[:, :, None], (n_heads, n_rows, RET_DV))
    cdec_b = jnp.broadcast_to(c_dec[:, None, None], (n_heads, 1, RET_DV))
    const = lambda shape: pl.BlockSpec(shape, lambda b: (0,) * len(shape))
    st_spec = pl.BlockSpec((None, n_heads, RET_DK, RET_DV), lambda b: (b, 0, 0, 0))
    return pl.pallas_call(
        functools.partial(_retention_sample_kernel, n_t=n_t, n_heads=n_heads),
        out_shape=(jax.ShapeDtypeStruct((n_rows, n_ret), BF16),
                   jax.ShapeDtypeStruct((db, n_heads, RET_DK, RET_DV), F32)),
        grid=(db,),
        in_specs=[const((n_rows, n_ret)), const((n_ret, n_rows)), const((n_rows, n_ret)), const((n_rows, n_ret)),
                  const((1, n_ret)), const((n_heads, n_rows, n_rows)), const((n_heads, n_rows, RET_DV)),
                  const((n_heads, n_rows, RET_DV)), const((n_heads, 1, RET_DV)), st_spec],
        out_specs=(const((n_rows, n_ret)), st_spec),
        compiler_params=_cparams("arbitrary"),
        name="retention_sample",
    )(rq, rkt, rv, rg, g_retn.reshape(1, n_ret), dfull, qdec_b, kdec_b, cdec_b, state)


def _mixer_dims(w_in, d_model):
    n_q = d_model // 2
    n_ret = d_model // 2
    n_kv = KV_HEADS * HEAD_DIM
    n_qi = IDX_HEADS * IDX_DIM
    assert n_kv == LANES and 2 * n_kv + n_q + n_qi + IDX_DIM + IDX_HEADS + 4 * n_ret == w_in.shape[1]
    return n_q, n_kv, n_qi, n_ret


def _prompt_layer(x, p, weights, *, n_fast=24):
    g_mix, w_in, g_retn, w_out, g_ffn, w_ffn_in, w_ffn_out, g_ple, w_ple_gate, w_ple_proj, g_final = weights
    nb, seq, d = x.shape
    dims = _mixer_dims(w_in, d)
    w_packed = _pack_w_in(w_in, *dims)
    x2d = x.reshape(nb * seq, d)
    tm = min(512, seq)
    (q, k, v, qi, ki, wi, rq, rkt, rv, rg, kt, vb, kit) = _inproj(
        x2d, jnp.arange(seq), g_mix, w_packed, seq=seq, tm=tm, dims=dims)
    topk = min(TOPK_MAX, seq // 4)
    attn = _dsa_prompt(qi, wi, q, kit, kt, vb, nb=nb, seq=seq, tq=min(256, seq), topk=topk, n_fast=n_fast)
    ret, st = _retention_prompt(rq, rkt, rv, rg, g_retn, nb=nb, seq=seq, tr=min(512, seq))
    y = _tail(x2d, attn, ret, p.reshape(nb * seq, -1), w_out, g_ffn, w_ffn_in, w_ffn_out,
              g_ple, w_ple_gate, w_ple_proj, g_final, tm=min(256, nb * seq))
    return (y.reshape(nb, seq, d), k.reshape(nb, seq, KV_HEADS, HEAD_DIM), v.reshape(nb, seq, KV_HEADS, HEAD_DIM),
            ki.reshape(nb, seq, IDX_DIM), st)


def _sample_layer(x, p, pool_k, pool_v, pool_ik, state, page_table, weights, *, n_fast=24):
    g_mix, w_in, g_retn, w_out, g_ffn, w_ffn_in, w_ffn_out, g_ple, w_ple_gate, w_ple_proj, g_final = weights
    db, n_t, d = x.shape
    n_rows = db * n_t
    n_pages = page_table.shape[1]
    past = n_pages * PAGE_SIZE
    dims = _mixer_dims(w_in, d)
    n_q = dims[0]
    w_packed = _pack_w_in(w_in, *dims)
    x2d = x.reshape(n_rows, d)
    pos = jnp.tile(past + jnp.arange(n_t), db)
    (q, k, v, qi, ki, wi, rq, rkt, rv, rg, _, _, _) = _inproj(
        x2d, pos, g_mix, w_packed, seq=n_rows, tm=n_rows, dims=dims)

    pad_rows = lambda a: jnp.pad(a.reshape(db, n_t, -1), ((0, 0), (0, PAGE_SIZE - n_t), (0, 0))).astype(BF16)
    group = n_q // HEAD_DIM // KV_HEADS
    qg = q.reshape(db, n_t, KV_HEADS, group, HEAD_DIM).transpose(0, 2, 3, 1, 4)
    zq = jnp.zeros_like(qg[:, 0])
    q_bd = jnp.stack([jnp.concatenate([qg[:, 0], zq], axis=-1), jnp.concatenate([zq, qg[:, 1]], axis=-1)], axis=1)
    q_bd = q_bd.reshape(db, KV_HEADS * group * n_t, KV_HEADS * HEAD_DIM)
    pages_per_chunk = max(1, min(16, n_pages // 2))

    s3d = _sample_scores(page_table, qi.reshape(db, n_t * IDX_HEADS, IDX_DIM), wi.reshape(db, n_t * IDX_HEADS, 1),
                         pad_rows(ki), pool_ik, n_t=n_t, pages_per_chunk=pages_per_chunk)
    topk = min(TOPK_MAX, (past + n_t) // 4)
    s_sel, thr = _sample_select(s3d.reshape(n_rows, -1), topk=topk, n_fast=n_fast)
    o = _sample_attend(page_table, q_bd, s_sel.reshape(db, n_t, -1), thr.reshape(db, n_t, LANES),
                       pad_rows(k), pad_rows(v), pool_k.reshape(pool_k.shape[0], PAGE_SIZE, -1),
                       pool_v.reshape(pool_v.shape[0], PAGE_SIZE, -1), n_t=n_t, pages_per_chunk=pages_per_chunk)
    o6 = o.reshape(db, KV_HEADS, group, n_t, KV_HEADS, HEAD_DIM)
    attn = jnp.stack([o6[:, g, :, :, g, :] for g in range(KV_HEADS)], axis=1)
    attn = attn.transpose(0, 3, 1, 2, 4).reshape(n_rows, n_q).astype(BF16)

    ret, st = _retention_sample(rq, rkt[0], rv, rg, g_retn, state, db=db, n_t=n_t)
    y = _tail(x2d, attn, ret, p.reshape(n_rows, -1), w_out, g_ffn, w_ffn_in, w_ffn_out,
              g_ple, w_ple_gate, w_ple_proj, g_final, tm=n_rows)
    return (y.reshape(db, n_t, d), k.reshape(db, n_t, KV_HEADS, HEAD_DIM), v.reshape(db, n_t, KV_HEADS, HEAD_DIM),
            ki.reshape(db, n_t, IDX_DIM), st)


def kernel(x_prompt, x_sample, cache_k, cache_v, cache_idx_k, state_retn, page_table, p_prompt, p_sample, g_mix, w_in,
           g_retn, w_out, g_ffn, w_ffn_in, w_ffn_out, g_ple, w_ple_gate, w_ple_proj, g_final):
    depth = w_in.shape[0]
    assert depth == 1, "the final RMSNorm is fused into the single layer's tail kernel"
    weights = (g_mix[0], w_in[0], g_retn[0], w_out[0], g_ffn[0], w_ffn_in[0], w_ffn_out[0], g_ple[0], w_ple_gate[0],
               w_ple_proj[0], g_final)
    y_p, k_p, v_p, ik_p, st_p = _prompt_layer(x_prompt, p_prompt[0], weights)
    y_s, k_s, v_s, ik_s, st_s = _sample_layer(x_sample, p_sample[0], cache_k[0], cache_v[0], cache_idx_k[0],
                                              state_retn[0], page_table, weights)
    return (y_p, y_s, k_p[None], v_p[None], ik_p[None], st_p[None], k_s[None], v_s[None], ik_s[None], st_s[None])
```

```python
import functools

import jax
import jax.numpy as jnp
import numpy as np
from jax import lax
from jax.experimental import pallas as pl
from jax.experimental.pallas import tpu as pltpu

F32 = jnp.float32
BF16 = jnp.bfloat16

HEAD_DIM = 64
KV_HEADS = 2
IDX_HEADS = 8
IDX_DIM = 64
TOPK_MAX = 256
RET_DK = 128
RET_DV = 128
RET_CHUNK = 128
PAGE_SIZE = 128
ROPE_THETA = 10000.0
EPS = 1e-6
GN_EPS = 1e-5

LANES = 128
SUBLANES = 8
VMEM_LIMIT_BYTES = 56 * 1024 * 1024

NEG = -3.0e38
POS = 3.0e38
SOFTMAX_NEG = -1.0e30


def _cparams(*sem):
    return pltpu.CompilerParams(dimension_semantics=sem, vmem_limit_bytes=VMEM_LIMIT_BYTES)


def _const_spec(shape):
    zeros = (0,) * len(shape)
    return pl.BlockSpec(shape, lambda *_: zeros, pipeline_mode=pl.Buffered(1))


def _rope_tables(pos, dim):
    half = dim // 2
    inv = ROPE_THETA ** (-jnp.arange(half, dtype=F32) / half)
    ang = pos.astype(F32)[:, None] * inv[None, :]
    cos = jnp.cos(ang)
    sin = jnp.sin(ang)
    reps = LANES // dim
    cos_t = jnp.tile(jnp.concatenate([cos, cos], axis=1), (1, reps))
    sin_t = jnp.tile(jnp.concatenate([-sin, sin], axis=1), (1, reps))
    return cos_t, sin_t


def _swap_halves_64(x):
    lane = lax.broadcasted_iota(jnp.int32, x.shape, 1)
    first = (lane % 64) < 32
    return jnp.where(first, pltpu.roll(x, 96, 1), pltpu.roll(x, 32, 1))


def _pad_pair_slabs(x, swapped, low):
    return (jnp.where(low, x, 0.0), jnp.where(low, 0.0, swapped), jnp.where(low, swapped, 0.0), jnp.where(low, 0.0, x))


def _inproj_kernel(x_ref, g_ref, w_ref, c64_ref, s64_ref, c128_ref, s128_ref,
                   qt_ref, k_ref, v_ref, qit_ref, ki_ref, wit_ref, rq_ref, rkt_ref, rv_ref, rg_ref,
                   kp_ref, vtt_ref, kip_ref, *, n_q, n_qi, n_ret):
    x = x_ref[...]
    tm = x.shape[0]
    ms = jnp.mean(x * x, axis=-1, keepdims=True)
    a = ((x * lax.rsqrt(ms + EPS)) * g_ref[...]).astype(BF16)
    c64, s64 = c64_ref[...], s64_ref[...]
    c128, s128 = c128_ref[...], s128_ref[...]
    low = lax.broadcasted_iota(jnp.int32, (tm, LANES), 1) < HEAD_DIM
    zt = jnp.zeros((HEAD_DIM, tm), BF16)

    def proj(col):
        return jnp.dot(a, w_ref[:, col:col + LANES], preferred_element_type=F32)

    def rope64(y):
        return y * c64 + _swap_halves_64(y) * s64

    def rope128(y):
        return y * c128 + pltpu.roll(y, 64, 1) * s128

    col = 0
    for j in range(n_q // LANES):
        qt_ref[j * LANES:(j + 1) * LANES, :] = (rope64(proj(col)) * (HEAD_DIM ** -0.5)).T.astype(BF16)
        col += LANES
    kk = rope64(proj(col))
    col += LANES
    k_ref[...] = kk
    for s, slab in enumerate(_pad_pair_slabs(kk, pltpu.roll(kk, HEAD_DIM, 1), low)):
        kp_ref[:, s * LANES:(s + 1) * LANES] = slab.astype(BF16)
    vv = proj(col)
    col += LANES
    v_ref[...] = vv
    vvt = vv.T.astype(BF16)
    for g in range(KV_HEADS):
        vtg = vvt[g * HEAD_DIM:(g + 1) * HEAD_DIM, :]
        base = g * 4 * HEAD_DIM
        vtt_ref[base:base + HEAD_DIM, :] = vtg
        vtt_ref[base + HEAD_DIM:base + 2 * HEAD_DIM, :] = zt
        vtt_ref[base + 2 * HEAD_DIM:base + 3 * HEAD_DIM, :] = zt
        vtt_ref[base + 3 * HEAD_DIM:base + 4 * HEAD_DIM, :] = vtg
    for j in range(n_qi // LANES):
        qit_ref[j * LANES:(j + 1) * LANES, :] = (rope64(proj(col)) * (IDX_DIM ** -0.5)).T.astype(BF16)
        col += LANES
    y = proj(col)
    col += LANES
    ki = rope64(y)
    ki_ref[...] = ki[:, :IDX_DIM]
    ki_lo, ki_hi, _, _ = _pad_pair_slabs(ki, pltpu.roll(ki, IDX_DIM, 1), low)
    kip_ref[:, 0:LANES] = ki_lo.astype(BF16)
    kip_ref[:, LANES:2 * LANES] = ki_hi.astype(BF16)
    wit_ref[...] = y.T[IDX_DIM:IDX_DIM + IDX_HEADS, :] * (IDX_HEADS ** -0.5)
    for j in range(n_ret // LANES):
        rq_ref[:, j * LANES:(j + 1) * LANES] = rope128(proj(col)).astype(BF16)
        col += LANES
    for j in range(n_ret // LANES):
        rk = rope128(proj(col)) * (RET_DK ** -0.5)
        rkt_ref[j * LANES:(j + 1) * LANES, :] = rk.T.astype(BF16)
        col += LANES
    for j in range(n_ret // LANES):
        rv_ref[:, j * LANES:(j + 1) * LANES] = proj(col).astype(BF16)
        col += LANES
    for j in range(n_ret // LANES):
        rg_ref[:, j * LANES:(j + 1) * LANES] = proj(col)
        col += LANES


def _pack_w_in(w_in, n_q, n_kv, n_qi, n_ret):
    d = w_in.shape[0]
    sizes = (n_q, n_kv, n_kv, n_qi, IDX_DIM, IDX_HEADS, n_ret, n_ret, n_ret, n_ret)
    offs = np.concatenate([[0], np.cumsum(sizes)])
    seg = [w_in[:, offs[i]:offs[i + 1]] for i in range(len(sizes))]
    pad = jnp.zeros((d, LANES - IDX_DIM - IDX_HEADS), w_in.dtype)
    packed = jnp.concatenate(seg[:4] + [seg[4], seg[5], pad] + seg[6:], axis=1)
    return packed.astype(BF16)


def _inproj(x2d, pos, g_mix, w_packed, *, seq, tm, dims):
    n_q, n_kv, n_qi, n_ret = dims
    rows, d = x2d.shape
    nb = rows // seq
    spt = seq // tm
    c64, s64 = _rope_tables(pos, HEAD_DIM)
    c128, s128 = _rope_tables(pos, RET_DK)
    wcols = w_packed.shape[1]

    row = lambda w: pl.BlockSpec((tm, w), lambda i: (i, 0))
    tab = pl.BlockSpec((tm, LANES), lambda i: (i % spt, 0))
    tr = lambda h: pl.BlockSpec((None, h, tm), lambda i: (i // spt, 0, i % spt))
    out_shape = (
        jax.ShapeDtypeStruct((nb, n_q, seq), BF16),
        jax.ShapeDtypeStruct((rows, n_kv), F32),
        jax.ShapeDtypeStruct((rows, n_kv), F32),
        jax.ShapeDtypeStruct((nb, n_qi, seq), BF16),
        jax.ShapeDtypeStruct((rows, IDX_DIM), F32),
        jax.ShapeDtypeStruct((nb, IDX_HEADS, seq), F32),
        jax.ShapeDtypeStruct((rows, n_ret), BF16),
        jax.ShapeDtypeStruct((nb, n_ret, seq), BF16),
        jax.ShapeDtypeStruct((rows, n_ret), BF16),
        jax.ShapeDtypeStruct((rows, n_ret), F32),
        jax.ShapeDtypeStruct((rows, 4 * n_kv), BF16),
        jax.ShapeDtypeStruct((nb, 4 * n_kv, seq), BF16),
        jax.ShapeDtypeStruct((rows, 2 * LANES), BF16),
    )
    out_specs = (tr(n_q), row(n_kv), row(n_kv), tr(n_qi), row(IDX_DIM), tr(IDX_HEADS),
                 row(n_ret), tr(n_ret), row(n_ret), row(n_ret), row(4 * n_kv), tr(4 * n_kv), row(2 * LANES))
    return pl.pallas_call(
        functools.partial(_inproj_kernel, n_q=n_q, n_qi=n_qi, n_ret=n_ret),
        out_shape=out_shape,
        grid=(rows // tm,),
        in_specs=[row(d), _const_spec((1, d)), _const_spec((d, wcols)), tab, tab, tab, tab],
        out_specs=out_specs,
        compiler_params=_cparams("parallel"),
        name="inproj",
    )(x2d, g_mix.reshape(1, d), w_packed, c64, s64, c128, s128)


N_ACC = 4


def _fold_keys(st_ref, nch, ck, init, fn, combine):
    def body(c, carries):
        carries = list(carries)
        base = pl.multiple_of(c * ck, ck)
        chunk = st_ref[pl.ds(base, ck), :]
        for j in range(ck // SUBLANES):
            blk = chunk[j * SUBLANES:(j + 1) * SUBLANES, :]
            carries[j % N_ACC] = fn(carries[j % N_ACC], blk, base + j * SUBLANES)
        return tuple(carries)
    carries = lax.fori_loop(0, nch, body, (init,) * N_ACC)
    out = carries[0]
    for other in carries[1:]:
        out = combine(out, other)
    return out


def _all_sublanes(x, op):
    for shift in (4, 2, 1):
        x = op(x, pltpu.roll(x, shift, 0))
    return x


def _any(mask):
    return jnp.max(jnp.where(mask, 1.0, 0.0)) > 0.5


def _count(st_ref, nch, ck, pred):
    nq = st_ref.shape[1]
    acc = _fold_keys(st_ref, nch, ck, jnp.zeros((SUBLANES, nq), F32),
                     lambda a, blk, row0: a + jnp.where(pred(blk, row0), 1.0, 0.0), jnp.add)
    return _all_sublanes(acc, jnp.add)


def _select_threshold(st_ref, thr_ref, nch, ck, rmin, rmax, topk, n_fast):
    nq = st_ref.shape[1]
    kf = float(topk)
    full = lambda v: jnp.full((SUBLANES, nq), v, F32)

    def fast_body(_, carry):
        lo, hi = carry
        mid = lo + (hi - lo) * 0.5
        up = _count(st_ref, nch, ck, lambda blk, _r: blk > mid) >= kf
        return jnp.where(up, mid, lo), jnp.where(up, hi, mid)

    lo, hi = lax.fori_loop(0, n_fast, fast_body, (rmin, rmax))
    unres = _count(st_ref, nch, ck, lambda blk, _r: blk >= lo) > kf
    thr_ref[...] = lo

    @pl.when(_any(unres))
    def _exact():
        pair_minmax = lambda x, y: (jnp.minimum(x[0], y[0]), jnp.maximum(x[1], y[1]))
        a, b = _fold_keys(
            st_ref, nch, ck, (full(POS), full(NEG)),
            lambda cr, blk, _r: (jnp.minimum(cr[0], jnp.where(blk >= lo, blk, POS)),
                                 jnp.maximum(cr[1], jnp.where(blk <= hi, blk, NEG))),
            pair_minmax)
        lo2 = jnp.where(unres, _all_sublanes(a, jnp.minimum), lo)
        ub = jnp.where(unres, _all_sublanes(b, jnp.maximum), lo)

        def body(carry):
            lo2, ub, _ = carry
            mid = lo2 + (ub - lo2) * 0.5
            mid = jnp.where(mid >= ub, lo2, mid)

            def step(cr, blk, _r):
                gt = blk > mid
                return (cr[0] + jnp.where(gt, 1.0, 0.0), jnp.minimum(cr[1], jnp.where(gt, blk, POS)),
                        jnp.maximum(cr[2], jnp.where(gt, NEG, blk)))
            cnt, a, b = _fold_keys(st_ref, nch, ck, (full(0.0), full(POS), full(NEG)), step,
                                   lambda x, y: (x[0] + y[0],) + pair_minmax(x[1:], y[1:]))
            up = _all_sublanes(cnt, jnp.add) >= kf
            active = lo2 < ub
            lo2n = jnp.where(active & up, _all_sublanes(a, jnp.minimum), lo2)
            ubn = jnp.where(active & jnp.logical_not(up), _all_sublanes(b, jnp.maximum), ub)
            return lo2n, ubn, _any(lo2n < ubn).astype(jnp.int32)

        v, _, _ = lax.while_loop(lambda carry: carry[2] > 0, body, (lo2, ub, _any(lo2 < ub).astype(jnp.int32)))
        thr_ref[...] = jnp.where(unres, v, lo)
        tied = unres & (_count(st_ref, nch, ck, lambda blk, _r: blk >= v) > kf)

        @pl.when(_any(tied))
        def _ties():
            _drop_excess_ties(st_ref, nch, ck, v, tied, kf)


def _drop_excess_ties(st_ref, nch, ck, v, tied, kf):
    nq = st_ref.shape[1]
    full = lambda x: jnp.full((SUBLANES, nq), x, F32)
    need = kf - _count(st_ref, nch, ck, lambda blk, _r: blk > v)
    nkeys = st_ref.shape[0]
    sub = lax.broadcasted_iota(jnp.int32, (SUBLANES, nq), 0)
    key_index = lambda row0: (sub + row0).astype(F32)

    def j_body(_, carry):
        lo_j, hi_j = carry
        mid_j = jnp.floor((lo_j + hi_j) * 0.5)
        ok = _count(st_ref, nch, ck, lambda blk, row0: (blk == v) & (key_index(row0) <= mid_j)) >= need
        return jnp.where(ok, lo_j, mid_j), jnp.where(ok, mid_j, hi_j)

    n_j = int(np.ceil(np.log2(nkeys))) + 1
    _, j_last = lax.fori_loop(0, n_j, j_body, (full(-1.0), full(float(nkeys - 1))))

    def drop_body(c, _):
        base = pl.multiple_of(c * ck, ck)
        for j in range(ck // SUBLANES):
            rows = pl.ds(base + j * SUBLANES, SUBLANES)
            blk = st_ref[rows, :]
            drop = tied & (blk == v) & (key_index(base + j * SUBLANES) > j_last)
            st_ref[rows, :] = jnp.where(drop, NEG, blk)
        return 0
    lax.fori_loop(0, nch, drop_body, 0)


def _rows_min(x):
    return jnp.min(x.reshape(x.shape[0] // SUBLANES, SUBLANES, x.shape[1]), axis=0)


def _rows_max(x):
    return jnp.max(x.reshape(x.shape[0] // SUBLANES, SUBLANES, x.shape[1]), axis=0)


def _rows_sum(x):
    return jnp.sum(x.reshape(x.shape[0] // SUBLANES, SUBLANES, x.shape[1]), axis=0)


def _dsa_prompt_kernel(kip_ref, qit_ref, wit_ref, kp_ref, qt_ref, vtt_ref, o_ref, st_ref, thr_ref,
                       m_ref, l_ref, acc_ref, lg_ref, pr_ref, *, tq, topk, n_fast, rs):
    ck = tq
    i = pl.program_id(1)
    nch = i + 1
    n_pairs = qt_ref.shape[0] // LANES
    pairs_per_kv = n_pairs // KV_HEADS
    bcast = lambda row, n: jnp.broadcast_to(row, (n, tq))

    key_l = lax.broadcasted_iota(jnp.int32, (rs, tq), 0)
    qry_l = lax.broadcasted_iota(jnp.int32, (rs, tq), 1)
    w_rows = [bcast(wit_ref[h:h + 1, :], rs) for h in range(IDX_HEADS)]

    def score_body(c, carry):
        mn, mx = carry
        base = pl.multiple_of(c * ck, ck)
        off_diag = c < i
        for r in range(ck // rs):
            rows = pl.ds(base + r * rs, rs)
            k_lo = kip_ref[rows, 0:LANES]
            k_hi = kip_ref[rows, LANES:2 * LANES]
            acc = jnp.zeros((rs, tq), F32)
            for p in range(IDX_HEADS // 2):
                rhs = qit_ref[p * LANES:(p + 1) * LANES, :]
                acc = acc + jnp.maximum(jnp.dot(k_lo, rhs, preferred_element_type=F32), 0.0) * w_rows[2 * p]
                acc = acc + jnp.maximum(jnp.dot(k_hi, rhs, preferred_element_type=F32), 0.0) * w_rows[2 * p + 1]
            valid = off_diag | (key_l + r * rs <= qry_l)
            st_ref[rows, :] = jnp.where(valid, acc, NEG)
            mn = jnp.minimum(mn, _rows_min(jnp.where(valid, acc, POS)))
            mx = jnp.maximum(mx, _rows_max(jnp.where(valid, acc, NEG)))
        return mn, mx

    mn, mx = lax.fori_loop(0, nch, score_body,
                           (jnp.full((SUBLANES, tq), POS, F32), jnp.full((SUBLANES, tq), NEG, F32)))
    _select_threshold(st_ref, thr_ref, nch, ck, _all_sublanes(mn, jnp.minimum), _all_sublanes(mx, jnp.maximum),
                      topk, n_fast)

    thr = bcast(thr_ref[0:1, :], ck)
    m_ref[...] = jnp.full(m_ref.shape, SOFTMAX_NEG, F32)
    l_ref[...] = jnp.zeros(l_ref.shape, F32)
    acc_ref[...] = jnp.zeros(acc_ref.shape, F32)

    def att_body(c, _):
        rows = pl.ds(pl.multiple_of(c * ck, ck), ck)
        sel = st_ref[rows, :] >= thr
        slab_of = lambda h: 2 * (h // 2 // pairs_per_kv) + h % 2
        n_heads = 2 * n_pairs
        for h in range(n_heads):
            slab = slab_of(h)
            kx = kp_ref[rows, slab * LANES:(slab + 1) * LANES]
            qt_pair = qt_ref[(h // 2) * LANES:(h // 2 + 1) * LANES, :]
            lg_ref[h] = jnp.where(sel, jnp.dot(kx, qt_pair, preferred_element_type=F32), SOFTMAX_NEG)
        alpha = []
        for h in range(n_heads):
            lg = lg_ref[h]
            m_old = m_ref[h]
            m_new = jnp.maximum(m_old, _all_sublanes(_rows_max(lg), jnp.maximum))
            a = jnp.exp(m_old - m_new)
            pr = jnp.exp(lg - bcast(m_new[0:1, :], ck))
            m_ref[h] = m_new
            l_ref[h] = a * l_ref[h] + _all_sublanes(_rows_sum(pr), jnp.add)
            pr_ref[h] = pr.astype(BF16)
            alpha.append(bcast(a[0:1, :], HEAD_DIM))
        for p in range(n_pairs):
            pv = [jnp.dot(vtt_ref[slab_of(h) * LANES:(slab_of(h) + 1) * LANES, rows], pr_ref[h],
                          preferred_element_type=F32) for h in (2 * p, 2 * p + 1)]
            acc_ref[p] = acc_ref[p] * jnp.concatenate(alpha[2 * p:2 * p + 2], axis=0) + pv[0] + pv[1]
        return 0

    lax.fori_loop(0, nch, att_body, 0)
    for p in range(n_pairs):
        denom = jnp.concatenate([bcast(l_ref[2 * p, 0:1, :], HEAD_DIM), bcast(l_ref[2 * p + 1, 0:1, :], HEAD_DIM)],
                                axis=0)
        o_ref[:, p * LANES:(p + 1) * LANES] = (acc_ref[p] / denom).T.astype(o_ref.dtype)


def _dsa_prompt(kip, qit, wit, kp, qt, vtt, *, nb, seq, tq, topk, n_fast):
    n_q = qt.shape[1]
    spb = seq // tq
    per_b_rows = lambda a: pl.BlockSpec((seq, a.shape[1]), lambda b, i: (b, 0))
    q_cols = lambda a: pl.BlockSpec((None, a.shape[1], tq), lambda b, i: (b, 0, i))
    return pl.pallas_call(
        functools.partial(_dsa_prompt_kernel, tq=tq, topk=topk, n_fast=n_fast, rs=min(tq, 128)),
        out_shape=jax.ShapeDtypeStruct((nb * seq, n_q), BF16),
        grid=(nb, spb),
        in_specs=[per_b_rows(kip), q_cols(qit), q_cols(wit), per_b_rows(kp), q_cols(qt),
                  pl.BlockSpec((None, vtt.shape[1], seq), lambda b, i: (b, 0, 0))],
        out_specs=pl.BlockSpec((tq, n_q), lambda b, i: (b * spb + i, 0)),
        scratch_shapes=[pltpu.VMEM((seq, tq), F32), pltpu.VMEM((SUBLANES, tq), F32),
                        pltpu.VMEM((n_q // HEAD_DIM, SUBLANES, tq), F32),
                        pltpu.VMEM((n_q // HEAD_DIM, SUBLANES, tq), F32),
                        pltpu.VMEM((n_q // LANES, LANES, tq), F32),
                        pltpu.VMEM((n_q // HEAD_DIM, tq, tq), F32),
                        pltpu.VMEM((n_q // HEAD_DIM, tq, tq), BF16)],
        compiler_params=_cparams("parallel", "arbitrary"),
        name="dsa_prompt",
    )(kip, qit, wit, kp, qt, vtt)


def _retention_tables(chunk, n_heads):
    log_g = jnp.log1p(-jnp.exp2(-5.0 - jnp.arange(n_heads, dtype=F32)))
    i = jnp.arange(chunk, dtype=F32)
    diff = i[:, None] - i[None, :]
    decay = jnp.where(diff >= 0, jnp.exp(log_g[:, None, None] * jnp.maximum(diff, 0.0)), 0.0)
    q_dec = jnp.exp(log_g[:, None] * (i[None, :] + 1.0))
    k_dec = jnp.exp(log_g[:, None] * (chunk - 1.0 - i)[None, :])
    c_dec = jnp.exp(log_g * chunk)
    return decay, q_dec, k_dec, c_dec


def _group_norm_gate(o, rg, gain):
    mu = jnp.mean(o, axis=-1, keepdims=True)
    var = jnp.mean(jnp.square(o - mu), axis=-1, keepdims=True)
    on = ((o - mu) * lax.rsqrt(var + GN_EPS)) * gain
    return jax.nn.silu(rg) * on


def _retention_prompt_kernel(rq_ref, rkt_ref, rv_ref, rg_ref, gain_ref, dmat_ref, qdec_ref, kdec_ref, cdec_ref,
                             o_ref, st_ref, state_ref, *, chunk, n_heads):
    j = pl.program_id(1)

    @pl.when(j == 0)
    def _():
        state_ref[...] = jnp.zeros_like(state_ref)

    for cc in range(rq_ref.shape[0] // chunk):
        rows = slice(cc * chunk, (cc + 1) * chunk)
        for h in range(n_heads):
            cols = slice(h * RET_DV, (h + 1) * RET_DV)
            q = rq_ref[rows, cols]
            kt = rkt_ref[cols, rows]
            v = rv_ref[rows, cols]
            state = state_ref[h]
            inner = jnp.dot(q, kt, preferred_element_type=F32) * dmat_ref[h]
            o = jnp.dot(inner.astype(BF16), v, preferred_element_type=F32)
            o = o + jnp.dot(q, state.astype(BF16), preferred_element_type=F32) * qdec_ref[h]
            kd = (kt.astype(F32) * kdec_ref[h]).astype(BF16)
            state_ref[h] = cdec_ref[h] * state + jnp.dot(kd, v, preferred_element_type=F32)
            o_ref[rows, cols] = _group_norm_gate(o, rg_ref[rows, cols], gain_ref[:, cols]).astype(o_ref.dtype)

    @pl.when(j == pl.num_programs(1) - 1)
    def _():
        st_ref[...] = state_ref[...]


def _retention_prompt(rq, rkt, rv, rg, g_retn, *, nb, seq, tr):
    n_ret = rq.shape[1]
    n_heads = n_ret // RET_DV
    chunk = RET_CHUNK
    decay, q_dec, k_dec, c_dec = _retention_tables(chunk, n_heads)
    qdec_b = jnp.broadcast_to(q_dec[:, :, None], (n_heads, chunk, LANES))
    kdec_b = k_dec[:, None, :]
    cdec_b = jnp.broadcast_to(c_dec[:, None, None], (n_heads, 1, LANES))
    spb = seq // tr
    row = lambda: pl.BlockSpec((tr, n_ret), lambda b, j: (b * spb + j, 0))
    return pl.pallas_call(
        functools.partial(_retention_prompt_kernel, chunk=chunk, n_heads=n_heads),
        out_shape=(jax.ShapeDtypeStruct((nb * seq, n_ret), BF16),
                   jax.ShapeDtypeStruct((nb, n_heads, RET_DK, RET_DV), F32)),
        grid=(nb, spb),
        in_specs=[row(), pl.BlockSpec((None, n_ret, tr), lambda b, j: (b, 0, j)), row(), row(),
                  _const_spec((1, n_ret)), _const_spec((n_heads, chunk, chunk)),
                  _const_spec((n_heads, chunk, LANES)), _const_spec((n_heads, 1, chunk)),
                  _const_spec((n_heads, 1, LANES))],
        out_specs=(row(), pl.BlockSpec((None, n_heads, RET_DK, RET_DV), lambda b, j: (b, 0, 0, 0))),
        scratch_shapes=[pltpu.VMEM((n_heads, RET_DK, RET_DV), F32)],
        compiler_params=_cparams("parallel", "arbitrary"),
        name="retention_prompt",
    )(rq, rkt, rv, rg, g_retn.reshape(1, n_ret), decay, qdec_b, kdec_b, cdec_b)


def _rms(x, g):
    return (x * lax.rsqrt(jnp.mean(x * x, axis=-1, keepdims=True) + EPS)) * g


def _tail_kernel(h_ref, attn_ref, ret_ref, p_ref, wo_a_ref, wo_r_ref, g_ffn_ref, w_gate_ref, w_up_ref, w_down_ref,
                 g_ple_ref, w_pg_ref, w_pp_ref, g_fin_ref, y_ref):
    dot = functools.partial(jnp.dot, preferred_element_type=F32)
    h = h_ref[...] + dot(attn_ref[...], wo_a_ref[...]) + dot(ret_ref[...], wo_r_ref[...])
    f = _rms(h, g_ffn_ref[...]).astype(BF16)
    act = (jax.nn.silu(dot(f, w_gate_ref[...])) * dot(f, w_up_ref[...])).astype(BF16)
    h = h + dot(act, w_down_ref[...])
    gate = jax.nn.sigmoid(dot(_rms(h, g_ple_ref[...]).astype(BF16), w_pg_ref[...]))
    h = h + gate * dot(p_ref[...].astype(BF16), w_pp_ref[...])
    y_ref[...] = _rms(h, g_fin_ref[...])


def _tail(h2d, attn, ret, p2d, w_out, g_ffn, w_ffn_in, w_ffn_out, g_ple, w_ple_gate, w_ple_proj, g_final, *, tm):
    rows, d = h2d.shape
    n_attn = attn.shape[1]
    d_ff = w_ffn_out.shape[0]
    ws = [w_out[:n_attn].astype(BF16), w_out[n_attn:].astype(BF16), g_ffn.reshape(1, d),
          w_ffn_in[:, :d_ff].astype(BF16), w_ffn_in[:, d_ff:].astype(BF16), w_ffn_out.astype(BF16),
          g_ple.reshape(1, d), w_ple_gate.astype(BF16), w_ple_proj.astype(BF16), g_final.reshape(1, d)]
    row = lambda w: pl.BlockSpec((tm, w), lambda i: (i, 0))
    return pl.pallas_call(
        _tail_kernel,
        out_shape=jax.ShapeDtypeStruct((rows, d), F32),
        grid=(rows // tm,),
        in_specs=[row(d), row(n_attn), row(ret.shape[1]), row(p2d.shape[1])] + [_const_spec(w.shape) for w in ws],
        out_specs=row(d),
        compiler_params=_cparams("parallel"),
        name="tail",
    )(h2d, attn, ret, p2d, *ws)


def _page_copies(pools, bufs, sems, pt_ref, b, c, slot, pages_per_chunk):
    copies = []
    for pool, buf in zip(pools, bufs):
        for p in range(pages_per_chunk):
            page = pt_ref[b, c * pages_per_chunk + p]
            copies.append(pltpu.make_async_copy(pool.at[page], buf.at[slot, pl.ds(p * PAGE_SIZE, PAGE_SIZE)],
                                                sems.at[slot]))
    return copies


def _stream_chunks(pools, bufs, sems, pt_ref, n_chunks, pages_per_chunk, compute):
    b = pl.program_id(0)
    nb = pl.num_programs(0)

    def start(bb, cc, slot):
        for cp in _page_copies(pools, bufs, sems, pt_ref, bb, cc, slot, pages_per_chunk):
            cp.start()

    @pl.when(b == 0)
    def _():
        start(0, 0, 0)

    assert n_chunks % 2 == 0
    for c in range(n_chunks):
        slot = c % 2
        if c + 1 < n_chunks:
            start(b, c + 1, 1 - slot)
        else:
            @pl.when(b + 1 < nb)
            def _():
                start(b + 1, 0, 1 - slot)
        for cp in _page_copies(pools, bufs, sems, pt_ref, b, c, slot, pages_per_chunk):
            cp.wait()
        compute(c, slot)


def _tile_lanes(x, width):
    return x if width == LANES else jnp.concatenate([x] * (width // LANES), axis=1)


def _nt_dot(a, b):
    return lax.dot_general(a, b, (((1,), (1,)), ((), ())), preferred_element_type=F32)


def _head_sum(x, n_t):
    return jnp.sum(x.reshape(n_t, IDX_HEADS, x.shape[-1]), axis=1)


def _sample_scores_kernel(pt_ref, qi_ref, wi_ref, kinew_ref, pool_ref, s_ref, buf, sems,
                          *, n_t, n_chunks, pages_per_chunk):
    ck = pages_per_chunk * PAGE_SIZE
    qi = qi_ref[...]
    w = wi_ref[...]

    def compute(c, slot):
        a = _nt_dot(qi, buf[slot].astype(BF16))
        s_ref[:, c * ck:(c + 1) * ck] = _head_sum(jnp.maximum(a, 0.0) * w, n_t)

    _stream_chunks([pool_ref], [buf], sems, pt_ref, n_chunks, pages_per_chunk, compute)

    a = _nt_dot(qi, kinew_ref[...])
    s_new = _head_sum(jnp.maximum(a, 0.0) * w, n_t)
    t_q = lax.broadcasted_iota(jnp.int32, s_new.shape, 0)
    t_k = lax.broadcasted_iota(jnp.int32, s_new.shape, 1)
    s_ref[:, n_chunks * ck:] = jnp.where(t_k <= t_q, s_new, NEG)


def _sample_scores(page_table, qi_s, wi_s, kinew, pool_ik, *, n_t, pages_per_chunk):
    db, n_pages = page_table.shape
    n_chunks = n_pages // pages_per_chunk
    past = n_pages * PAGE_SIZE
    rows = n_t * IDX_HEADS
    grid_spec = pltpu.PrefetchScalarGridSpec(
        num_scalar_prefetch=1,
        grid=(db,),
        in_specs=[pl.BlockSpec((None, rows, IDX_DIM), lambda b, pt: (b, 0, 0)),
                  pl.BlockSpec((None, rows, 1), lambda b, pt: (b, 0, 0)),
                  pl.BlockSpec((None, PAGE_SIZE, IDX_DIM), lambda b, pt: (b, 0, 0)),
                  pl.BlockSpec(memory_space=pl.ANY)],
        out_specs=pl.BlockSpec((None, n_t, past + PAGE_SIZE), lambda b, pt: (b, 0, 0)),
        scratch_shapes=[pltpu.VMEM((2, pages_per_chunk * PAGE_SIZE, IDX_DIM), F32),
                        pltpu.SemaphoreType.DMA((2,))],
    )
    return pl.pallas_call(
        functools.partial(_sample_scores_kernel, n_t=n_t, n_chunks=n_chunks, pages_per_chunk=pages_per_chunk),
        out_shape=jax.ShapeDtypeStruct((db, n_t, past + PAGE_SIZE), F32),
        grid_spec=grid_spec,
        compiler_params=_cparams("arbitrary"),
        name="sample_scores",
    )(page_table, qi_s, wi_s, kinew, pool_ik)


def _sample_select_kernel(s_in_ref, s_out_ref, thr_ref, st_ref, thr8_ref, *, ck, topk, n_fast):
    nq, nkeys = s_in_ref.shape
    nch = nkeys // ck + 0 * pl.program_id(0)
    chunk = lambda c: pl.ds(pl.multiple_of(c * ck, ck), ck)

    def to_lanes(c, _):
        st_ref[chunk(c), :] = s_in_ref[:, chunk(c)].T
        return 0
    lax.fori_loop(0, nch, to_lanes, 0)

    mn, mx = _fold_keys(
        st_ref, nch, ck, (jnp.full((SUBLANES, nq), POS, F32), jnp.full((SUBLANES, nq), NEG, F32)),
        lambda cr, blk, _r: (jnp.minimum(cr[0], jnp.where(blk > 0.5 * NEG, blk, POS)), jnp.maximum(cr[1], blk)),
        lambda x, y: (jnp.minimum(x[0], y[0]), jnp.maximum(x[1], y[1])))
    _select_threshold(st_ref, thr8_ref, nch, ck, _all_sublanes(mn, jnp.minimum), _all_sublanes(mx, jnp.maximum),
                      topk, n_fast)
    def to_rows(c, _):
        s_out_ref[:, chunk(c)] = st_ref[chunk(c), :].T
        return 0
    lax.fori_loop(0, nch, to_rows, 0)
    thr_ref[...] = jnp.broadcast_to(thr8_ref[0:1, :], (LANES, nq)).T


def _sample_select(s2d, *, topk, n_fast):
    rows, ncols = s2d.shape
    ck = max(w for w in (LANES, 2 * LANES, 3 * LANES, 4 * LANES) if ncols % w == 0)
    full = pl.BlockSpec((rows, ncols), lambda i: (0, 0), pipeline_mode=pl.Buffered(1))
    return pl.pallas_call(
        functools.partial(_sample_select_kernel, ck=ck, topk=topk, n_fast=n_fast),
        out_shape=(jax.ShapeDtypeStruct((rows, ncols), F32), jax.ShapeDtypeStruct((rows, LANES), F32)),
        grid=(1,),
        in_specs=[full],
        out_specs=(full, pl.BlockSpec((rows, LANES), lambda i: (0, 0))),
        scratch_shapes=[pltpu.VMEM((ncols, rows), F32), pltpu.VMEM((SUBLANES, rows), F32)],
        compiler_params=_cparams("arbitrary"),
        name="sample_select",
    )(s2d)


def _sample_attend_kernel(pt_ref, q_ref, s_ref, thr_ref, knew_ref, vnew_ref, poolk_ref, poolv_ref, o_ref,
                          kbuf, vbuf, sems, *, n_t, n_chunks, pages_per_chunk):
    ck = pages_per_chunk * PAGE_SIZE
    q = q_ref[...]
    rows = q.shape[0]
    reps = rows // n_t
    thr = jnp.concatenate([thr_ref[...]] * reps, axis=0)
    state = [jnp.full((rows, 1), SOFTMAX_NEG, F32), jnp.zeros((rows, 1), F32), jnp.zeros((rows, LANES), F32)]

    def attend(kb, vb, s_blk):
        m, l, acc = state
        width = s_blk.shape[1]
        sel = jnp.concatenate([s_blk] * reps, axis=0) >= _tile_lanes(thr, width)
        lg = jnp.where(sel, _nt_dot(q, kb), SOFTMAX_NEG)
        m_new = jnp.maximum(m, jnp.max(lg, axis=1, keepdims=True))
        alpha = jnp.exp(m - m_new)
        pr = jnp.exp(lg - m_new)
        state[0] = m_new
        state[1] = alpha * l + jnp.sum(pr, axis=1, keepdims=True)
        state[2] = alpha * acc + jnp.dot(pr.astype(BF16), vb, preferred_element_type=F32)

    def compute(c, slot):
        attend(kbuf[slot].astype(BF16), vbuf[slot].astype(BF16), s_ref[:, c * ck:(c + 1) * ck])

    _stream_chunks([poolk_ref, poolv_ref], [kbuf, vbuf], sems, pt_ref, n_chunks, pages_per_chunk, compute)
    attend(knew_ref[...], vnew_ref[...], s_ref[:, n_chunks * ck:])
    o_ref[...] = state[2] / state[1]


def _sample_attend(page_table, q_bd, s3d, thr3d, knew, vnew, pool_k, pool_v, *, n_t, pages_per_chunk):
    db, n_pages = page_table.shape
    n_chunks = n_pages // pages_per_chunk
    rows = q_bd.shape[1]
    ncols = s3d.shape[2]
    n_kv = pool_k.shape[2]
    per_b = lambda r, w: pl.BlockSpec((None, r, w), lambda b, pt: (b, 0, 0))
    grid_spec = pltpu.PrefetchScalarGridSpec(
        num_scalar_prefetch=1,
        grid=(db,),
        in_specs=[per_b(rows, n_kv), per_b(n_t, ncols), per_b(n_t, LANES), per_b(PAGE_SIZE, n_kv),
                  per_b(PAGE_SIZE, n_kv), pl.BlockSpec(memory_space=pl.ANY), pl.BlockSpec(memory_space=pl.ANY)],
        out_specs=per_b(rows, n_kv),
        scratch_shapes=[pltpu.VMEM((2, pages_per_chunk * PAGE_SIZE, n_kv), F32),
                        pltpu.VMEM((2, pages_per_chunk * PAGE_SIZE, n_kv), F32),
                        pltpu.SemaphoreType.DMA((2,))],
    )
    return pl.pallas_call(
        functools.partial(_sample_attend_kernel, n_t=n_t, n_chunks=n_chunks, pages_per_chunk=pages_per_chunk),
        out_shape=jax.ShapeDtypeStruct((db, rows, n_kv), F32),
        grid_spec=grid_spec,
        compiler_params=_cparams("arbitrary"),
        name="sample_attend",
    )(page_table, q_bd, s3d, thr3d, knew, vnew, pool_k, pool_v)


def _retention_sample_kernel(rq_ref, rkt_ref, rv_ref, rg_ref, gain_ref, dfull_ref, qdec_ref, kdec_ref, cdec_ref,
                             st_in_ref, o_ref, st_out_ref, *, n_t, n_heads):
    b = pl.program_id(0)

    @pl.when(b == 0)
    def _():
        o_ref[...] = jnp.zeros_like(o_ref)

    n_rows = rq_ref.shape[0]
    row = lax.broadcasted_iota(jnp.int32, (n_rows, RET_DV), 0)
    mine = (row >= b * n_t) & (row < (b + 1) * n_t)
    for h in range(n_heads):
        cols = slice(h * RET_DV, (h + 1) * RET_DV)
        q = jnp.where(mine, rq_ref[:, cols], 0.0).astype(BF16)
        v = rv_ref[:, cols]
        kt = rkt_ref[cols, :]
        state = st_in_ref[h]
        inner = jnp.dot(q, kt, preferred_element_type=F32) * dfull_ref[h]
        o = jnp.dot(inner.astype(BF16), v, preferred_element_type=F32)
        o = o + jnp.dot(q, state.astype(BF16), preferred_element_type=F32) * qdec_ref[h]
        vd = jnp.where(mine, v.astype(F32) * kdec_ref[h], 0.0).astype(BF16)
        st_out_ref[h] = cdec_ref[h] * state + jnp.dot(kt, vd, preferred_element_type=F32)
        gated = _group_norm_gate(o, rg_ref[:, cols], gain_ref[:, cols])
        o_ref[:, cols] = jnp.where(mine, gated, o_ref[:, cols].astype(F32)).astype(o_ref.dtype)


def _retention_sample(rq, rkt, rv, rg, g_retn, state, *, db, n_t):
    n_rows, n_ret = rq.shape
    n_heads = n_ret // RET_DV
    decay, q_dec, k_dec, c_dec = _retention_tables(n_t, n_heads)
    same = jnp.kron(jnp.eye(db, dtype=F32), jnp.ones((n_t, n_t), F32))
    dfull = same[None] * jnp.tile(decay, (1, db, db))
    qdec_b = jnp.broadcast_to(jnp.tile(q_dec, (1, db))[:, :, None], (n_heads, n_rows, RET_DV))
    kdec_b = jnp.broadcast_to(jnp.tile(k_dec, (1, db))[:, :, None], (n_heads, n_rows, RET_DV))
    cdec_b = jnp.broadcast_to(c_dec[:, None, None], (n_heads, 1, RET_DV))
    const = lambda shape: pl.BlockSpec(shape, lambda b: (0,) * len(shape))
    st_spec = pl.BlockSpec((None, n_heads, RET_DK, RET_DV), lambda b: (b, 0, 0, 0))
    return pl.pallas_call(
        functools.partial(_retention_sample_kernel, n_t=n_t, n_heads=n_heads),
        out_shape=(jax.ShapeDtypeStruct((n_rows, n_ret), BF16),
                   jax.ShapeDtypeStruct((db, n_heads, RET_DK, RET_DV), F32)),
        grid=(db,),
        in_specs=[const((n_rows, n_ret)), const((n_ret, n_rows)), const((n_rows, n_ret)), const((n_rows, n_ret)),
                  const((1, n_ret)), const((n_heads, n_rows, n_rows)), const((n_heads, n_rows, RET_DV)),
                  const((n_heads, n_rows, RET_DV)), const((n_heads, 1, RET_DV)), st_spec],
        out_specs=(const((n_rows, n_ret)), st_spec),
        compiler_params=_cparams("arbitrary"),
        name="retention_sample",
    )(rq, rkt, rv, rg, g_retn.reshape(1, n_ret), dfull, qdec_b, kdec_b, cdec_b, state)


def _mixer_dims(w_in, d_model):
    n_q = d_model // 2
    n_ret = d_model // 2
    n_kv = KV_HEADS * HEAD_DIM
    n_qi = IDX_HEADS * IDX_DIM
    assert n_kv == LANES and 2 * n_kv + n_q + n_qi + IDX_DIM + IDX_HEADS + 4 * n_ret == w_in.shape[1]
    return n_q, n_kv, n_qi, n_ret


def _prompt_layer(x, p, weights, *, n_fast=24):
    g_mix, w_in, g_retn, w_out, g_ffn, w_ffn_in, w_ffn_out, g_ple, w_ple_gate, w_ple_proj, g_final = weights
    nb, seq, d = x.shape
    dims = _mixer_dims(w_in, d)
    w_packed = _pack_w_in(w_in, *dims)
    x2d = x.reshape(nb * seq, d)
    tm = min(512, seq)
    (qt, k, v, qit, ki, wit, rq, rkt, rv, rg, kp, vtt, kip) = _inproj(
        x2d, jnp.arange(seq), g_mix, w_packed, seq=seq, tm=tm, dims=dims)
    topk = min(TOPK_MAX, seq // 4)
    attn = _dsa_prompt(kip, qit, wit, kp, qt, vtt, nb=nb, seq=seq, tq=min(256, seq), topk=topk, n_fast=n_fast)
    ret, st = _retention_prompt(rq, rkt, rv, rg, g_retn, nb=nb, seq=seq, tr=min(512, seq))
    y = _tail(x2d, attn, ret, p.reshape(nb * seq, -1), w_out, g_ffn, w_ffn_in, w_ffn_out,
              g_ple, w_ple_gate, w_ple_proj, g_final, tm=min(256, nb * seq))
    return (y.reshape(nb, seq, d), k.reshape(nb, seq, KV_HEADS, HEAD_DIM), v.reshape(nb, seq, KV_HEADS, HEAD_DIM),
            ki.reshape(nb, seq, IDX_DIM), st)


def _sample_layer(x, p, pool_k, pool_v, pool_ik, state, page_table, weights, *, n_fast=24):
    g_mix, w_in, g_retn, w_out, g_ffn, w_ffn_in, w_ffn_out, g_ple, w_ple_gate, w_ple_proj, g_final = weights
    db, n_t, d = x.shape
    n_rows = db * n_t
    n_pages = page_table.shape[1]
    past = n_pages * PAGE_SIZE
    dims = _mixer_dims(w_in, d)
    n_q = dims[0]
    w_packed = _pack_w_in(w_in, *dims)
    x2d = x.reshape(n_rows, d)
    pos = jnp.tile(past + jnp.arange(n_t), db)
    (qt, k, v, qit, ki, wit, rq, rkt, rv, rg, _, _, _) = _inproj(
        x2d, pos, g_mix, w_packed, seq=n_rows, tm=n_rows, dims=dims)
    q, qi, wi = qt[0].T, qit[0].T, wit[0].T

    pad_rows = lambda a: jnp.pad(a.reshape(db, n_t, -1), ((0, 0), (0, PAGE_SIZE - n_t), (0, 0))).astype(BF16)
    group = n_q // HEAD_DIM // KV_HEADS
    qg = q.reshape(db, n_t, KV_HEADS, group, HEAD_DIM).transpose(0, 2, 3, 1, 4)
    zq = jnp.zeros_like(qg[:, 0])
    q_bd = jnp.stack([jnp.concatenate([qg[:, 0], zq], axis=-1), jnp.concatenate([zq, qg[:, 1]], axis=-1)], axis=1)
    q_bd = q_bd.reshape(db, KV_HEADS * group * n_t, KV_HEADS * HEAD_DIM)
    pages_per_chunk = max(1, min(16, n_pages // 2))

    s3d = _sample_scores(page_table, qi.reshape(db, n_t * IDX_HEADS, IDX_DIM), wi.reshape(db, n_t * IDX_HEADS, 1),
                         pad_rows(ki), pool_ik, n_t=n_t, pages_per_chunk=pages_per_chunk)
    topk = min(TOPK_MAX, (past + n_t) // 4)
    s_sel, thr = _sample_select(s3d.reshape(n_rows, -1), topk=topk, n_fast=n_fast)
    o = _sample_attend(page_table, q_bd, s_sel.reshape(db, n_t, -1), thr.reshape(db, n_t, LANES),
                       pad_rows(k), pad_rows(v), pool_k.reshape(pool_k.shape[0], PAGE_SIZE, -1),
                       pool_v.reshape(pool_v.shape[0], PAGE_SIZE, -1), n_t=n_t, pages_per_chunk=pages_per_chunk)
    o6 = o.reshape(db, KV_HEADS, group, n_t, KV_HEADS, HEAD_DIM)
    attn = jnp.stack([o6[:, g, :, :, g, :] for g in range(KV_HEADS)], axis=1)
    attn = attn.transpose(0, 3, 1, 2, 4).reshape(n_rows, n_q).astype(BF16)

    ret, st = _retention_sample(rq, rkt[0], rv, rg, g_retn, state, db=db, n_t=n_t)
    y = _tail(x2d, attn, ret, p.reshape(n_rows, -1), w_out, g_ffn, w_ffn_in, w_ffn_out,
              g_ple, w_ple_gate, w_ple_proj, g_final, tm=n_rows)
    return (y.reshape(db, n_t, d), k.reshape(db, n_t, KV_HEADS, HEAD_DIM), v.reshape(db, n_t, KV_HEADS, HEAD_DIM),
            ki.reshape(db, n_t, IDX_DIM), st)


def kernel(x_prompt, x_sample, cache_k, cache_v, cache_idx_k, state_retn, page_table, p_prompt, p_sample, g_mix, w_in,
           g_retn, w_out, g_ffn, w_ffn_in, w_ffn_out, g_ple, w_ple_gate, w_ple_proj, g_final):
    depth = w_in.shape[0]
    assert depth == 1, "the final RMSNorm is fused into the single layer's tail kernel"
    weights = (g_mix[0], w_in[0], g_retn[0], w_out[0], g_ffn[0], w_ffn_in[0], w_ffn_out[0], g_ple[0], w_ple_gate[0],
               w_ple_proj[0], g_final)
    y_p, k_p, v_p, ik_p, st_p = _prompt_layer(x_prompt, p_prompt[0], weights)
    y_s, k_s, v_s, ik_s, st_s = _sample_layer(x_sample, p_sample[0], cache_k[0], cache_v[0], cache_idx_k[0],
                                              state_retn[0], page_table, weights)
    return (y_p, y_s, k_p[None], v_p[None], ik_p[None], st_p[None], k_s[None], v_s[None], ik_s[None], st_s[None])
```

```python
import functools

import jax
import jax.numpy as jnp
import numpy as np
from jax import lax
from jax.experimental import pallas as pl
from jax.experimental.pallas import tpu as pltpu

F32 = jnp.float32
BF16 = jnp.bfloat16

HEAD_DIM = 64
KV_HEADS = 2
IDX_HEADS = 8
IDX_DIM = 64
TOPK_MAX = 256
RET_DK = 128
RET_DV = 128
RET_CHUNK = 128
PAGE_SIZE = 128
ROPE_THETA = 10000.0
EPS = 1e-6
GN_EPS = 1e-5

LANES = 128
SUBLANES = 8
VMEM_LIMIT_BYTES = 56 * 1024 * 1024

NEG = -3.0e38
POS = 3.0e38
SOFTMAX_NEG = -1.0e30


def _cparams(*sem):
    return pltpu.CompilerParams(dimension_semantics=sem, vmem_limit_bytes=VMEM_LIMIT_BYTES)


def _const_spec(shape):
    zeros = (0,) * len(shape)
    return pl.BlockSpec(shape, lambda *_: zeros, pipeline_mode=pl.Buffered(1))


def _rope_tables(pos, dim):
    half = dim // 2
    inv = ROPE_THETA ** (-jnp.arange(half, dtype=F32) / half)
    ang = pos.astype(F32)[:, None] * inv[None, :]
    cos = jnp.cos(ang)
    sin = jnp.sin(ang)
    reps = LANES // dim
    cos_t = jnp.tile(jnp.concatenate([cos, cos], axis=1), (1, reps))
    sin_t = jnp.tile(jnp.concatenate([-sin, sin], axis=1), (1, reps))
    return cos_t, sin_t


def _swap_halves_64(x):
    lane = lax.broadcasted_iota(jnp.int32, x.shape, 1)
    first = (lane % 64) < 32
    return jnp.where(first, pltpu.roll(x, 96, 1), pltpu.roll(x, 32, 1))


def _pad_pair_slabs(x, swapped, low):
    return (jnp.where(low, x, 0.0), jnp.where(low, 0.0, swapped), jnp.where(low, swapped, 0.0), jnp.where(low, 0.0, x))


def _inproj_kernel(x_ref, g_ref, w_ref, c64_ref, s64_ref, c128_ref, s128_ref,
                   qt_ref, kt_ref, vt_ref, qit_ref, kit_ref, wit_ref, rq_ref, rkt_ref, rv_ref, rg_ref,
                   kp_ref, vtt_ref, kip_ref, *, n_q, n_qi, n_ret):
    x = x_ref[...]
    tm = x.shape[0]
    ms = jnp.mean(x * x, axis=-1, keepdims=True)
    a = ((x * lax.rsqrt(ms + EPS)) * g_ref[...]).astype(BF16)
    c64, s64 = c64_ref[...], s64_ref[...]
    c128, s128 = c128_ref[...], s128_ref[...]
    low = lax.broadcasted_iota(jnp.int32, (tm, LANES), 1) < HEAD_DIM
    zt = jnp.zeros((HEAD_DIM, tm), BF16)

    def proj(col):
        return jnp.dot(a, w_ref[:, col:col + LANES], preferred_element_type=F32)

    def rope64(y):
        return y * c64 + _swap_halves_64(y) * s64

    def rope128(y):
        return y * c128 + pltpu.roll(y, 64, 1) * s128

    col = 0
    for j in range(n_q // LANES):
        qt_ref[j * LANES:(j + 1) * LANES, :] = (rope64(proj(col)) * (HEAD_DIM ** -0.5)).T.astype(BF16)
        col += LANES
    kk = rope64(proj(col))
    col += LANES
    kt_ref[...] = kk.T
    for s, slab in enumerate(_pad_pair_slabs(kk, pltpu.roll(kk, HEAD_DIM, 1), low)):
        kp_ref[:, s * LANES:(s + 1) * LANES] = slab.astype(BF16)
    vvt = proj(col).T
    col += LANES
    vt_ref[...] = vvt
    vvt = vvt.astype(BF16)
    for g in range(KV_HEADS):
        vtg = vvt[g * HEAD_DIM:(g + 1) * HEAD_DIM, :]
        base = g * 4 * HEAD_DIM
        vtt_ref[base:base + HEAD_DIM, :] = vtg
        vtt_ref[base + HEAD_DIM:base + 2 * HEAD_DIM, :] = zt
        vtt_ref[base + 2 * HEAD_DIM:base + 3 * HEAD_DIM, :] = zt
        vtt_ref[base + 3 * HEAD_DIM:base + 4 * HEAD_DIM, :] = vtg
    for j in range(n_qi // LANES):
        qit_ref[j * LANES:(j + 1) * LANES, :] = (rope64(proj(col)) * (IDX_DIM ** -0.5)).T.astype(BF16)
        col += LANES
    y = proj(col)
    col += LANES
    ki = rope64(y)
    kit_ref[...] = ki.T[:IDX_DIM, :]
    ki_lo, ki_hi, _, _ = _pad_pair_slabs(ki, pltpu.roll(ki, IDX_DIM, 1), low)
    kip_ref[:, 0:LANES] = ki_lo.astype(BF16)
    kip_ref[:, LANES:2 * LANES] = ki_hi.astype(BF16)
    wit_ref[...] = y.T[IDX_DIM:IDX_DIM + IDX_HEADS, :] * (IDX_HEADS ** -0.5)
    for j in range(n_ret // LANES):
        rq_ref[:, j * LANES:(j + 1) * LANES] = rope128(proj(col)).astype(BF16)
        col += LANES
    for j in range(n_ret // LANES):
        rk = rope128(proj(col)) * (RET_DK ** -0.5)
        rkt_ref[j * LANES:(j + 1) * LANES, :] = rk.T.astype(BF16)
        col += LANES
    for j in range(n_ret // LANES):
        rv_ref[:, j * LANES:(j + 1) * LANES] = proj(col).astype(BF16)
        col += LANES
    for j in range(n_ret // LANES):
        rg_ref[:, j * LANES:(j + 1) * LANES] = proj(col)
        col += LANES


def _pack_w_in(w_in, n_q, n_kv, n_qi, n_ret):
    d = w_in.shape[0]
    sizes = (n_q, n_kv, n_kv, n_qi, IDX_DIM, IDX_HEADS, n_ret, n_ret, n_ret, n_ret)
    offs = np.concatenate([[0], np.cumsum(sizes)])
    seg = [w_in[:, offs[i]:offs[i + 1]] for i in range(len(sizes))]
    pad = jnp.zeros((d, LANES - IDX_DIM - IDX_HEADS), w_in.dtype)
    packed = jnp.concatenate(seg[:4] + [seg[4], seg[5], pad] + seg[6:], axis=1)
    return packed.astype(BF16)


def _inproj(x2d, pos, g_mix, w_packed, *, seq, tm, dims):
    n_q, n_kv, n_qi, n_ret = dims
    rows, d = x2d.shape
    nb = rows // seq
    spt = seq // tm
    c64, s64 = _rope_tables(pos, HEAD_DIM)
    c128, s128 = _rope_tables(pos, RET_DK)
    wcols = w_packed.shape[1]

    row = lambda w: pl.BlockSpec((tm, w), lambda i: (i, 0))
    tab = pl.BlockSpec((tm, LANES), lambda i: (i % spt, 0))
    tr = lambda h: pl.BlockSpec((None, h, tm), lambda i: (i // spt, 0, i % spt))
    out_shape = (
        jax.ShapeDtypeStruct((nb, n_q, seq), BF16),
        jax.ShapeDtypeStruct((nb, n_kv, seq), F32),
        jax.ShapeDtypeStruct((nb, n_kv, seq), F32),
        jax.ShapeDtypeStruct((nb, n_qi, seq), BF16),
        jax.ShapeDtypeStruct((nb, IDX_DIM, seq), F32),
        jax.ShapeDtypeStruct((nb, IDX_HEADS, seq), F32),
        jax.ShapeDtypeStruct((rows, n_ret), BF16),
        jax.ShapeDtypeStruct((nb, n_ret, seq), BF16),
        jax.ShapeDtypeStruct((rows, n_ret), BF16),
        jax.ShapeDtypeStruct((rows, n_ret), F32),
        jax.ShapeDtypeStruct((rows, 4 * n_kv), BF16),
        jax.ShapeDtypeStruct((nb, 4 * n_kv, seq), BF16),
        jax.ShapeDtypeStruct((rows, 2 * LANES), BF16),
    )
    out_specs = (tr(n_q), tr(n_kv), tr(n_kv), tr(n_qi), tr(IDX_DIM), tr(IDX_HEADS),
                 row(n_ret), tr(n_ret), row(n_ret), row(n_ret), row(4 * n_kv), tr(4 * n_kv), row(2 * LANES))
    return pl.pallas_call(
        functools.partial(_inproj_kernel, n_q=n_q, n_qi=n_qi, n_ret=n_ret),
        out_shape=out_shape,
        grid=(rows // tm,),
        in_specs=[row(d), _const_spec((1, d)), _const_spec((d, wcols)), tab, tab, tab, tab],
        out_specs=out_specs,
        compiler_params=_cparams("parallel"),
        name="inproj",
    )(x2d, g_mix.reshape(1, d), w_packed, c64, s64, c128, s128)


N_ACC = 4


def _fold_keys(st_ref, nch, ck, init, fn, combine):
    def body(c, carries):
        carries = list(carries)
        base = pl.multiple_of(c * ck, ck)
        chunk = st_ref[pl.ds(base, ck), :]
        for j in range(ck // SUBLANES):
            blk = chunk[j * SUBLANES:(j + 1) * SUBLANES, :]
            carries[j % N_ACC] = fn(carries[j % N_ACC], blk, base + j * SUBLANES)
        return tuple(carries)
    carries = lax.fori_loop(0, nch, body, (init,) * N_ACC)
    out = carries[0]
    for other in carries[1:]:
        out = combine(out, other)
    return out


def _all_sublanes(x, op):
    for shift in (4, 2, 1):
        x = op(x, pltpu.roll(x, shift, 0))
    return x


def _any(mask):
    return jnp.max(jnp.where(mask, 1.0, 0.0)) > 0.5


def _count(st_ref, nch, ck, pred):
    nq = st_ref.shape[1]
    acc = _fold_keys(st_ref, nch, ck, jnp.zeros((SUBLANES, nq), F32),
                     lambda a, blk, row0: a + jnp.where(pred(blk, row0), 1.0, 0.0), jnp.add)
    return _all_sublanes(acc, jnp.add)


def _select_threshold(st_ref, thr_ref, nch, ck, rmin, rmax, topk, n_fast):
    nq = st_ref.shape[1]
    kf = float(topk)
    full = lambda v: jnp.full((SUBLANES, nq), v, F32)

    def fast_body(_, carry):
        lo, hi = carry
        mid = lo + (hi - lo) * 0.5
        up = _count(st_ref, nch, ck, lambda blk, _r: blk > mid) >= kf
        return jnp.where(up, mid, lo), jnp.where(up, hi, mid)

    lo, hi = lax.fori_loop(0, n_fast, fast_body, (rmin, rmax))
    unres = _count(st_ref, nch, ck, lambda blk, _r: blk >= lo) > kf
    thr_ref[...] = lo

    @pl.when(_any(unres))
    def _exact():
        pair_minmax = lambda x, y: (jnp.minimum(x[0], y[0]), jnp.maximum(x[1], y[1]))
        a, b = _fold_keys(
            st_ref, nch, ck, (full(POS), full(NEG)),
            lambda cr, blk, _r: (jnp.minimum(cr[0], jnp.where(blk >= lo, blk, POS)),
                                 jnp.maximum(cr[1], jnp.where(blk <= hi, blk, NEG))),
            pair_minmax)
        lo2 = jnp.where(unres, _all_sublanes(a, jnp.minimum), lo)
        ub = jnp.where(unres, _all_sublanes(b, jnp.maximum), lo)

        def body(carry):
            lo2, ub, _ = carry
            mid = lo2 + (ub - lo2) * 0.5
            mid = jnp.where(mid >= ub, lo2, mid)

            def step(cr, blk, _r):
                gt = blk > mid
                return (cr[0] + jnp.where(gt, 1.0, 0.0), jnp.minimum(cr[1], jnp.where(gt, blk, POS)),
                        jnp.maximum(cr[2], jnp.where(gt, NEG, blk)))
            cnt, a, b = _fold_keys(st_ref, nch, ck, (full(0.0), full(POS), full(NEG)), step,
                                   lambda x, y: (x[0] + y[0],) + pair_minmax(x[1:], y[1:]))
            up = _all_sublanes(cnt, jnp.add) >= kf
            active = lo2 < ub
            lo2n = jnp.where(active & up, _all_sublanes(a, jnp.minimum), lo2)
            ubn = jnp.where(active & jnp.logical_not(up), _all_sublanes(b, jnp.maximum), ub)
            return lo2n, ubn, _any(lo2n < ubn).astype(jnp.int32)

        v, _, _ = lax.while_loop(lambda carry: carry[2] > 0, body, (lo2, ub, _any(lo2 < ub).astype(jnp.int32)))
        thr_ref[...] = jnp.where(unres, v, lo)
        tied = unres & (_count(st_ref, nch, ck, lambda blk, _r: blk >= v) > kf)

        @pl.when(_any(tied))
        def _ties():
            _drop_excess_ties(st_ref, nch, ck, v, tied, kf)


def _drop_excess_ties(st_ref, nch, ck, v, tied, kf):
    nq = st_ref.shape[1]
    full = lambda x: jnp.full((SUBLANES, nq), x, F32)
    need = kf - _count(st_ref, nch, ck, lambda blk, _r: blk > v)
    nkeys = st_ref.shape[0]
    sub = lax.broadcasted_iota(jnp.int32, (SUBLANES, nq), 0)
    key_index = lambda row0: (sub + row0).astype(F32)

    def j_body(_, carry):
        lo_j, hi_j = carry
        mid_j = jnp.floor((lo_j + hi_j) * 0.5)
        ok = _count(st_ref, nch, ck, lambda blk, row0: (blk == v) & (key_index(row0) <= mid_j)) >= need
        return jnp.where(ok, lo_j, mid_j), jnp.where(ok, mid_j, hi_j)

    n_j = int(np.ceil(np.log2(nkeys))) + 1
    _, j_last = lax.fori_loop(0, n_j, j_body, (full(-1.0), full(float(nkeys - 1))))

    def drop_body(c, _):
        base = pl.multiple_of(c * ck, ck)
        for j in range(ck // SUBLANES):
            rows = pl.ds(base + j * SUBLANES, SUBLANES)
            blk = st_ref[rows, :]
            drop = tied & (blk == v) & (key_index(base + j * SUBLANES) > j_last)
            st_ref[rows, :] = jnp.where(drop, NEG, blk)
        return 0
    lax.fori_loop(0, nch, drop_body, 0)


def _rows_min(x):
    return jnp.min(x.reshape(x.shape[0] // SUBLANES, SUBLANES, x.shape[1]), axis=0)


def _rows_max(x):
    return jnp.max(x.reshape(x.shape[0] // SUBLANES, SUBLANES, x.shape[1]), axis=0)


def _rows_sum(x):
    return jnp.sum(x.reshape(x.shape[0] // SUBLANES, SUBLANES, x.shape[1]), axis=0)


def _dsa_prompt_kernel(kip_ref, qit_ref, wit_ref, kp_ref, qt_ref, vtt_ref, o_ref, st_ref, thr_ref,
                       m_ref, l_ref, acc_ref, lg_ref, pr_ref, *, tq, topk, n_fast, rs):
    ck = tq
    i = pl.program_id(1)
    nch = i + 1
    n_pairs = qt_ref.shape[0] // LANES
    pairs_per_kv = n_pairs // KV_HEADS
    bcast = lambda row, n: jnp.broadcast_to(row, (n, tq))

    key_l = lax.broadcasted_iota(jnp.int32, (rs, tq), 0)
    qry_l = lax.broadcasted_iota(jnp.int32, (rs, tq), 1)
    w_rows = [bcast(wit_ref[h:h + 1, :], rs) for h in range(IDX_HEADS)]

    def score_body(c, carry):
        mn, mx = carry
        base = pl.multiple_of(c * ck, ck)
        off_diag = c < i
        for r in range(ck // rs):
            rows = pl.ds(base + r * rs, rs)
            k_lo = kip_ref[rows, 0:LANES]
            k_hi = kip_ref[rows, LANES:2 * LANES]
            acc = jnp.zeros((rs, tq), F32)
            for p in range(IDX_HEADS // 2):
                rhs = qit_ref[p * LANES:(p + 1) * LANES, :]
                acc = acc + jnp.maximum(jnp.dot(k_lo, rhs, preferred_element_type=F32), 0.0) * w_rows[2 * p]
                acc = acc + jnp.maximum(jnp.dot(k_hi, rhs, preferred_element_type=F32), 0.0) * w_rows[2 * p + 1]
            valid = off_diag | (key_l + r * rs <= qry_l)
            st_ref[rows, :] = jnp.where(valid, acc, NEG)
            mn = jnp.minimum(mn, _rows_min(jnp.where(valid, acc, POS)))
            mx = jnp.maximum(mx, _rows_max(jnp.where(valid, acc, NEG)))
        return mn, mx

    mn, mx = lax.fori_loop(0, nch, score_body,
                           (jnp.full((SUBLANES, tq), POS, F32), jnp.full((SUBLANES, tq), NEG, F32)))
    _select_threshold(st_ref, thr_ref, nch, ck, _all_sublanes(mn, jnp.minimum), _all_sublanes(mx, jnp.maximum),
                      topk, n_fast)

    thr = bcast(thr_ref[0:1, :], ck)
    m_ref[...] = jnp.full(m_ref.shape, SOFTMAX_NEG, F32)
    l_ref[...] = jnp.zeros(l_ref.shape, F32)
    acc_ref[...] = jnp.zeros(acc_ref.shape, F32)

    def att_body(c, _):
        rows = pl.ds(pl.multiple_of(c * ck, ck), ck)
        sel = st_ref[rows, :] >= thr
        slab_of = lambda h: 2 * (h // 2 // pairs_per_kv) + h % 2
        n_heads = 2 * n_pairs
        for h in range(n_heads):
            slab = slab_of(h)
            kx = kp_ref[rows, slab * LANES:(slab + 1) * LANES]
            qt_pair = qt_ref[(h // 2) * LANES:(h // 2 + 1) * LANES, :]
            lg_ref[h] = jnp.where(sel, jnp.dot(kx, qt_pair, preferred_element_type=F32), SOFTMAX_NEG)
        alpha = []
        for h in range(n_heads):
            lg = lg_ref[h]
            m_old = m_ref[h]
            m_new = jnp.maximum(m_old, _all_sublanes(_rows_max(lg), jnp.maximum))
            a = jnp.exp(m_old - m_new)
            pr = jnp.exp(lg - bcast(m_new[0:1, :], ck))
            m_ref[h] = m_new
            l_ref[h] = a * l_ref[h] + _all_sublanes(_rows_sum(pr), jnp.add)
            pr_ref[h] = pr.astype(BF16)
            alpha.append(bcast(a[0:1, :], HEAD_DIM))
        for p in range(n_pairs):
            pv = [jnp.dot(vtt_ref[slab_of(h) * LANES:(slab_of(h) + 1) * LANES, rows], pr_ref[h],
                          preferred_element_type=F32) for h in (2 * p, 2 * p + 1)]
            acc_ref[p] = acc_ref[p] * jnp.concatenate(alpha[2 * p:2 * p + 2], axis=0) + pv[0] + pv[1]
        return 0

    lax.fori_loop(0, nch, att_body, 0)
    for p in range(n_pairs):
        denom = jnp.concatenate([bcast(l_ref[2 * p, 0:1, :], HEAD_DIM), bcast(l_ref[2 * p + 1, 0:1, :], HEAD_DIM)],
                                axis=0)
        o_ref[:, p * LANES:(p + 1) * LANES] = (acc_ref[p] / denom).T.astype(o_ref.dtype)


def _dsa_prompt(kip, qit, wit, kp, qt, vtt, *, nb, seq, tq, topk, n_fast):
    n_q = qt.shape[1]
    spb = seq // tq
    per_b_rows = lambda a: pl.BlockSpec((seq, a.shape[1]), lambda b, i: (b, 0))
    q_cols = lambda a: pl.BlockSpec((None, a.shape[1], tq), lambda b, i: (b, 0, i))
    return pl.pallas_call(
        functools.partial(_dsa_prompt_kernel, tq=tq, topk=topk, n_fast=n_fast, rs=min(tq, 128)),
        out_shape=jax.ShapeDtypeStruct((nb * seq, n_q), BF16),
        grid=(nb, spb),
        in_specs=[per_b_rows(kip), q_cols(qit), q_cols(wit), per_b_rows(kp), q_cols(qt),
                  pl.BlockSpec((None, vtt.shape[1], seq), lambda b, i: (b, 0, 0))],
        out_specs=pl.BlockSpec((tq, n_q), lambda b, i: (b * spb + i, 0)),
        scratch_shapes=[pltpu.VMEM((seq, tq), F32), pltpu.VMEM((SUBLANES, tq), F32),
                        pltpu.VMEM((n_q // HEAD_DIM, SUBLANES, tq), F32),
                        pltpu.VMEM((n_q // HEAD_DIM, SUBLANES, tq), F32),
                        pltpu.VMEM((n_q // LANES, LANES, tq), F32),
                        pltpu.VMEM((n_q // HEAD_DIM, tq, tq), F32),
                        pltpu.VMEM((n_q // HEAD_DIM, tq, tq), BF16)],
        compiler_params=_cparams("parallel", "arbitrary"),
        name="dsa_prompt",
    )(kip, qit, wit, kp, qt, vtt)


def _retention_tables(chunk, n_heads):
    log_g = jnp.log1p(-jnp.exp2(-5.0 - jnp.arange(n_heads, dtype=F32)))
    i = jnp.arange(chunk, dtype=F32)
    diff = i[:, None] - i[None, :]
    decay = jnp.where(diff >= 0, jnp.exp(log_g[:, None, None] * jnp.maximum(diff, 0.0)), 0.0)
    q_dec = jnp.exp(log_g[:, None] * (i[None, :] + 1.0))
    k_dec = jnp.exp(log_g[:, None] * (chunk - 1.0 - i)[None, :])
    c_dec = jnp.exp(log_g * chunk)
    return decay, q_dec, k_dec, c_dec


def _group_norm_gate(o, rg, gain):
    mu = jnp.mean(o, axis=-1, keepdims=True)
    var = jnp.mean(jnp.square(o - mu), axis=-1, keepdims=True)
    on = ((o - mu) * lax.rsqrt(var + GN_EPS)) * gain
    return jax.nn.silu(rg) * on


def _retention_prompt_kernel(rq_ref, rkt_ref, rv_ref, rg_ref, gain_ref, dmat_ref, qdec_ref, kdec_ref, cdec_ref,
                             o_ref, st_ref, state_ref, *, chunk, n_heads):
    j = pl.program_id(1)

    @pl.when(j == 0)
    def _():
        state_ref[...] = jnp.zeros_like(state_ref)

    for cc in range(rq_ref.shape[0] // chunk):
        rows = slice(cc * chunk, (cc + 1) * chunk)
        for h in range(n_heads):
            cols = slice(h * RET_DV, (h + 1) * RET_DV)
            q = rq_ref[rows, cols]
            kt = rkt_ref[cols, rows]
            v = rv_ref[rows, cols]
            state = state_ref[h]
            inner = jnp.dot(q, kt, preferred_element_type=F32) * dmat_ref[h]
            o = jnp.dot(inner.astype(BF16), v, preferred_element_type=F32)
            o = o + jnp.dot(q, state.astype(BF16), preferred_element_type=F32) * qdec_ref[h]
            kd = (kt.astype(F32) * kdec_ref[h]).astype(BF16)
            state_ref[h] = cdec_ref[h] * state + jnp.dot(kd, v, preferred_element_type=F32)
            o_ref[rows, cols] = _group_norm_gate(o, rg_ref[rows, cols], gain_ref[:, cols]).astype(o_ref.dtype)

    @pl.when(j == pl.num_programs(1) - 1)
    def _():
        st_ref[...] = state_ref[...]


def _retention_prompt(rq, rkt, rv, rg, g_retn, *, nb, seq, tr):
    n_ret = rq.shape[1]
    n_heads = n_ret // RET_DV
    chunk = RET_CHUNK
    decay, q_dec, k_dec, c_dec = _retention_tables(chunk, n_heads)
    qdec_b = jnp.broadcast_to(q_dec[:, :, None], (n_heads, chunk, LANES))
    kdec_b = k_dec[:, None, :]
    cdec_b = jnp.broadcast_to(c_dec[:, None, None], (n_heads, 1, LANES))
    spb = seq // tr
    row = lambda: pl.BlockSpec((tr, n_ret), lambda b, j: (b * spb + j, 0))
    return pl.pallas_call(
        functools.partial(_retention_prompt_kernel, chunk=chunk, n_heads=n_heads),
        out_shape=(jax.ShapeDtypeStruct((nb * seq, n_ret), BF16),
                   jax.ShapeDtypeStruct((nb, n_heads, RET_DK, RET_DV), F32)),
        grid=(nb, spb),
        in_specs=[row(), pl.BlockSpec((None, n_ret, tr), lambda b, j: (b, 0, j)), row(), row(),
                  _const_spec((1, n_ret)), _const_spec((n_heads, chunk, chunk)),
                  _const_spec((n_heads, chunk, LANES)), _const_spec((n_heads, 1, chunk)),
                  _const_spec((n_heads, 1, LANES))],
        out_specs=(row(), pl.BlockSpec((None, n_heads, RET_DK, RET_DV), lambda b, j: (b, 0, 0, 0))),
        scratch_shapes=[pltpu.VMEM((n_heads, RET_DK, RET_DV), F32)],
        compiler_params=_cparams("parallel", "arbitrary"),
        name="retention_prompt",
    )(rq, rkt, rv, rg, g_retn.reshape(1, n_ret), decay, qdec_b, kdec_b, cdec_b)


def _rms(x, g):
    return (x * lax.rsqrt(jnp.mean(x * x, axis=-1, keepdims=True) + EPS)) * g


def _tail_kernel(h_ref, attn_ref, ret_ref, p_ref, wo_a_ref, wo_r_ref, g_ffn_ref, w_gate_ref, w_up_ref, w_down_ref,
                 g_ple_ref, w_pg_ref, w_pp_ref, g_fin_ref, y_ref):
    dot = functools.partial(jnp.dot, preferred_element_type=F32)
    h = h_ref[...] + dot(attn_ref[...], wo_a_ref[...]) + dot(ret_ref[...], wo_r_ref[...])
    f = _rms(h, g_ffn_ref[...]).astype(BF16)
    act = (jax.nn.silu(dot(f, w_gate_ref[...])) * dot(f, w_up_ref[...])).astype(BF16)
    h = h + dot(act, w_down_ref[...])
    gate = jax.nn.sigmoid(dot(_rms(h, g_ple_ref[...]).astype(BF16), w_pg_ref[...]))
    h = h + gate * dot(p_ref[...].astype(BF16), w_pp_ref[...])
    y_ref[...] = _rms(h, g_fin_ref[...])


def _tail(h2d, attn, ret, p2d, w_out, g_ffn, w_ffn_in, w_ffn_out, g_ple, w_ple_gate, w_ple_proj, g_final, *, tm):
    rows, d = h2d.shape
    n_attn = attn.shape[1]
    d_ff = w_ffn_out.shape[0]
    ws = [w_out[:n_attn].astype(BF16), w_out[n_attn:].astype(BF16), g_ffn.reshape(1, d),
          w_ffn_in[:, :d_ff].astype(BF16), w_ffn_in[:, d_ff:].astype(BF16), w_ffn_out.astype(BF16),
          g_ple.reshape(1, d), w_ple_gate.astype(BF16), w_ple_proj.astype(BF16), g_final.reshape(1, d)]
    row = lambda w: pl.BlockSpec((tm, w), lambda i: (i, 0))
    return pl.pallas_call(
        _tail_kernel,
        out_shape=jax.ShapeDtypeStruct((rows, d), F32),
        grid=(rows // tm,),
        in_specs=[row(d), row(n_attn), row(ret.shape[1]), row(p2d.shape[1])] + [_const_spec(w.shape) for w in ws],
        out_specs=row(d),
        compiler_params=_cparams("parallel"),
        name="tail",
    )(h2d, attn, ret, p2d, *ws)


def _page_copies(pools, bufs, sems, pt_ref, b, c, slot, pages_per_chunk):
    copies = []
    for pool, buf in zip(pools, bufs):
        for p in range(pages_per_chunk):
            page = pt_ref[b, c * pages_per_chunk + p]
            copies.append(pltpu.make_async_copy(pool.at[page], buf.at[slot, p], sems.at[slot]))
    return copies


def _chunk_keys_on_lanes(pages):
    return jnp.concatenate([pages[p] for p in range(pages.shape[0])], axis=1).astype(BF16)


def _stream_chunks(pools, bufs, sems, pt_ref, n_chunks, pages_per_chunk, compute):
    b = pl.program_id(0)
    nb = pl.num_programs(0)

    def start(bb, cc, slot):
        for cp in _page_copies(pools, bufs, sems, pt_ref, bb, cc, slot, pages_per_chunk):
            cp.start()

    @pl.when(b == 0)
    def _():
        start(0, 0, 0)

    assert n_chunks % 2 == 0
    for c in range(n_chunks):
        slot = c % 2
        if c + 1 < n_chunks:
            start(b, c + 1, 1 - slot)
        else:
            @pl.when(b + 1 < nb)
            def _():
                start(b + 1, 0, 1 - slot)
        for cp in _page_copies(pools, bufs, sems, pt_ref, b, c, slot, pages_per_chunk):
            cp.wait()
        compute(c, slot)


def _tile_lanes(x, width):
    return x if width == LANES else jnp.concatenate([x] * (width // LANES), axis=1)


def _nt_dot(a, b):
    return lax.dot_general(a, b, (((1,), (1,)), ((), ())), preferred_element_type=F32)


def _head_sum(x, n_t):
    return jnp.sum(x.reshape(n_t, IDX_HEADS, x.shape[-1]), axis=1)


def _sample_scores_kernel(pt_ref, qi_ref, wi_ref, kinew_ref, pool_ref, s_ref, buf, sems,
                          *, n_t, n_chunks, pages_per_chunk):
    ck = pages_per_chunk * PAGE_SIZE
    qi = qi_ref[...]
    w = wi_ref[...]

    def compute(c, slot):
        a = jnp.dot(qi, _chunk_keys_on_lanes(buf[slot]), preferred_element_type=F32)
        s_ref[:, c * ck:(c + 1) * ck] = _head_sum(jnp.maximum(a, 0.0) * w, n_t)

    _stream_chunks([pool_ref], [buf], sems, pt_ref, n_chunks, pages_per_chunk, compute)

    a = jnp.dot(qi, kinew_ref[...], preferred_element_type=F32)
    s_new = _head_sum(jnp.maximum(a, 0.0) * w, n_t)
    t_q = lax.broadcasted_iota(jnp.int32, s_new.shape, 0)
    t_k = lax.broadcasted_iota(jnp.int32, s_new.shape, 1)
    s_ref[:, n_chunks * ck:] = jnp.where(t_k <= t_q, s_new, NEG)


def _sample_scores(page_table, qi_s, wi_s, kinew, pool_ik, *, n_t, pages_per_chunk):
    db, n_pages = page_table.shape
    n_chunks = n_pages // pages_per_chunk
    past = n_pages * PAGE_SIZE
    rows = n_t * IDX_HEADS
    grid_spec = pltpu.PrefetchScalarGridSpec(
        num_scalar_prefetch=1,
        grid=(db,),
        in_specs=[pl.BlockSpec((None, rows, IDX_DIM), lambda b, pt: (b, 0, 0)),
                  pl.BlockSpec((None, rows, 1), lambda b, pt: (b, 0, 0)),
                  pl.BlockSpec((None, IDX_DIM, PAGE_SIZE), lambda b, pt: (b, 0, 0)),
                  pl.BlockSpec(memory_space=pl.ANY)],
        out_specs=pl.BlockSpec((None, n_t, past + PAGE_SIZE), lambda b, pt: (b, 0, 0)),
        scratch_shapes=[pltpu.VMEM((2, pages_per_chunk, IDX_DIM, PAGE_SIZE), F32),
                        pltpu.SemaphoreType.DMA((2,))],
    )
    return pl.pallas_call(
        functools.partial(_sample_scores_kernel, n_t=n_t, n_chunks=n_chunks, pages_per_chunk=pages_per_chunk),
        out_shape=jax.ShapeDtypeStruct((db, n_t, past + PAGE_SIZE), F32),
        grid_spec=grid_spec,
        compiler_params=_cparams("arbitrary"),
        name="sample_scores",
    )(page_table, qi_s, wi_s, kinew, pool_ik)


def _sample_select_kernel(s_in_ref, s_out_ref, thr_ref, st_ref, thr8_ref, *, ck, topk, n_fast):
    nq, nkeys = s_in_ref.shape
    nch = nkeys // ck + 0 * pl.program_id(0)
    chunk = lambda c: pl.ds(pl.multiple_of(c * ck, ck), ck)

    def to_lanes(c, _):
        st_ref[chunk(c), :] = s_in_ref[:, chunk(c)].T
        return 0
    lax.fori_loop(0, nch, to_lanes, 0)

    mn, mx = _fold_keys(
        st_ref, nch, ck, (jnp.full((SUBLANES, nq), POS, F32), jnp.full((SUBLANES, nq), NEG, F32)),
        lambda cr, blk, _r: (jnp.minimum(cr[0], jnp.where(blk > 0.5 * NEG, blk, POS)), jnp.maximum(cr[1], blk)),
        lambda x, y: (jnp.minimum(x[0], y[0]), jnp.maximum(x[1], y[1])))
    _select_threshold(st_ref, thr8_ref, nch, ck, _all_sublanes(mn, jnp.minimum), _all_sublanes(mx, jnp.maximum),
                      topk, n_fast)
    def to_rows(c, _):
        s_out_ref[:, chunk(c)] = st_ref[chunk(c), :].T
        return 0
    lax.fori_loop(0, nch, to_rows, 0)
    thr_ref[...] = jnp.broadcast_to(thr8_ref[0:1, :], (LANES, nq)).T


def _sample_select(s2d, *, topk, n_fast):
    rows, ncols = s2d.shape
    ck = max(w for w in (LANES, 2 * LANES, 3 * LANES, 4 * LANES) if ncols % w == 0)
    full = pl.BlockSpec((rows, ncols), lambda i: (0, 0), pipeline_mode=pl.Buffered(1))
    return pl.pallas_call(
        functools.partial(_sample_select_kernel, ck=ck, topk=topk, n_fast=n_fast),
        out_shape=(jax.ShapeDtypeStruct((rows, ncols), F32), jax.ShapeDtypeStruct((rows, LANES), F32)),
        grid=(1,),
        in_specs=[full],
        out_specs=(full, pl.BlockSpec((rows, LANES), lambda i: (0, 0))),
        scratch_shapes=[pltpu.VMEM((ncols, rows), F32), pltpu.VMEM((SUBLANES, rows), F32)],
        compiler_params=_cparams("arbitrary"),
        name="sample_select",
    )(s2d)


def _sample_attend_kernel(pt_ref, q_ref, s_ref, thr_ref, knew_ref, vnew_ref, poolk_ref, poolv_ref, o_ref,
                          kbuf, vbuf, sems, *, n_t, n_chunks, pages_per_chunk):
    ck = pages_per_chunk * PAGE_SIZE
    rows = q_ref.shape[1]
    reps = rows // n_t
    thr = jnp.concatenate([thr_ref[...]] * reps, axis=0)
    state = [[jnp.full((rows, 1), SOFTMAX_NEG, F32), jnp.zeros((rows, 1), F32), jnp.zeros((rows, HEAD_DIM), F32)]
             for _ in range(KV_HEADS)]

    def attend(kt, vt, s_blk):
        width = s_blk.shape[1]
        sel = jnp.concatenate([s_blk] * reps, axis=0) >= _tile_lanes(thr, width)
        for g in range(KV_HEADS):
            m, l, acc = state[g]
            lg = jnp.where(sel, jnp.dot(q_ref[g], kt[g], preferred_element_type=F32), SOFTMAX_NEG)
            m_new = jnp.maximum(m, jnp.max(lg, axis=1, keepdims=True))
            alpha = jnp.exp(m - m_new)
            pr = jnp.exp(lg - m_new)
            state[g] = [m_new, alpha * l + jnp.sum(pr, axis=1, keepdims=True),
                        alpha * acc + _nt_dot(pr.astype(BF16), vt[g])]

    def compute(c, slot):
        attend([_chunk_keys_on_lanes(kbuf[slot, :, g]) for g in range(KV_HEADS)],
               [_chunk_keys_on_lanes(vbuf[slot, :, g]) for g in range(KV_HEADS)], s_ref[:, c * ck:(c + 1) * ck])

    _stream_chunks([poolk_ref, poolv_ref], [kbuf, vbuf], sems, pt_ref, n_chunks, pages_per_chunk, compute)
    attend(knew_ref[...], vnew_ref[...], s_ref[:, n_chunks * ck:])
    for g in range(KV_HEADS):
        o_ref[g] = state[g][2] / state[g][1]


def _sample_attend(page_table, q_bd, s3d, thr3d, knew, vnew, pool_k, pool_v, *, n_t, pages_per_chunk):
    db, n_pages = page_table.shape
    n_chunks = n_pages // pages_per_chunk
    rows = q_bd.shape[2]
    ncols = s3d.shape[2]
    per_b = lambda r, w: pl.BlockSpec((None, r, w), lambda b, pt: (b, 0, 0))
    per_b4 = lambda r, w: pl.BlockSpec((None, KV_HEADS, r, w), lambda b, pt: (b, 0, 0, 0))
    grid_spec = pltpu.PrefetchScalarGridSpec(
        num_scalar_prefetch=1,
        grid=(db,),
        in_specs=[per_b4(rows, HEAD_DIM), per_b(n_t, ncols), per_b(n_t, LANES), per_b4(HEAD_DIM, PAGE_SIZE),
                  per_b4(HEAD_DIM, PAGE_SIZE), pl.BlockSpec(memory_space=pl.ANY), pl.BlockSpec(memory_space=pl.ANY)],
        out_specs=per_b4(rows, HEAD_DIM),
        scratch_shapes=[pltpu.VMEM((2, pages_per_chunk, KV_HEADS, HEAD_DIM, PAGE_SIZE), F32),
                        pltpu.VMEM((2, pages_per_chunk, KV_HEADS, HEAD_DIM, PAGE_SIZE), F32),
                        pltpu.SemaphoreType.DMA((2,))],
    )
    return pl.pallas_call(
        functools.partial(_sample_attend_kernel, n_t=n_t, n_chunks=n_chunks, pages_per_chunk=pages_per_chunk),
        out_shape=jax.ShapeDtypeStruct((db, KV_HEADS, rows, HEAD_DIM), F32),
        grid_spec=grid_spec,
        compiler_params=_cparams("arbitrary"),
        name="sample_attend",
    )(page_table, q_bd, s3d, thr3d, knew, vnew, pool_k, pool_v)


def _retention_sample_kernel(rq_ref, rkt_ref, rv_ref, rg_ref, gain_ref, dfull_ref, qdec_ref, kdec_ref, cdec_ref,
                             st_in_ref, o_ref, st_out_ref, *, n_t, n_heads):
    b = pl.program_id(0)

    @pl.when(b == 0)
    def _():
        o_ref[...] = jnp.zeros_like(o_ref)

    n_rows = rq_ref.shape[0]
    row = lax.broadcasted_iota(jnp.int32, (n_rows, RET_DV), 0)
    mine = (row >= b * n_t) & (row < (b + 1) * n_t)
    for h in range(n_heads):
        cols = slice(h * RET_DV, (h + 1) * RET_DV)
        q = jnp.where(mine, rq_ref[:, cols], 0.0).astype(BF16)
        v = rv_ref[:, cols]
        kt = rkt_ref[cols, :]
        state = st_in_ref[h]
        inner = jnp.dot(q, kt, preferred_element_type=F32) * dfull_ref[h]
        o = jnp.dot(inner.astype(BF16), v, preferred_element_type=F32)
        o = o + jnp.dot(q, state.astype(BF16), preferred_element_type=F32) * qdec_ref[h]
        vd = jnp.where(mine, v.astype(F32) * kdec_ref[h], 0.0).astype(BF16)
        st_out_ref[h] = cdec_ref[h] * state + jnp.dot(kt, vd, preferred_element_type=F32)
        gated = _group_norm_gate(o, rg_ref[:, cols], gain_ref[:, cols])
        o_ref[:, cols] = jnp.where(mine, gated, o_ref[:, cols].astype(F32)).astype(o_ref.dtype)


def _retention_sample(rq, rkt, rv, rg, g_retn, state, *, db, n_t):
    n_rows, n_ret = rq.shape
    n_heads = n_ret // RET_DV
    decay, q_dec, k_dec, c_dec = _retention_tables(n_t, n_heads)
    same = jnp.kron(jnp.eye(db, dtype=F32), jnp.ones((n_t, n_t), F32))
    dfull = same[None] * jnp.tile(decay, (1, db, db))
    qdec_b = jnp.broadcast_to(jnp.tile(q_dec, (1, db))[:, :, None], (n_heads, n_rows, RET_DV))
    kdec_b = jnp.broadcast_to(jnp.tile(k_dec, (1, db))[:, :, None], (n_heads, n_rows, RET_DV))
    cdec_b = jnp.broadcast_to(c_dec[:, None, None], (n_heads, 1, RET_DV))
    const = lambda shape: pl.BlockSpec(shape, lambda b: (0,) * len(shape))
    st_spec = pl.BlockSpec((None, n_heads, RET_DK, RET_DV), lambda b: (b, 0, 0, 0))
    return pl.pallas_call(
        functools.partial(_retention_sample_kernel, n_t=n_t, n_heads=n_heads),
        out_shape=(jax.ShapeDtypeStruct((n_rows, n_ret), BF16),
                   jax.ShapeDtypeStruct((db, n_heads, RET_DK, RET_DV), F32)),
        grid=(db,),
        in_specs=[const((n_rows, n_ret)), const((n_ret, n_rows)), const((n_rows, n_ret)), const((n_rows, n_ret)),
                  const((1, n_ret)), const((n_heads, n_rows, n_rows)), const((n_heads, n_rows, RET_DV)),
                  const((n_heads, n_rows, RET_DV)), const((n_heads, 1, RET_DV)), st_spec],
        out_specs=(const((n_rows, n_ret)), st_spec),
        compiler_params=_cparams("arbitrary"),
        name="retention_sample",
    )(rq, rkt, rv, rg, g_retn.reshape(1, n_ret), dfull, qdec_b, kdec_b, cdec_b, state)


def _mixer_dims(w_in, d_model):
    n_q = d_model // 2
    n_ret = d_model // 2
    n_kv = KV_HEADS * HEAD_DIM
    n_qi = IDX_HEADS * IDX_DIM
    assert n_kv == LANES and 2 * n_kv + n_q + n_qi + IDX_DIM + IDX_HEADS + 4 * n_ret == w_in.shape[1]
    return n_q, n_kv, n_qi, n_ret


def _prompt_layer(x, p, weights, *, n_fast=24):
    g_mix, w_in, g_retn, w_out, g_ffn, w_ffn_in, w_ffn_out, g_ple, w_ple_gate, w_ple_proj, g_final = weights
    nb, seq, d = x.shape
    dims = _mixer_dims(w_in, d)
    w_packed = _pack_w_in(w_in, *dims)
    x2d = x.reshape(nb * seq, d)
    tm = min(512, seq)
    (qt, kt, vt, qit, kit, wit, rq, rkt, rv, rg, kp, vtt, kip) = _inproj(
        x2d, jnp.arange(seq), g_mix, w_packed, seq=seq, tm=tm, dims=dims)
    topk = min(TOPK_MAX, seq // 4)
    attn = _dsa_prompt(kip, qit, wit, kp, qt, vtt, nb=nb, seq=seq, tq=min(256, seq), topk=topk, n_fast=n_fast)
    ret, st = _retention_prompt(rq, rkt, rv, rg, g_retn, nb=nb, seq=seq, tr=min(512, seq))
    y = _tail(x2d, attn, ret, p.reshape(nb * seq, -1), w_out, g_ffn, w_ffn_in, w_ffn_out,
              g_ple, w_ple_gate, w_ple_proj, g_final, tm=min(256, nb * seq))
    heads_last = lambda a: a.reshape(nb, KV_HEADS, HEAD_DIM, seq).transpose(0, 3, 1, 2)
    return y.reshape(nb, seq, d), heads_last(kt), heads_last(vt), kit.transpose(0, 2, 1), st


def _sample_layer(x, p, pool_k, pool_v, pool_ik, state, page_table, weights, *, n_fast=24):
    g_mix, w_in, g_retn, w_out, g_ffn, w_ffn_in, w_ffn_out, g_ple, w_ple_gate, w_ple_proj, g_final = weights
    db, n_t, d = x.shape
    n_rows = db * n_t
    n_pages = page_table.shape[1]
    past = n_pages * PAGE_SIZE
    dims = _mixer_dims(w_in, d)
    n_q = dims[0]
    w_packed = _pack_w_in(w_in, *dims)
    x2d = x.reshape(n_rows, d)
    pos = jnp.tile(past + jnp.arange(n_t), db)
    (qt, kt, vt, qit, kit, wit, rq, rkt, rv, rg, _, _, _) = _inproj(
        x2d, pos, g_mix, w_packed, seq=n_rows, tm=n_rows, dims=dims)
    q, qi, wi = qt[0].T, qit[0].T, wit[0].T

    def new_pages(at, heads):
        a = at[0].reshape(heads, -1, db, n_t).transpose(2, 0, 1, 3)
        return jnp.pad(a, ((0, 0), (0, 0), (0, 0), (0, PAGE_SIZE - n_t))).astype(BF16)
    group = n_q // HEAD_DIM // KV_HEADS
    qg = q.reshape(db, n_t, KV_HEADS, group, HEAD_DIM).transpose(0, 2, 3, 1, 4)
    qg = qg.reshape(db, KV_HEADS, group * n_t, HEAD_DIM)
    pages_per_chunk = max(1, min(16, n_pages // 2))

    s3d = _sample_scores(page_table, qi.reshape(db, n_t * IDX_HEADS, IDX_DIM), wi.reshape(db, n_t * IDX_HEADS, 1),
                         new_pages(kit, 1)[:, 0], pool_ik.transpose(0, 2, 1), n_t=n_t, pages_per_chunk=pages_per_chunk)
    topk = min(TOPK_MAX, (past + n_t) // 4)
    s_sel, thr = _sample_select(s3d.reshape(n_rows, -1), topk=topk, n_fast=n_fast)
    o = _sample_attend(page_table, qg, s_sel.reshape(db, n_t, -1), thr.reshape(db, n_t, LANES),
                       new_pages(kt, KV_HEADS), new_pages(vt, KV_HEADS), pool_k.transpose(0, 2, 3, 1),
                       pool_v.transpose(0, 2, 3, 1), n_t=n_t, pages_per_chunk=pages_per_chunk)
    attn = o.reshape(db, KV_HEADS, group, n_t, HEAD_DIM).transpose(0, 3, 1, 2, 4).reshape(n_rows, n_q).astype(BF16)

    ret, st = _retention_sample(rq, rkt[0], rv, rg, g_retn, state, db=db, n_t=n_t)
    y = _tail(x2d, attn, ret, p.reshape(n_rows, -1), w_out, g_ffn, w_ffn_in, w_ffn_out,
              g_ple, w_ple_gate, w_ple_proj, g_final, tm=n_rows)
    heads_last = lambda a: a[0].reshape(KV_HEADS, HEAD_DIM, db, n_t).transpose(2, 3, 0, 1)
    return y.reshape(db, n_t, d), heads_last(kt), heads_last(vt), kit[0].T.reshape(db, n_t, IDX_DIM), st


def kernel(x_prompt, x_sample, cache_k, cache_v, cache_idx_k, state_retn, page_table, p_prompt, p_sample, g_mix, w_in,
           g_retn, w_out, g_ffn, w_ffn_in, w_ffn_out, g_ple, w_ple_gate, w_ple_proj, g_final):
    depth = w_in.shape[0]
    assert depth == 1, "the final RMSNorm is fused into the single layer's tail kernel"
    weights = (g_mix[0], w_in[0], g_retn[0], w_out[0], g_ffn[0], w_ffn_in[0], w_ffn_out[0], g_ple[0], w_ple_gate[0],
               w_ple_proj[0], g_final)
    y_p, k_p, v_p, ik_p, st_p = _prompt_layer(x_prompt, p_prompt[0], weights)
    y_s, k_s, v_s, ik_s, st_s = _sample_layer(x_sample, p_sample[0], cache_k[0], cache_v[0], cache_idx_k[0],
                                              state_retn[0], page_table, weights)
    return (y_p, y_s, k_p[None], v_p[None], ik_p[None], st_p[None], k_s[None], v_s[None], ik_s[None], st_s[None])
```

```python
import functools

import jax
import jax.numpy as jnp
import numpy as np
from jax import lax
from jax.experimental import pallas as pl
from jax.experimental.pallas import tpu as pltpu

F32 = jnp.float32
BF16 = jnp.bfloat16

HEAD_DIM = 64
KV_HEADS = 2
IDX_HEADS = 8
IDX_DIM = 64
TOPK_MAX = 256
RET_DK = 128
RET_DV = 128
RET_CHUNK = 128
PAGE_SIZE = 128
ROPE_THETA = 10000.0
EPS = 1e-6
GN_EPS = 1e-5

LANES = 128
SUBLANES = 8
VMEM_LIMIT_BYTES = 56 * 1024 * 1024

Q_SCALE = HEAD_DIM ** -0.5 * 1.4426950408889634
ONES_ROWS = 16

NEG = -3.0e38
POS = 3.0e38
SOFTMAX_NEG = -1.0e30


def _cparams(*sem):
    return pltpu.CompilerParams(dimension_semantics=sem, vmem_limit_bytes=VMEM_LIMIT_BYTES)


def _const_spec(shape):
    zeros = (0,) * len(shape)
    return pl.BlockSpec(shape, lambda *_: zeros, pipeline_mode=pl.Buffered(1))


def _rope_tables(pos, dim):
    half = dim // 2
    inv = ROPE_THETA ** (-jnp.arange(half, dtype=F32) / half)
    ang = pos.astype(F32)[:, None] * inv[None, :]
    cos = jnp.cos(ang)
    sin = jnp.sin(ang)
    reps = LANES // dim
    cos_t = jnp.tile(jnp.concatenate([cos, cos], axis=1), (1, reps))
    sin_t = jnp.tile(jnp.concatenate([-sin, sin], axis=1), (1, reps))
    return cos_t, sin_t


def _swap_halves_64(x):
    lane = lax.broadcasted_iota(jnp.int32, x.shape, 1)
    first = (lane % 64) < 32
    return jnp.where(first, pltpu.roll(x, 96, 1), pltpu.roll(x, 32, 1))


def _pad_pair_slabs(x, swapped, low):
    return (jnp.where(low, x, 0.0), jnp.where(low, 0.0, swapped), jnp.where(low, swapped, 0.0), jnp.where(low, 0.0, x))


def _inproj_kernel(x_ref, g_ref, w_ref, c64_ref, s64_ref, c128_ref, s128_ref,
                   qt_ref, kt_ref, vt_ref, qit_ref, kit_ref, wit_ref, rq_ref, rkt_ref, rv_ref, rg_ref,
                   kp_ref, vtt_ref, kip_ref, *, n_q, n_qi, n_ret):
    x = x_ref[...]
    tm = x.shape[0]
    ms = jnp.mean(x * x, axis=-1, keepdims=True)
    a = ((x * lax.rsqrt(ms + EPS)) * g_ref[...]).astype(BF16)
    c64, s64 = c64_ref[...], s64_ref[...]
    c128, s128 = c128_ref[...], s128_ref[...]
    low = lax.broadcasted_iota(jnp.int32, (tm, LANES), 1) < HEAD_DIM
    zt = jnp.zeros((HEAD_DIM, tm), BF16)

    col = [0]

    def segment(width):
        y = jnp.dot(a, w_ref[:, col[0]:col[0] + width], preferred_element_type=F32)
        col[0] += width
        return [y[:, j * LANES:(j + 1) * LANES] for j in range(width // LANES)]

    def rope64(y):
        return y * c64 + _swap_halves_64(y) * s64

    def rope128(y):
        return y * c128 + pltpu.roll(y, 64, 1) * s128

    for j, y in enumerate(segment(n_q)):
        qt_ref[j * LANES:(j + 1) * LANES, :] = (rope64(y) * Q_SCALE).T.astype(BF16)
    kk, vv = segment(2 * LANES)
    kk = rope64(kk)
    kt_ref[...] = kk.T
    for s, slab in enumerate(_pad_pair_slabs(kk, pltpu.roll(kk, HEAD_DIM, 1), low)):
        kp_ref[:, s * LANES:(s + 1) * LANES] = slab.astype(BF16)
    vvt = vv.T
    vt_ref[...] = vvt
    vvt = vvt.astype(BF16)
    for g in range(KV_HEADS):
        vtg = vvt[g * HEAD_DIM:(g + 1) * HEAD_DIM, :]
        base = g * 4 * HEAD_DIM
        vtt_ref[base:base + HEAD_DIM, :] = vtg
        vtt_ref[base + HEAD_DIM:base + 2 * HEAD_DIM, :] = zt
        vtt_ref[base + 2 * HEAD_DIM:base + 3 * HEAD_DIM, :] = zt
        vtt_ref[base + 3 * HEAD_DIM:base + 4 * HEAD_DIM, :] = vtg
    for j, y in enumerate(segment(n_qi)):
        qit_ref[j * LANES:(j + 1) * LANES, :] = (rope64(y) * (IDX_DIM ** -0.5)).T.astype(BF16)
    y, _ = segment(2 * LANES)
    ki = rope64(y)
    kit_ref[...] = ki.T[:IDX_DIM, :]
    ki_lo, ki_hi, _, _ = _pad_pair_slabs(ki, pltpu.roll(ki, IDX_DIM, 1), low)
    kip_ref[:, 0:LANES] = ki_lo.astype(BF16)
    kip_ref[:, LANES:2 * LANES] = ki_hi.astype(BF16)
    wit_ref[...] = y.T[IDX_DIM:IDX_DIM + IDX_HEADS, :] * (IDX_HEADS ** -0.5)
    for j, y in enumerate(segment(n_ret)):
        rq_ref[:, j * LANES:(j + 1) * LANES] = rope128(y).astype(BF16)
    for j, y in enumerate(segment(n_ret)):
        rkt_ref[j * LANES:(j + 1) * LANES, :] = (rope128(y) * (RET_DK ** -0.5)).T.astype(BF16)
    for j, y in enumerate(segment(n_ret)):
        rv_ref[:, j * LANES:(j + 1) * LANES] = y.astype(BF16)
    for j, y in enumerate(segment(n_ret)):
        rg_ref[:, j * LANES:(j + 1) * LANES] = y


def _pack_w_in(w_in, n_q, n_kv, n_qi, n_ret):
    d = w_in.shape[0]
    sizes = (n_q, n_kv, n_kv, n_qi, IDX_DIM, IDX_HEADS, n_ret, n_ret, n_ret, n_ret)
    offs = np.concatenate([[0], np.cumsum(sizes)])
    seg = [w_in[:, offs[i]:offs[i + 1]] for i in range(len(sizes))]
    pad = jnp.zeros((d, 2 * LANES - IDX_DIM - IDX_HEADS), w_in.dtype)
    packed = jnp.concatenate(seg[:4] + [seg[4], seg[5], pad] + seg[6:], axis=1)
    return packed.astype(BF16)


def _inproj(x2d, pos, g_mix, w_packed, *, seq, tm, dims):
    n_q, n_kv, n_qi, n_ret = dims
    rows, d = x2d.shape
    nb = rows // seq
    spt = seq // tm
    c64, s64 = _rope_tables(pos, HEAD_DIM)
    c128, s128 = _rope_tables(pos, RET_DK)
    wcols = w_packed.shape[1]

    row = lambda w: pl.BlockSpec((tm, w), lambda i: (i, 0))
    tab = pl.BlockSpec((tm, LANES), lambda i: (i % spt, 0))
    tr = lambda h: pl.BlockSpec((None, h, tm), lambda i: (i // spt, 0, i % spt))
    out_shape = (
        jax.ShapeDtypeStruct((nb, n_q, seq), BF16),
        jax.ShapeDtypeStruct((nb, n_kv, seq), F32),
        jax.ShapeDtypeStruct((nb, n_kv, seq), F32),
        jax.ShapeDtypeStruct((nb, n_qi, seq), BF16),
        jax.ShapeDtypeStruct((nb, IDX_DIM, seq), F32),
        jax.ShapeDtypeStruct((nb, IDX_HEADS, seq), F32),
        jax.ShapeDtypeStruct((rows, n_ret), BF16),
        jax.ShapeDtypeStruct((nb, n_ret, seq), BF16),
        jax.ShapeDtypeStruct((rows, n_ret), BF16),
        jax.ShapeDtypeStruct((rows, n_ret), F32),
        jax.ShapeDtypeStruct((rows, 4 * n_kv), BF16),
        jax.ShapeDtypeStruct((nb, 4 * n_kv, seq), BF16),
        jax.ShapeDtypeStruct((rows, 2 * LANES), BF16),
    )
    out_specs = (tr(n_q), tr(n_kv), tr(n_kv), tr(n_qi), tr(IDX_DIM), tr(IDX_HEADS),
                 row(n_ret), tr(n_ret), row(n_ret), row(n_ret), row(4 * n_kv), tr(4 * n_kv), row(2 * LANES))
    return pl.pallas_call(
        functools.partial(_inproj_kernel, n_q=n_q, n_qi=n_qi, n_ret=n_ret),
        out_shape=out_shape,
        grid=(rows // tm,),
        in_specs=[row(d), _const_spec((1, d)), _const_spec((d, wcols)), tab, tab, tab, tab],
        out_specs=out_specs,
        compiler_params=_cparams("parallel"),
        name="inproj",
    )(x2d, g_mix.reshape(1, d), w_packed, c64, s64, c128, s128)


N_ACC = 4


def _fold_keys(st_ref, nch, ck, init, fn, combine):
    def body(c, carries):
        carries = list(carries)
        base = pl.multiple_of(c * ck, ck)
        chunk = st_ref[pl.ds(base, ck), :]
        for j in range(ck // SUBLANES):
            blk = chunk[j * SUBLANES:(j + 1) * SUBLANES, :]
            carries[j % N_ACC] = fn(carries[j % N_ACC], blk, base + j * SUBLANES)
        return tuple(carries)
    carries = lax.fori_loop(0, nch, body, (init,) * N_ACC)
    out = carries[0]
    for other in carries[1:]:
        out = combine(out, other)
    return out


def _all_sublanes(x, op):
    for shift in (4, 2, 1):
        x = op(x, pltpu.roll(x, shift, 0))
    return x


def _any(mask):
    return jnp.max(jnp.where(mask, 1.0, 0.0)) > 0.5


def _count(st_ref, nch, ck, pred):
    nq = st_ref.shape[1]
    acc = _fold_keys(st_ref, nch, ck, jnp.zeros((SUBLANES, nq), F32),
                     lambda a, blk, row0: a + jnp.where(pred(blk, row0), 1.0, 0.0), jnp.add)
    return _all_sublanes(acc, jnp.add)


def _select_threshold(st_ref, thr_ref, nch, ck, rmin, rmax, topk, n_fast):
    nq = st_ref.shape[1]
    kf = float(topk)
    full = lambda v: jnp.full((SUBLANES, nq), v, F32)

    def fast_body(_, carry):
        lo, hi = carry
        mid = lo + (hi - lo) * 0.5
        up = _count(st_ref, nch, ck, lambda blk, _r: blk > mid) >= kf
        return jnp.where(up, mid, lo), jnp.where(up, hi, mid)

    lo, hi = lax.fori_loop(0, n_fast, fast_body, (rmin, rmax))
    unres = _count(st_ref, nch, ck, lambda blk, _r: blk >= lo) > kf
    thr_ref[...] = lo

    @pl.when(_any(unres))
    def _exact():
        pair_minmax = lambda x, y: (jnp.minimum(x[0], y[0]), jnp.maximum(x[1], y[1]))
        a, b = _fold_keys(
            st_ref, nch, ck, (full(POS), full(NEG)),
            lambda cr, blk, _r: (jnp.minimum(cr[0], jnp.where(blk >= lo, blk, POS)),
                                 jnp.maximum(cr[1], jnp.where(blk <= hi, blk, NEG))),
            pair_minmax)
        lo2 = jnp.where(unres, _all_sublanes(a, jnp.minimum), lo)
        ub = jnp.where(unres, _all_sublanes(b, jnp.maximum), lo)

        def body(carry):
            lo2, ub, _ = carry
            mid = lo2 + (ub - lo2) * 0.5
            mid = jnp.where(mid >= ub, lo2, mid)

            def step(cr, blk, _r):
                gt = blk > mid
                return (cr[0] + jnp.where(gt, 1.0, 0.0), jnp.minimum(cr[1], jnp.where(gt, blk, POS)),
                        jnp.maximum(cr[2], jnp.where(gt, NEG, blk)))
            cnt, a, b = _fold_keys(st_ref, nch, ck, (full(0.0), full(POS), full(NEG)), step,
                                   lambda x, y: (x[0] + y[0],) + pair_minmax(x[1:], y[1:]))
            up = _all_sublanes(cnt, jnp.add) >= kf
            active = lo2 < ub
            lo2n = jnp.where(active & up, _all_sublanes(a, jnp.minimum), lo2)
            ubn = jnp.where(active & jnp.logical_not(up), _all_sublanes(b, jnp.maximum), ub)
            return lo2n, ubn, _any(lo2n < ubn).astype(jnp.int32)

        v, _, _ = lax.while_loop(lambda carry: carry[2] > 0, body, (lo2, ub, _any(lo2 < ub).astype(jnp.int32)))
        thr_ref[...] = jnp.where(unres, v, lo)
        tied = unres & (_count(st_ref, nch, ck, lambda blk, _r: blk >= v) > kf)

        @pl.when(_any(tied))
        def _ties():
            _drop_excess_ties(st_ref, nch, ck, v, tied, kf)


def _drop_excess_ties(st_ref, nch, ck, v, tied, kf):
    nq = st_ref.shape[1]
    full = lambda x: jnp.full((SUBLANES, nq), x, F32)
    need = kf - _count(st_ref, nch, ck, lambda blk, _r: blk > v)
    nkeys = st_ref.shape[0]
    sub = lax.broadcasted_iota(jnp.int32, (SUBLANES, nq), 0)
    key_index = lambda row0: (sub + row0).astype(F32)

    def j_body(_, carry):
        lo_j, hi_j = carry
        mid_j = jnp.floor((lo_j + hi_j) * 0.5)
        ok = _count(st_ref, nch, ck, lambda blk, row0: (blk == v) & (key_index(row0) <= mid_j)) >= need
        return jnp.where(ok, lo_j, mid_j), jnp.where(ok, mid_j, hi_j)

    n_j = int(np.ceil(np.log2(nkeys))) + 1
    _, j_last = lax.fori_loop(0, n_j, j_body, (full(-1.0), full(float(nkeys - 1))))

    def drop_body(c, _):
        base = pl.multiple_of(c * ck, ck)
        for j in range(ck // SUBLANES):
            rows = pl.ds(base + j * SUBLANES, SUBLANES)
            blk = st_ref[rows, :]
            drop = tied & (blk == v) & (key_index(base + j * SUBLANES) > j_last)
            st_ref[rows, :] = jnp.where(drop, NEG, blk)
        return 0
    lax.fori_loop(0, nch, drop_body, 0)


def _rows_min(x):
    return jnp.min(x.reshape(x.shape[0] // SUBLANES, SUBLANES, x.shape[1]), axis=0)


def _rows_max(x):
    return jnp.max(x.reshape(x.shape[0] // SUBLANES, SUBLANES, x.shape[1]), axis=0)


def _rows_sum(x):
    return jnp.sum(x.reshape(x.shape[0] // SUBLANES, SUBLANES, x.shape[1]), axis=0)


def _dsa_prompt_kernel(kip_ref, qit_ref, wit_ref, kp_ref, qt_ref, vtt_ref, o_ref, st_ref, thr_ref,
                       m_ref, l_ref, acc_ref, lg_ref, pr_ref, *, tq, topk, n_fast, rs):
    ck = tq
    i = pl.program_id(1)
    nch = i + 1
    n_pairs = qt_ref.shape[0] // LANES
    pairs_per_kv = n_pairs // KV_HEADS
    bcast = lambda row, n: jnp.broadcast_to(row, (n, tq))

    key_l = lax.broadcasted_iota(jnp.int32, (rs, tq), 0)
    qry_l = lax.broadcasted_iota(jnp.int32, (rs, tq), 1)
    w_rows = [bcast(wit_ref[h:h + 1, :], rs) for h in range(IDX_HEADS)]

    def score_body(c, carry):
        mn, mx = carry
        base = pl.multiple_of(c * ck, ck)
        off_diag = c < i
        for r in range(ck // rs):
            rows = pl.ds(base + r * rs, rs)
            k_lo = kip_ref[rows, 0:LANES]
            k_hi = kip_ref[rows, LANES:2 * LANES]
            acc = jnp.zeros((rs, tq), F32)
            for p in range(IDX_HEADS // 2):
                rhs = qit_ref[p * LANES:(p + 1) * LANES, :]
                acc = acc + jnp.maximum(jnp.dot(k_lo, rhs, preferred_element_type=F32), 0.0) * w_rows[2 * p]
                acc = acc + jnp.maximum(jnp.dot(k_hi, rhs, preferred_element_type=F32), 0.0) * w_rows[2 * p + 1]
            valid = off_diag | (key_l + r * rs <= qry_l)
            st_ref[rows, :] = jnp.where(valid, acc, NEG)
            mn = jnp.minimum(mn, _rows_min(jnp.where(valid, acc, POS)))
            mx = jnp.maximum(mx, _rows_max(jnp.where(valid, acc, NEG)))
        return mn, mx

    mn, mx = lax.fori_loop(0, nch, score_body,
                           (jnp.full((SUBLANES, tq), POS, F32), jnp.full((SUBLANES, tq), NEG, F32)))
    _select_threshold(st_ref, thr_ref, nch, ck, _all_sublanes(mn, jnp.minimum), _all_sublanes(mx, jnp.maximum),
                      topk, n_fast)

    thr = bcast(thr_ref[0:1, :], ck)
    m_ref[...] = jnp.full(m_ref.shape, SOFTMAX_NEG, F32)
    l_ref[...] = jnp.zeros(l_ref.shape, F32)
    acc_ref[...] = jnp.zeros(acc_ref.shape, F32)

    def att_body(c, _):
        rows = pl.ds(pl.multiple_of(c * ck, ck), ck)
        sel = st_ref[rows, :] >= thr
        slab_of = lambda h: 2 * (h // 2 // pairs_per_kv) + h % 2
        n_heads = 2 * n_pairs
        for h in range(n_heads):
            slab = slab_of(h)
            kx = kp_ref[rows, slab * LANES:(slab + 1) * LANES]
            qt_pair = qt_ref[(h // 2) * LANES:(h // 2 + 1) * LANES, :]
            lg_ref[h] = jnp.where(sel, jnp.dot(kx, qt_pair, preferred_element_type=F32), SOFTMAX_NEG)
        alpha = []
        for h in range(n_heads):
            lg = lg_ref[h]
            m_old = m_ref[h]
            m_new = jnp.maximum(m_old, _all_sublanes(_rows_max(lg), jnp.maximum))
            alpha.append(jnp.exp2(m_old - m_new))
            m_ref[h] = m_new
            pr_ref[h] = jnp.exp2(lg - bcast(m_new[0:1, :], ck)).astype(BF16)
        ones = jnp.ones((ONES_ROWS, ck), BF16)
        for p in range(n_pairs):
            pv = []
            for h in (2 * p, 2 * p + 1):
                vt1 = jnp.concatenate([vtt_ref[slab_of(h) * LANES:(slab_of(h) + 1) * LANES, rows], ones], axis=0)
                pv1 = jnp.dot(vt1, pr_ref[h], preferred_element_type=F32)
                l_ref[h] = alpha[h] * l_ref[h] + pv1[LANES:LANES + SUBLANES, :]
                pv.append(pv1[:LANES, :])
            scale = jnp.concatenate([bcast(alpha[2 * p][0:1, :], HEAD_DIM), bcast(alpha[2 * p + 1][0:1, :], HEAD_DIM)],
                                    axis=0)
            acc_ref[p] = acc_ref[p] * scale + pv[0] + pv[1]
        return 0

    lax.fori_loop(0, nch, att_body, 0)
    for p in range(n_pairs):
        denom = jnp.concatenate([bcast(l_ref[2 * p, 0:1, :], HEAD_DIM), bcast(l_ref[2 * p + 1, 0:1, :], HEAD_DIM)],
                                axis=0)
        o_ref[:, p * LANES:(p + 1) * LANES] = (acc_ref[p] / denom).T.astype(o_ref.dtype)


def _dsa_prompt(kip, qit, wit, kp, qt, vtt, *, nb, seq, tq, topk, n_fast):
    n_q = qt.shape[1]
    spb = seq // tq
    per_b_rows = lambda a: pl.BlockSpec((seq, a.shape[1]), lambda b, i: (b, 0))
    q_cols = lambda a: pl.BlockSpec((None, a.shape[1], tq), lambda b, i: (b, 0, i))
    return pl.pallas_call(
        functools.partial(_dsa_prompt_kernel, tq=tq, topk=topk, n_fast=n_fast, rs=min(tq, 128)),
        out_shape=jax.ShapeDtypeStruct((nb * seq, n_q), BF16),
        grid=(nb, spb),
        in_specs=[per_b_rows(kip), q_cols(qit), q_cols(wit), per_b_rows(kp), q_cols(qt),
                  pl.BlockSpec((None, vtt.shape[1], seq), lambda b, i: (b, 0, 0))],
        out_specs=pl.BlockSpec((tq, n_q), lambda b, i: (b * spb + i, 0)),
        scratch_shapes=[pltpu.VMEM((seq, tq), F32), pltpu.VMEM((SUBLANES, tq), F32),
                        pltpu.VMEM((n_q // HEAD_DIM, SUBLANES, tq), F32),
                        pltpu.VMEM((n_q // HEAD_DIM, SUBLANES, tq), F32),
                        pltpu.VMEM((n_q // LANES, LANES, tq), F32),
                        pltpu.VMEM((n_q // HEAD_DIM, tq, tq), F32),
                        pltpu.VMEM((n_q // HEAD_DIM, tq, tq), BF16)],
        compiler_params=_cparams("parallel", "arbitrary"),
        name="dsa_prompt",
    )(kip, qit, wit, kp, qt, vtt)


def _retention_tables(chunk, n_heads):
    log_g = jnp.log1p(-jnp.exp2(-5.0 - jnp.arange(n_heads, dtype=F32)))
    i = jnp.arange(chunk, dtype=F32)
    diff = i[:, None] - i[None, :]
    decay = jnp.where(diff >= 0, jnp.exp(log_g[:, None, None] * jnp.maximum(diff, 0.0)), 0.0)
    q_dec = jnp.exp(log_g[:, None] * (i[None, :] + 1.0))
    k_dec = jnp.exp(log_g[:, None] * (chunk - 1.0 - i)[None, :])
    c_dec = jnp.exp(log_g * chunk)
    return decay, q_dec, k_dec, c_dec


def _group_norm_gate(o, rg, gain):
    mu = jnp.mean(o, axis=-1, keepdims=True)
    var = jnp.mean(jnp.square(o - mu), axis=-1, keepdims=True)
    on = ((o - mu) * lax.rsqrt(var + GN_EPS)) * gain
    return jax.nn.silu(rg) * on


def _retention_prompt_kernel(rq_ref, rkt_ref, rv_ref, rg_ref, gain_ref, dmat_ref, qdec_ref, kdec_ref, cdec_ref,
                             o_ref, st_ref, state_ref, *, chunk, n_heads):
    j = pl.program_id(1)

    @pl.when(j == 0)
    def _():
        state_ref[...] = jnp.zeros_like(state_ref)

    for cc in range(rq_ref.shape[0] // chunk):
        rows = slice(cc * chunk, (cc + 1) * chunk)
        for h in range(n_heads):
            cols = slice(h * RET_DV, (h + 1) * RET_DV)
            q = rq_ref[rows, cols]
            kt = rkt_ref[cols, rows]
            v = rv_ref[rows, cols]
            state = state_ref[h]
            inner = jnp.dot(q, kt, preferred_element_type=F32) * dmat_ref[h]
            o = jnp.dot(inner.astype(BF16), v, preferred_element_type=F32)
            o = o + jnp.dot(q, state.astype(BF16), preferred_element_type=F32) * qdec_ref[h]
            kd = (kt.astype(F32) * kdec_ref[h]).astype(BF16)
            state_ref[h] = cdec_ref[h] * state + jnp.dot(kd, v, preferred_element_type=F32)
            o_ref[rows, cols] = _group_norm_gate(o, rg_ref[rows, cols], gain_ref[:, cols]).astype(o_ref.dtype)

    @pl.when(j == pl.num_programs(1) - 1)
    def _():
        st_ref[...] = state_ref[...]


def _retention_prompt(rq, rkt, rv, rg, g_retn, *, nb, seq, tr):
    n_ret = rq.shape[1]
    n_heads = n_ret // RET_DV
    chunk = RET_CHUNK
    decay, q_dec, k_dec, c_dec = _retention_tables(chunk, n_heads)
    qdec_b = jnp.broadcast_to(q_dec[:, :, None], (n_heads, chunk, LANES))
    kdec_b = k_dec[:, None, :]
    cdec_b = jnp.broadcast_to(c_dec[:, None, None], (n_heads, 1, LANES))
    spb = seq // tr
    row = lambda: pl.BlockSpec((tr, n_ret), lambda b, j: (b * spb + j, 0))
    return pl.pallas_call(
        functools.partial(_retention_prompt_kernel, chunk=chunk, n_heads=n_heads),
        out_shape=(jax.ShapeDtypeStruct((nb * seq, n_ret), BF16),
                   jax.ShapeDtypeStruct((nb, n_heads, RET_DK, RET_DV), F32)),
        grid=(nb, spb),
        in_specs=[row(), pl.BlockSpec((None, n_ret, tr), lambda b, j: (b, 0, j)), row(), row(),
                  _const_spec((1, n_ret)), _const_spec((n_heads, chunk, chunk)),
                  _const_spec((n_heads, chunk, LANES)), _const_spec((n_heads, 1, chunk)),
                  _const_spec((n_heads, 1, LANES))],
        out_specs=(row(), pl.BlockSpec((None, n_heads, RET_DK, RET_DV), lambda b, j: (b, 0, 0, 0))),
        scratch_shapes=[pltpu.VMEM((n_heads, RET_DK, RET_DV), F32)],
        compiler_params=_cparams("parallel", "arbitrary"),
        name="retention_prompt",
    )(rq, rkt, rv, rg, g_retn.reshape(1, n_ret), decay, qdec_b, kdec_b, cdec_b)


def _rms(x, g):
    return (x * lax.rsqrt(jnp.mean(x * x, axis=-1, keepdims=True) + EPS)) * g


def _tail_kernel(h_ref, attn_ref, ret_ref, p_ref, wo_a_ref, wo_r_ref, g_ffn_ref, w_gate_ref, w_up_ref, w_down_ref,
                 g_ple_ref, w_pg_ref, w_pp_ref, g_fin_ref, y_ref):
    dot = functools.partial(jnp.dot, preferred_element_type=F32)
    h = h_ref[...] + dot(attn_ref[...], wo_a_ref[...]) + dot(ret_ref[...], wo_r_ref[...])
    f = _rms(h, g_ffn_ref[...]).astype(BF16)
    act = (jax.nn.silu(dot(f, w_gate_ref[...])) * dot(f, w_up_ref[...])).astype(BF16)
    h = h + dot(act, w_down_ref[...])
    gate = jax.nn.sigmoid(dot(_rms(h, g_ple_ref[...]).astype(BF16), w_pg_ref[...]))
    h = h + gate * dot(p_ref[...].astype(BF16), w_pp_ref[...])
    y_ref[...] = _rms(h, g_fin_ref[...])


def _tail(h2d, attn, ret, p2d, w_out, g_ffn, w_ffn_in, w_ffn_out, g_ple, w_ple_gate, w_ple_proj, g_final, *, tm):
    rows, d = h2d.shape
    n_attn = attn.shape[1]
    d_ff = w_ffn_out.shape[0]
    ws = [w_out[:n_attn].astype(BF16), w_out[n_attn:].astype(BF16), g_ffn.reshape(1, d),
          w_ffn_in[:, :d_ff].astype(BF16), w_ffn_in[:, d_ff:].astype(BF16), w_ffn_out.astype(BF16),
          g_ple.reshape(1, d), w_ple_gate.astype(BF16), w_ple_proj.astype(BF16), g_final.reshape(1, d)]
    row = lambda w: pl.BlockSpec((tm, w), lambda i: (i, 0))
    return pl.pallas_call(
        _tail_kernel,
        out_shape=jax.ShapeDtypeStruct((rows, d), F32),
        grid=(rows // tm,),
        in_specs=[row(d), row(n_attn), row(ret.shape[1]), row(p2d.shape[1])] + [_const_spec(w.shape) for w in ws],
        out_specs=row(d),
        compiler_params=_cparams("parallel"),
        name="tail",
    )(h2d, attn, ret, p2d, *ws)


def _page_copies(pools, bufs, sems, pt_ref, b, c, slot, pages_per_chunk):
    copies = []
    for pool, buf in zip(pools, bufs):
        for p in range(pages_per_chunk):
            page = pt_ref[b, c * pages_per_chunk + p]
            copies.append(pltpu.make_async_copy(pool.at[page], buf.at[slot, p], sems.at[slot]))
    return copies


def _chunk_keys_on_lanes(pages):
    return jnp.concatenate([pages[p] for p in range(pages.shape[0])], axis=1).astype(BF16)


def _stream_chunks(pools, bufs, sems, pt_ref, n_chunks, pages_per_chunk, compute):
    b = pl.program_id(0)
    nb = pl.num_programs(0)

    def start(bb, cc, slot):
        for cp in _page_copies(pools, bufs, sems, pt_ref, bb, cc, slot, pages_per_chunk):
            cp.start()

    @pl.when(b == 0)
    def _():
        start(0, 0, 0)

    assert n_chunks % 2 == 0
    for c in range(n_chunks):
        slot = c % 2
        if c + 1 < n_chunks:
            start(b, c + 1, 1 - slot)
        else:
            @pl.when(b + 1 < nb)
            def _():
                start(b + 1, 0, 1 - slot)
        for cp in _page_copies(pools, bufs, sems, pt_ref, b, c, slot, pages_per_chunk):
            cp.wait()
        compute(c, slot)


def _tile_lanes(x, width):
    return x if width == LANES else jnp.concatenate([x] * (width // LANES), axis=1)


def _nt_dot(a, b):
    return lax.dot_general(a, b, (((1,), (1,)), ((), ())), preferred_element_type=F32)


def _head_sum(x, n_t):
    return jnp.sum(x.reshape(n_t, IDX_HEADS, x.shape[-1]), axis=1)


def _sample_scores_kernel(pt_ref, qi_ref, wi_ref, kinew_ref, pool_ref, s_ref, buf, sems,
                          *, n_t, n_chunks, pages_per_chunk):
    ck = pages_per_chunk * PAGE_SIZE
    qi = qi_ref[...]
    w = wi_ref[...]

    def compute(c, slot):
        a = jnp.dot(qi, _chunk_keys_on_lanes(buf[slot]), preferred_element_type=F32)
        s_ref[:, c * ck:(c + 1) * ck] = _head_sum(jnp.maximum(a, 0.0) * w, n_t)

    _stream_chunks([pool_ref], [buf], sems, pt_ref, n_chunks, pages_per_chunk, compute)

    a = jnp.dot(qi, kinew_ref[...], preferred_element_type=F32)
    s_new = _head_sum(jnp.maximum(a, 0.0) * w, n_t)
    t_q = lax.broadcasted_iota(jnp.int32, s_new.shape, 0)
    t_k = lax.broadcasted_iota(jnp.int32, s_new.shape, 1)
    s_ref[:, n_chunks * ck:] = jnp.where(t_k <= t_q, s_new, NEG)


def _sample_scores(page_table, qi_s, wi_s, kinew, pool_ik, *, n_t, pages_per_chunk):
    db, n_pages = page_table.shape
    n_chunks = n_pages // pages_per_chunk
    past = n_pages * PAGE_SIZE
    rows = n_t * IDX_HEADS
    grid_spec = pltpu.PrefetchScalarGridSpec(
        num_scalar_prefetch=1,
        grid=(db,),
        in_specs=[pl.BlockSpec((None, rows, IDX_DIM), lambda b, pt: (b, 0, 0)),
                  pl.BlockSpec((None, rows, 1), lambda b, pt: (b, 0, 0)),
                  pl.BlockSpec((None, IDX_DIM, PAGE_SIZE), lambda b, pt: (b, 0, 0)),
                  pl.BlockSpec(memory_space=pl.ANY)],
        out_specs=pl.BlockSpec((None, n_t, past + PAGE_SIZE), lambda b, pt: (b, 0, 0)),
        scratch_shapes=[pltpu.VMEM((2, pages_per_chunk, IDX_DIM, PAGE_SIZE), F32),
                        pltpu.SemaphoreType.DMA((2,))],
    )
    return pl.pallas_call(
        functools.partial(_sample_scores_kernel, n_t=n_t, n_chunks=n_chunks, pages_per_chunk=pages_per_chunk),
        out_shape=jax.ShapeDtypeStruct((db, n_t, past + PAGE_SIZE), F32),
        grid_spec=grid_spec,
        compiler_params=_cparams("arbitrary"),
        name="sample_scores",
    )(page_table, qi_s, wi_s, kinew, pool_ik)


def _sample_select_kernel(s_in_ref, s_out_ref, thr_ref, st_ref, thr8_ref, *, ck, topk, n_fast):
    nq, nkeys = s_in_ref.shape
    nch = nkeys // ck + 0 * pl.program_id(0)
    chunk = lambda c: pl.ds(pl.multiple_of(c * ck, ck), ck)

    def to_lanes(c, _):
        st_ref[chunk(c), :] = s_in_ref[:, chunk(c)].T
        return 0
    lax.fori_loop(0, nch, to_lanes, 0)

    mn, mx = _fold_keys(
        st_ref, nch, ck, (jnp.full((SUBLANES, nq), POS, F32), jnp.full((SUBLANES, nq), NEG, F32)),
        lambda cr, blk, _r: (jnp.minimum(cr[0], jnp.where(blk > 0.5 * NEG, blk, POS)), jnp.maximum(cr[1], blk)),
        lambda x, y: (jnp.minimum(x[0], y[0]), jnp.maximum(x[1], y[1])))
    _select_threshold(st_ref, thr8_ref, nch, ck, _all_sublanes(mn, jnp.minimum), _all_sublanes(mx, jnp.maximum),
                      topk, n_fast)
    def to_rows(c, _):
        s_out_ref[:, chunk(c)] = st_ref[chunk(c), :].T
        return 0
    lax.fori_loop(0, nch, to_rows, 0)
    thr_ref[...] = jnp.broadcast_to(thr8_ref[0:1, :], (LANES, nq)).T


def _sample_select(s2d, *, topk, n_fast):
    rows, ncols = s2d.shape
    ck = max(w for w in (LANES, 2 * LANES, 3 * LANES, 4 * LANES) if ncols % w == 0)
    full = pl.BlockSpec((rows, ncols), lambda i: (0, 0), pipeline_mode=pl.Buffered(1))
    return pl.pallas_call(
        functools.partial(_sample_select_kernel, ck=ck, topk=topk, n_fast=n_fast),
        out_shape=(jax.ShapeDtypeStruct((rows, ncols), F32), jax.ShapeDtypeStruct((rows, LANES), F32)),
        grid=(1,),
        in_specs=[full],
        out_specs=(full, pl.BlockSpec((rows, LANES), lambda i: (0, 0))),
        scratch_shapes=[pltpu.VMEM((ncols, rows), F32), pltpu.VMEM((SUBLANES, rows), F32)],
        compiler_params=_cparams("arbitrary"),
        name="sample_select",
    )(s2d)


def _sample_attend_kernel(pt_ref, q_ref, s_ref, thr_ref, knew_ref, vnew_ref, poolk_ref, poolv_ref, o_ref,
                          kbuf, vbuf, sems, *, n_t, n_chunks, pages_per_chunk):
    ck = pages_per_chunk * PAGE_SIZE
    rows = q_ref.shape[1]
    reps = rows // n_t
    thr = jnp.concatenate([thr_ref[...]] * reps, axis=0)
    state = [[jnp.full((rows, 1), SOFTMAX_NEG, F32), jnp.zeros((rows, 1), F32), jnp.zeros((rows, HEAD_DIM), F32)]
             for _ in range(KV_HEADS)]

    def attend(kt, vt, s_blk):
        width = s_blk.shape[1]
        sel = jnp.concatenate([s_blk] * reps, axis=0) >= _tile_lanes(thr, width)
        for g in range(KV_HEADS):
            m, l, acc = state[g]
            lg = jnp.where(sel, jnp.dot(q_ref[g], kt[g], preferred_element_type=F32), SOFTMAX_NEG)
            m_new = jnp.maximum(m, jnp.max(lg, axis=1, keepdims=True))
            alpha = jnp.exp2(m - m_new)
            pr = jnp.exp2(lg - m_new)
            state[g] = [m_new, alpha * l + jnp.sum(pr, axis=1, keepdims=True),
                        alpha * acc + _nt_dot(pr.astype(BF16), vt[g])]

    def compute(c, slot):
        attend([_chunk_keys_on_lanes(kbuf[slot, :, g]) for g in range(KV_HEADS)],
               [_chunk_keys_on_lanes(vbuf[slot, :, g]) for g in range(KV_HEADS)], s_ref[:, c * ck:(c + 1) * ck])

    _stream_chunks([poolk_ref, poolv_ref], [kbuf, vbuf], sems, pt_ref, n_chunks, pages_per_chunk, compute)
    attend(knew_ref[...], vnew_ref[...], s_ref[:, n_chunks * ck:])
    for g in range(KV_HEADS):
        o_ref[g] = state[g][2] / state[g][1]


def _sample_attend(page_table, q_bd, s3d, thr3d, knew, vnew, pool_k, pool_v, *, n_t, pages_per_chunk):
    db, n_pages = page_table.shape
    n_chunks = n_pages // pages_per_chunk
    rows = q_bd.shape[2]
    ncols = s3d.shape[2]
    per_b = lambda r, w: pl.BlockSpec((None, r, w), lambda b, pt: (b, 0, 0))
    per_b4 = lambda r, w: pl.BlockSpec((None, KV_HEADS, r, w), lambda b, pt: (b, 0, 0, 0))
    grid_spec = pltpu.PrefetchScalarGridSpec(
        num_scalar_prefetch=1,
        grid=(db,),
        in_specs=[per_b4(rows, HEAD_DIM), per_b(n_t, ncols), per_b(n_t, LANES), per_b4(HEAD_DIM, PAGE_SIZE),
                  per_b4(HEAD_DIM, PAGE_SIZE), pl.BlockSpec(memory_space=pl.ANY), pl.BlockSpec(memory_space=pl.ANY)],
        out_specs=per_b4(rows, HEAD_DIM),
        scratch_shapes=[pltpu.VMEM((2, pages_per_chunk, KV_HEADS, HEAD_DIM, PAGE_SIZE), F32),
                        pltpu.VMEM((2, pages_per_chunk, KV_HEADS, HEAD_DIM, PAGE_SIZE), F32),
                        pltpu.SemaphoreType.DMA((2,))],
    )
    return pl.pallas_call(
        functools.partial(_sample_attend_kernel, n_t=n_t, n_chunks=n_chunks, pages_per_chunk=pages_per_chunk),
        out_shape=jax.ShapeDtypeStruct((db, KV_HEADS, rows, HEAD_DIM), F32),
        grid_spec=grid_spec,
        compiler_params=_cparams("arbitrary"),
        name="sample_attend",
    )(page_table, q_bd, s3d, thr3d, knew, vnew, pool_k, pool_v)


def _retention_sample_kernel(rq_ref, rkt_ref, rv_ref, rg_ref, gain_ref, dfull_ref, qdec_ref, kdec_ref, cdec_ref,
                             st_in_ref, o_ref, st_out_ref, *, n_t, n_heads):
    b = pl.program_id(0)

    @pl.when(b == 0)
    def _():
        o_ref[...] = jnp.zeros_like(o_ref)

    n_rows = rq_ref.shape[0]
    row = lax.broadcasted_iota(jnp.int32, (n_rows, RET_DV), 0)
    mine = (row >= b * n_t) & (row < (b + 1) * n_t)
    for h in range(n_heads):
        cols = slice(h * RET_DV, (h + 1) * RET_DV)
        q = jnp.where(mine, rq_ref[:, cols], 0.0).astype(BF16)
        v = rv_ref[:, cols]
        kt = rkt_ref[cols, :]
        state = st_in_ref[h]
        inner = jnp.dot(q, kt, preferred_element_type=F32) * dfull_ref[h]
        o = jnp.dot(inner.astype(BF16), v, preferred_element_type=F32)
        o = o + jnp.dot(q, state.astype(BF16), preferred_element_type=F32) * qdec_ref[h]
        vd = jnp.where(mine, v.astype(F32) * kdec_ref[h], 0.0).astype(BF16)
        st_out_ref[h] = cdec_ref[h] * state + jnp.dot(kt, vd, preferred_element_type=F32)
        gated = _group_norm_gate(o, rg_ref[:, cols], gain_ref[:, cols])
        o_ref[:, cols] = jnp.where(mine, gated, o_ref[:, cols].astype(F32)).astype(o_ref.dtype)


def _retention_sample(rq, rkt, rv, rg, g_retn, state, *, db, n_t):
    n_rows, n_ret = rq.shape
    n_heads = n_ret // RET_DV
    decay, q_dec, k_dec, c_dec = _retention_tables(n_t, n_heads)
    same = jnp.kron(jnp.eye(db, dtype=F32), jnp.ones((n_t, n_t), F32))
    dfull = same[None] * jnp.tile(decay, (1, db, db))
    qdec_b = jnp.broadcast_to(jnp.tile(q_dec, (1, db))[:, :, None], (n_heads, n_rows, RET_DV))
    kdec_b = jnp.broadcast_to(jnp.tile(k_dec, (1, db))[:, :, None], (n_heads, n_rows, RET_DV))
    cdec_b = jnp.broadcast_to(c_dec[:, None, None], (n_heads, 1, RET_DV))
    const = lambda shape: pl.BlockSpec(shape, lambda b: (0,) * len(shape))
    st_spec = pl.BlockSpec((None, n_heads, RET_DK, RET_DV), lambda b: (b, 0, 0, 0))
    return pl.pallas_call(
        functools.partial(_retention_sample_kernel, n_t=n_t, n_heads=n_heads),
        out_shape=(jax.ShapeDtypeStruct((n_rows, n_ret), BF16),
                   jax.ShapeDtypeStruct((db, n_heads, RET_DK, RET_DV), F32)),
        grid=(db,),
        in_specs=[const((n_rows, n_ret)), const((n_ret, n_rows)), const((n_rows, n_ret)), const((n_rows, n_ret)),
                  const((1, n_ret)), const((n_heads, n_rows, n_rows)), const((n_heads, n_rows, RET_DV)),
                  const((n_heads, n_rows, RET_DV)), const((n_heads, 1, RET_DV)), st_spec],
        out_specs=(const((n_rows, n_ret)), st_spec),
        compiler_params=_cparams("arbitrary"),
        name="retention_sample",
    )(rq, rkt, rv, rg, g_retn.reshape(1, n_ret), dfull, qdec_b, kdec_b, cdec_b, state)


def _mixer_dims(w_in, d_model):
    n_q = d_model // 2
    n_ret = d_model // 2
    n_kv = KV_HEADS * HEAD_DIM
    n_qi = IDX_HEADS * IDX_DIM
    assert n_kv == LANES and 2 * n_kv + n_q + n_qi + IDX_DIM + IDX_HEADS + 4 * n_ret == w_in.shape[1]
    return n_q, n_kv, n_qi, n_ret


def _prompt_layer(x, p, weights, *, n_fast=24):
    g_mix, w_in, g_retn, w_out, g_ffn, w_ffn_in, w_ffn_out, g_ple, w_ple_gate, w_ple_proj, g_final = weights
    nb, seq, d = x.shape
    dims = _mixer_dims(w_in, d)
    w_packed = _pack_w_in(w_in, *dims)
    x2d = x.reshape(nb * seq, d)
    tm = min(512, seq)
    (qt, kt, vt, qit, kit, wit, rq, rkt, rv, rg, kp, vtt, kip) = _inproj(
        x2d, jnp.arange(seq), g_mix, w_packed, seq=seq, tm=tm, dims=dims)
    topk = min(TOPK_MAX, seq // 4)
    attn = _dsa_prompt(kip, qit, wit, kp, qt, vtt, nb=nb, seq=seq, tq=min(256, seq), topk=topk, n_fast=n_fast)
    ret, st = _retention_prompt(rq, rkt, rv, rg, g_retn, nb=nb, seq=seq, tr=min(512, seq))
    y = _tail(x2d, attn, ret, p.reshape(nb * seq, -1), w_out, g_ffn, w_ffn_in, w_ffn_out,
              g_ple, w_ple_gate, w_ple_proj, g_final, tm=min(256, nb * seq))
    heads_last = lambda a: a.reshape(nb, KV_HEADS, HEAD_DIM, seq).transpose(0, 3, 1, 2)
    return y.reshape(nb, seq, d), heads_last(kt), heads_last(vt), kit.transpose(0, 2, 1), st


def _sample_layer(x, p, pool_k, pool_v, pool_ik, state, page_table, weights, *, n_fast=24):
    g_mix, w_in, g_retn, w_out, g_ffn, w_ffn_in, w_ffn_out, g_ple, w_ple_gate, w_ple_proj, g_final = weights
    db, n_t, d = x.shape
    n_rows = db * n_t
    n_pages = page_table.shape[1]
    past = n_pages * PAGE_SIZE
    dims = _mixer_dims(w_in, d)
    n_q = dims[0]
    w_packed = _pack_w_in(w_in, *dims)
    x2d = x.reshape(n_rows, d)
    pos = jnp.tile(past + jnp.arange(n_t), db)
    (qt, kt, vt, qit, kit, wit, rq, rkt, rv, rg, _, _, _) = _inproj(
        x2d, pos, g_mix, w_packed, seq=n_rows, tm=n_rows, dims=dims)
    q, qi, wi = qt[0].T, qit[0].T, wit[0].T

    def new_pages(at, heads):
        a = at[0].reshape(heads, -1, db, n_t).transpose(2, 0, 1, 3)
        return jnp.pad(a, ((0, 0), (0, 0), (0, 0), (0, PAGE_SIZE - n_t))).astype(BF16)
    group = n_q // HEAD_DIM // KV_HEADS
    qg = q.reshape(db, n_t, KV_HEADS, group, HEAD_DIM).transpose(0, 2, 3, 1, 4)
    qg = qg.reshape(db, KV_HEADS, group * n_t, HEAD_DIM)
    pages_per_chunk = max(1, min(32, n_pages // 2))

    s3d = _sample_scores(page_table, qi.reshape(db, n_t * IDX_HEADS, IDX_DIM), wi.reshape(db, n_t * IDX_HEADS, 1),
                         new_pages(kit, 1)[:, 0], pool_ik.transpose(0, 2, 1), n_t=n_t, pages_per_chunk=pages_per_chunk)
    topk = min(TOPK_MAX, (past + n_t) // 4)
    s_sel, thr = _sample_select(s3d.reshape(n_rows, -1), topk=topk, n_fast=n_fast)
    o = _sample_attend(page_table, qg, s_sel.reshape(db, n_t, -1), thr.reshape(db, n_t, LANES),
                       new_pages(kt, KV_HEADS), new_pages(vt, KV_HEADS), pool_k.transpose(0, 2, 3, 1),
                       pool_v.transpose(0, 2, 3, 1), n_t=n_t, pages_per_chunk=pages_per_chunk)
    attn = o.reshape(db, KV_HEADS, group, n_t, HEAD_DIM).transpose(0, 3, 1, 2, 4).reshape(n_rows, n_q).astype(BF16)

    ret, st = _retention_sample(rq, rkt[0], rv, rg, g_retn, state, db=db, n_t=n_t)
    y = _tail(x2d, attn, ret, p.reshape(n_rows, -1), w_out, g_ffn, w_ffn_in, w_ffn_out,
              g_ple, w_ple_gate, w_ple_proj, g_final, tm=n_rows)
    heads_last = lambda a: a[0].reshape(KV_HEADS, HEAD_DIM, db, n_t).transpose(2, 3, 0, 1)
    return y.reshape(db, n_t, d), heads_last(kt), heads_last(vt), kit[0].T.reshape(db, n_t, IDX_DIM), st


def kernel(x_prompt, x_sample, cache_k, cache_v, cache_idx_k, state_retn, page_table, p_prompt, p_sample, g_mix, w_in,
           g_retn, w_out, g_ffn, w_ffn_in, w_ffn_out, g_ple, w_ple_gate, w_ple_proj, g_final):
    depth = w_in.shape[0]
    assert depth == 1, "the final RMSNorm is fused into the single layer's tail kernel"
    weights = (g_mix[0], w_in[0], g_retn[0], w_out[0], g_ffn[0], w_ffn_in[0], w_ffn_out[0], g_ple[0], w_ple_gate[0],
               w_ple_proj[0], g_final)
    y_p, k_p, v_p, ik_p, st_p = _prompt_layer(x_prompt, p_prompt[0], weights)
    y_s, k_s, v_s, ik_s, st_s = _sample_layer(x_sample, p_sample[0], cache_k[0], cache_v[0], cache_idx_k[0],
                                              state_retn[0], page_table, weights)
    return (y_p, y_s, k_p[None], v_p[None], ik_p[None], st_p[None], k_s[None], v_s[None], ik_s[None], st_s[None])
```

```python
import functools

import jax
import jax.numpy as jnp
import numpy as np
from jax import lax
from jax.experimental import pallas as pl
from jax.experimental.pallas import tpu as pltpu

F32 = jnp.float32
BF16 = jnp.bfloat16

HEAD_DIM = 64
KV_HEADS = 2
IDX_HEADS = 8
IDX_DIM = 64
TOPK_MAX = 256
RET_DK = 128
RET_DV = 128
RET_CHUNK = 128
PAGE_SIZE = 128
ROPE_THETA = 10000.0
EPS = 1e-6
GN_EPS = 1e-5

LANES = 128
SUBLANES = 8
VMEM_LIMIT_BYTES = 56 * 1024 * 1024

Q_SCALE = HEAD_DIM ** -0.5 * 1.4426950408889634
ONES_ROWS = 16

NEG = -3.0e38
POS = 3.0e38
SOFTMAX_NEG = -1.0e30


def _cparams(*sem):
    return pltpu.CompilerParams(dimension_semantics=sem, vmem_limit_bytes=VMEM_LIMIT_BYTES)


def _const_spec(shape):
    zeros = (0,) * len(shape)
    return pl.BlockSpec(shape, lambda *_: zeros, pipeline_mode=pl.Buffered(1))


def _rope_tables(pos, dim):
    half = dim // 2
    inv = ROPE_THETA ** (-np.arange(half, dtype=np.float64) / half)
    ang = np.asarray(pos, np.float64)[:, None] * inv[None, :]
    cos, sin = np.cos(ang), np.sin(ang)
    reps = LANES // dim
    cos_t = np.tile(np.concatenate([cos, cos], axis=1), (1, reps))
    sin_t = np.tile(np.concatenate([-sin, sin], axis=1), (1, reps))
    return jnp.asarray(cos_t, F32), jnp.asarray(sin_t, F32)


def _swap_halves_64(x):
    lane = lax.broadcasted_iota(jnp.int32, x.shape, 1)
    first = (lane % 64) < 32
    return jnp.where(first, pltpu.roll(x, 96, 1), pltpu.roll(x, 32, 1))


def _pad_pair_slabs(x, swapped, low):
    return (jnp.where(low, x, 0.0), jnp.where(low, 0.0, swapped), jnp.where(low, swapped, 0.0), jnp.where(low, 0.0, x))


def _inproj_kernel(x_ref, g_ref, w_ref, c64_ref, s64_ref, c128_ref, s128_ref,
                   qt_ref, kt_ref, vt_ref, qit_ref, kit_ref, wit_ref, rq_ref, rkt_ref, rv_ref, rg_ref,
                   kp_ref, vtt_ref, kip_ref, *, n_q, n_qi, n_ret):
    x = x_ref[...]
    tm = x.shape[0]
    ms = jnp.mean(x * x, axis=-1, keepdims=True)
    a = ((x * lax.rsqrt(ms + EPS)) * g_ref[...]).astype(BF16)
    c64, s64 = c64_ref[...], s64_ref[...]
    c128, s128 = c128_ref[...], s128_ref[...]
    low = lax.broadcasted_iota(jnp.int32, (tm, LANES), 1) < HEAD_DIM
    zt = jnp.zeros((HEAD_DIM, tm), BF16)

    col = [0]

    def segment(width):
        y = jnp.dot(a, w_ref[:, col[0]:col[0] + width], preferred_element_type=F32)
        col[0] += width
        return [y[:, j * LANES:(j + 1) * LANES] for j in range(width // LANES)]

    def rope64(y):
        return y * c64 + _swap_halves_64(y) * s64

    def rope128(y):
        return y * c128 + pltpu.roll(y, 64, 1) * s128

    for j, y in enumerate(segment(n_q)):
        qt_ref[j * LANES:(j + 1) * LANES, :] = (rope64(y) * Q_SCALE).T.astype(BF16)
    kk, vv = segment(2 * LANES)
    kk = rope64(kk)
    kt_ref[...] = kk.T
    for s, slab in enumerate(_pad_pair_slabs(kk, pltpu.roll(kk, HEAD_DIM, 1), low)):
        kp_ref[:, s * LANES:(s + 1) * LANES] = slab.astype(BF16)
    vvt = vv.T
    vt_ref[...] = vvt
    vvt = vvt.astype(BF16)
    for g in range(KV_HEADS):
        vtg = vvt[g * HEAD_DIM:(g + 1) * HEAD_DIM, :]
        base = g * 4 * HEAD_DIM
        vtt_ref[base:base + HEAD_DIM, :] = vtg
        vtt_ref[base + HEAD_DIM:base + 2 * HEAD_DIM, :] = zt
        vtt_ref[base + 2 * HEAD_DIM:base + 3 * HEAD_DIM, :] = zt
        vtt_ref[base + 3 * HEAD_DIM:base + 4 * HEAD_DIM, :] = vtg
    for j, y in enumerate(segment(n_qi)):
        qit_ref[j * LANES:(j + 1) * LANES, :] = (rope64(y) * (IDX_DIM ** -0.5)).T.astype(BF16)
    y, _ = segment(2 * LANES)
    ki = rope64(y)
    kit_ref[...] = ki.T[:IDX_DIM, :]
    ki_lo, ki_hi, _, _ = _pad_pair_slabs(ki, pltpu.roll(ki, IDX_DIM, 1), low)
    kip_ref[:, 0:LANES] = ki_lo.astype(BF16)
    kip_ref[:, LANES:2 * LANES] = ki_hi.astype(BF16)
    wit_ref[...] = y.T[IDX_DIM:IDX_DIM + IDX_HEADS, :] * (IDX_HEADS ** -0.5)
    for j, y in enumerate(segment(n_ret)):
        rq_ref[:, j * LANES:(j + 1) * LANES] = rope128(y).astype(BF16)
    for j, y in enumerate(segment(n_ret)):
        rkt_ref[j * LANES:(j + 1) * LANES, :] = (rope128(y) * (RET_DK ** -0.5)).T.astype(BF16)
    for j, y in enumerate(segment(n_ret)):
        rv_ref[:, j * LANES:(j + 1) * LANES] = y.astype(BF16)
    for j, y in enumerate(segment(n_ret)):
        rg_ref[:, j * LANES:(j + 1) * LANES] = y


def _pack_w_in(w_in, n_q, n_kv, n_qi, n_ret):
    d = w_in.shape[0]
    sizes = (n_q, n_kv, n_kv, n_qi, IDX_DIM, IDX_HEADS, n_ret, n_ret, n_ret, n_ret)
    offs = np.concatenate([[0], np.cumsum(sizes)])
    seg = [w_in[:, offs[i]:offs[i + 1]] for i in range(len(sizes))]
    pad = jnp.zeros((d, 2 * LANES - IDX_DIM - IDX_HEADS), w_in.dtype)
    packed = jnp.concatenate(seg[:4] + [seg[4], seg[5], pad] + seg[6:], axis=1)
    return packed.astype(BF16)


def _inproj(x2d, pos, g_mix, w_packed, *, seq, tm, dims):
    n_q, n_kv, n_qi, n_ret = dims
    rows, d = x2d.shape
    nb = rows // seq
    spt = seq // tm
    c64, s64 = _rope_tables(pos, HEAD_DIM)
    c128, s128 = _rope_tables(pos, RET_DK)
    wcols = w_packed.shape[1]

    row = lambda w: pl.BlockSpec((tm, w), lambda i: (i, 0))
    tab = pl.BlockSpec((tm, LANES), lambda i: (i % spt, 0))
    tr = lambda h: pl.BlockSpec((None, h, tm), lambda i: (i // spt, 0, i % spt))
    out_shape = (
        jax.ShapeDtypeStruct((nb, n_q, seq), BF16),
        jax.ShapeDtypeStruct((nb, n_kv, seq), F32),
        jax.ShapeDtypeStruct((nb, n_kv, seq), F32),
        jax.ShapeDtypeStruct((nb, n_qi, seq), BF16),
        jax.ShapeDtypeStruct((nb, IDX_DIM, seq), F32),
        jax.ShapeDtypeStruct((nb, IDX_HEADS, seq), F32),
        jax.ShapeDtypeStruct((rows, n_ret), BF16),
        jax.ShapeDtypeStruct((nb, n_ret, seq), BF16),
        jax.ShapeDtypeStruct((rows, n_ret), BF16),
        jax.ShapeDtypeStruct((rows, n_ret), F32),
        jax.ShapeDtypeStruct((rows, 4 * n_kv), BF16),
        jax.ShapeDtypeStruct((nb, 4 * n_kv, seq), BF16),
        jax.ShapeDtypeStruct((rows, 2 * LANES), BF16),
    )
    out_specs = (tr(n_q), tr(n_kv), tr(n_kv), tr(n_qi), tr(IDX_DIM), tr(IDX_HEADS),
                 row(n_ret), tr(n_ret), row(n_ret), row(n_ret), row(4 * n_kv), tr(4 * n_kv), row(2 * LANES))
    return pl.pallas_call(
        functools.partial(_inproj_kernel, n_q=n_q, n_qi=n_qi, n_ret=n_ret),
        out_shape=out_shape,
        grid=(rows // tm,),
        in_specs=[row(d), _const_spec((1, d)), _const_spec((d, wcols)), tab, tab, tab, tab],
        out_specs=out_specs,
        compiler_params=_cparams("parallel"),
        name="inproj",
    )(x2d, g_mix.reshape(1, d), w_packed, c64, s64, c128, s128)


N_ACC = 4


def _fold_keys(st_ref, nch, ck, init, fn, combine):
    def body(c, carries):
        carries = list(carries)
        base = pl.multiple_of(c * ck, ck)
        chunk = st_ref[pl.ds(base, ck), :]
        for j in range(ck // SUBLANES):
            blk = chunk[j * SUBLANES:(j + 1) * SUBLANES, :]
            carries[j % N_ACC] = fn(carries[j % N_ACC], blk, base + j * SUBLANES)
        return tuple(carries)
    carries = lax.fori_loop(0, nch, body, (init,) * N_ACC)
    out = carries[0]
    for other in carries[1:]:
        out = combine(out, other)
    return out


def _all_sublanes(x, op):
    for shift in (4, 2, 1):
        x = op(x, pltpu.roll(x, shift, 0))
    return x


def _any(mask):
    return jnp.max(jnp.where(mask, 1.0, 0.0)) > 0.5


def _count(st_ref, nch, ck, pred):
    nq = st_ref.shape[1]
    acc = _fold_keys(st_ref, nch, ck, jnp.zeros((SUBLANES, nq), F32),
                     lambda a, blk, row0: a + jnp.where(pred(blk, row0), 1.0, 0.0), jnp.add)
    return _all_sublanes(acc, jnp.add)


def _select_threshold(st_ref, thr_ref, nch, ck, rmin, rmax, topk, n_fast):
    nq = st_ref.shape[1]
    kf = float(topk)
    full = lambda v: jnp.full((SUBLANES, nq), v, F32)

    def fast_body(_, carry):
        lo, hi = carry
        mid = lo + (hi - lo) * 0.5
        up = _count(st_ref, nch, ck, lambda blk, _r: blk > mid) >= kf
        return jnp.where(up, mid, lo), jnp.where(up, hi, mid)

    lo, hi = lax.fori_loop(0, n_fast, fast_body, (rmin, rmax))
    unres = _count(st_ref, nch, ck, lambda blk, _r: blk >= lo) > kf
    thr_ref[...] = lo

    @pl.when(_any(unres))
    def _exact():
        pair_minmax = lambda x, y: (jnp.minimum(x[0], y[0]), jnp.maximum(x[1], y[1]))
        a, b = _fold_keys(
            st_ref, nch, ck, (full(POS), full(NEG)),
            lambda cr, blk, _r: (jnp.minimum(cr[0], jnp.where(blk >= lo, blk, POS)),
                                 jnp.maximum(cr[1], jnp.where(blk <= hi, blk, NEG))),
            pair_minmax)
        lo2 = jnp.where(unres, _all_sublanes(a, jnp.minimum), lo)
        ub = jnp.where(unres, _all_sublanes(b, jnp.maximum), lo)

        def body(carry):
            lo2, ub, _ = carry
            mid = lo2 + (ub - lo2) * 0.5
            mid = jnp.where(mid >= ub, lo2, mid)

            def step(cr, blk, _r):
                gt = blk > mid
                return (cr[0] + jnp.where(gt, 1.0, 0.0), jnp.minimum(cr[1], jnp.where(gt, blk, POS)),
                        jnp.maximum(cr[2], jnp.where(gt, NEG, blk)))
            cnt, a, b = _fold_keys(st_ref, nch, ck, (full(0.0), full(POS), full(NEG)), step,
                                   lambda x, y: (x[0] + y[0],) + pair_minmax(x[1:], y[1:]))
            up = _all_sublanes(cnt, jnp.add) >= kf
            active = lo2 < ub
            lo2n = jnp.where(active & up, _all_sublanes(a, jnp.minimum), lo2)
            ubn = jnp.where(active & jnp.logical_not(up), _all_sublanes(b, jnp.maximum), ub)
            return lo2n, ubn, _any(lo2n < ubn).astype(jnp.int32)

        v, _, _ = lax.while_loop(lambda carry: carry[2] > 0, body, (lo2, ub, _any(lo2 < ub).astype(jnp.int32)))
        thr_ref[...] = jnp.where(unres, v, lo)
        tied = unres & (_count(st_ref, nch, ck, lambda blk, _r: blk >= v) > kf)

        @pl.when(_any(tied))
        def _ties():
            _drop_excess_ties(st_ref, nch, ck, v, tied, kf)


def _drop_excess_ties(st_ref, nch, ck, v, tied, kf):
    nq = st_ref.shape[1]
    full = lambda x: jnp.full((SUBLANES, nq), x, F32)
    need = kf - _count(st_ref, nch, ck, lambda blk, _r: blk > v)
    nkeys = st_ref.shape[0]
    sub = lax.broadcasted_iota(jnp.int32, (SUBLANES, nq), 0)
    key_index = lambda row0: (sub + row0).astype(F32)

    def j_body(_, carry):
        lo_j, hi_j = carry
        mid_j = jnp.floor((lo_j + hi_j) * 0.5)
        ok = _count(st_ref, nch, ck, lambda blk, row0: (blk == v) & (key_index(row0) <= mid_j)) >= need
        return jnp.where(ok, lo_j, mid_j), jnp.where(ok, mid_j, hi_j)

    n_j = int(np.ceil(np.log2(nkeys))) + 1
    _, j_last = lax.fori_loop(0, n_j, j_body, (full(-1.0), full(float(nkeys - 1))))

    def drop_body(c, _):
        base = pl.multiple_of(c * ck, ck)
        for j in range(ck // SUBLANES):
            rows = pl.ds(base + j * SUBLANES, SUBLANES)
            blk = st_ref[rows, :]
            drop = tied & (blk == v) & (key_index(base + j * SUBLANES) > j_last)
            st_ref[rows, :] = jnp.where(drop, NEG, blk)
        return 0
    lax.fori_loop(0, nch, drop_body, 0)


def _rows_min(x):
    return jnp.min(x.reshape(x.shape[0] // SUBLANES, SUBLANES, x.shape[1]), axis=0)


def _rows_max(x):
    return jnp.max(x.reshape(x.shape[0] // SUBLANES, SUBLANES, x.shape[1]), axis=0)


def _rows_sum(x):
    return jnp.sum(x.reshape(x.shape[0] // SUBLANES, SUBLANES, x.shape[1]), axis=0)


def _dsa_prompt_kernel(kip_ref, qit_ref, wit_ref, kp_ref, qt_ref, vtt_ref, o_ref, st_ref, thr_ref,
                       m_ref, l_ref, acc_ref, lg_ref, pr_ref, *, tq, topk, n_fast, rs):
    ck = tq
    i = pl.program_id(1)
    nch = i + 1
    n_pairs = qt_ref.shape[0] // LANES
    pairs_per_kv = n_pairs // KV_HEADS
    bcast = lambda row, n: jnp.broadcast_to(row, (n, tq))

    key_l = lax.broadcasted_iota(jnp.int32, (rs, tq), 0)
    qry_l = lax.broadcasted_iota(jnp.int32, (rs, tq), 1)
    w_rows = [bcast(wit_ref[h:h + 1, :], rs) for h in range(IDX_HEADS)]

    def score_body(c, carry):
        mn, mx = carry
        base = pl.multiple_of(c * ck, ck)
        off_diag = c < i
        for r in range(ck // rs):
            rows = pl.ds(base + r * rs, rs)
            k_lo = kip_ref[rows, 0:LANES]
            k_hi = kip_ref[rows, LANES:2 * LANES]
            acc = jnp.zeros((rs, tq), F32)
            for p in range(IDX_HEADS // 2):
                rhs = qit_ref[p * LANES:(p + 1) * LANES, :]
                acc = acc + jnp.maximum(jnp.dot(k_lo, rhs, preferred_element_type=F32), 0.0) * w_rows[2 * p]
                acc = acc + jnp.maximum(jnp.dot(k_hi, rhs, preferred_element_type=F32), 0.0) * w_rows[2 * p + 1]
            valid = off_diag | (key_l + r * rs <= qry_l)
            st_ref[rows, :] = jnp.where(valid, acc, NEG)
            mn = jnp.minimum(mn, _rows_min(jnp.where(valid, acc, POS)))
            mx = jnp.maximum(mx, _rows_max(jnp.where(valid, acc, NEG)))
        return mn, mx

    mn, mx = lax.fori_loop(0, nch, score_body,
                           (jnp.full((SUBLANES, tq), POS, F32), jnp.full((SUBLANES, tq), NEG, F32)))
    _select_threshold(st_ref, thr_ref, nch, ck, _all_sublanes(mn, jnp.minimum), _all_sublanes(mx, jnp.maximum),
                      topk, n_fast)

    thr = bcast(thr_ref[0:1, :], ck)
    m_ref[...] = jnp.full(m_ref.shape, SOFTMAX_NEG, F32)
    l_ref[...] = jnp.zeros(l_ref.shape, F32)
    acc_ref[...] = jnp.zeros(acc_ref.shape, F32)

    def att_body(c, _):
        rows = pl.ds(pl.multiple_of(c * ck, ck), ck)
        sel = st_ref[rows, :] >= thr
        slab_of = lambda h: 2 * (h // 2 // pairs_per_kv) + h % 2
        n_heads = 2 * n_pairs
        for h in range(n_heads):
            slab = slab_of(h)
            kx = kp_ref[rows, slab * LANES:(slab + 1) * LANES]
            qt_pair = qt_ref[(h // 2) * LANES:(h // 2 + 1) * LANES, :]
            lg_ref[h] = jnp.where(sel, jnp.dot(kx, qt_pair, preferred_element_type=F32), SOFTMAX_NEG)
        alpha = []
        for h in range(n_heads):
            lg = lg_ref[h]
            m_old = m_ref[h]
            m_new = jnp.maximum(m_old, _all_sublanes(_rows_max(lg), jnp.maximum))
            alpha.append(jnp.exp2(m_old - m_new))
            m_ref[h] = m_new
            pr_ref[h] = jnp.exp2(lg - bcast(m_new[0:1, :], ck)).astype(BF16)
        ones = jnp.ones((ONES_ROWS, ck), BF16)
        for p in range(n_pairs):
            pv = []
            for h in (2 * p, 2 * p + 1):
                vt1 = jnp.concatenate([vtt_ref[slab_of(h) * LANES:(slab_of(h) + 1) * LANES, rows], ones], axis=0)
                pv1 = jnp.dot(vt1, pr_ref[h], preferred_element_type=F32)
                l_ref[h] = alpha[h] * l_ref[h] + pv1[LANES:LANES + SUBLANES, :]
                pv.append(pv1[:LANES, :])
            scale = jnp.concatenate([bcast(alpha[2 * p][0:1, :], HEAD_DIM), bcast(alpha[2 * p + 1][0:1, :], HEAD_DIM)],
                                    axis=0)
            acc_ref[p] = acc_ref[p] * scale + pv[0] + pv[1]
        return 0

    lax.fori_loop(0, nch, att_body, 0)
    for p in range(n_pairs):
        denom = jnp.concatenate([bcast(l_ref[2 * p, 0:1, :], HEAD_DIM), bcast(l_ref[2 * p + 1, 0:1, :], HEAD_DIM)],
                                axis=0)
        o_ref[:, p * LANES:(p + 1) * LANES] = (acc_ref[p] / denom).T.astype(o_ref.dtype)


def _dsa_prompt(kip, qit, wit, kp, qt, vtt, *, nb, seq, tq, topk, n_fast):
    n_q = qt.shape[1]
    spb = seq // tq
    per_b_rows = lambda a: pl.BlockSpec((seq, a.shape[1]), lambda b, i: (b, 0))
    q_cols = lambda a: pl.BlockSpec((None, a.shape[1], tq), lambda b, i: (b, 0, i))
    return pl.pallas_call(
        functools.partial(_dsa_prompt_kernel, tq=tq, topk=topk, n_fast=n_fast, rs=min(tq, 128)),
        out_shape=jax.ShapeDtypeStruct((nb * seq, n_q), BF16),
        grid=(nb, spb),
        in_specs=[per_b_rows(kip), q_cols(qit), q_cols(wit), per_b_rows(kp), q_cols(qt),
                  pl.BlockSpec((None, vtt.shape[1], seq), lambda b, i: (b, 0, 0))],
        out_specs=pl.BlockSpec((tq, n_q), lambda b, i: (b * spb + i, 0)),
        scratch_shapes=[pltpu.VMEM((seq, tq), F32), pltpu.VMEM((SUBLANES, tq), F32),
                        pltpu.VMEM((n_q // HEAD_DIM, SUBLANES, tq), F32),
                        pltpu.VMEM((n_q // HEAD_DIM, SUBLANES, tq), F32),
                        pltpu.VMEM((n_q // LANES, LANES, tq), F32),
                        pltpu.VMEM((n_q // HEAD_DIM, tq, tq), F32),
                        pltpu.VMEM((n_q // HEAD_DIM, tq, tq), BF16)],
        compiler_params=_cparams("parallel", "arbitrary"),
        name="dsa_prompt",
    )(kip, qit, wit, kp, qt, vtt)


def _retention_tables(chunk, n_heads):
    log_g = np.log1p(-np.exp2(-5.0 - np.arange(n_heads, dtype=np.float64)))
    i = np.arange(chunk, dtype=np.float64)
    diff = i[:, None] - i[None, :]
    decay = np.where(diff >= 0, np.exp(log_g[:, None, None] * np.maximum(diff, 0.0)), 0.0)
    q_dec = np.exp(log_g[:, None] * (i[None, :] + 1.0))
    k_dec = np.exp(log_g[:, None] * (chunk - 1.0 - i)[None, :])
    c_dec = np.exp(log_g * chunk)
    return decay, q_dec, k_dec, c_dec


def _group_norm_gate(o, rg, gain):
    mu = jnp.mean(o, axis=-1, keepdims=True)
    var = jnp.mean(jnp.square(o - mu), axis=-1, keepdims=True)
    on = ((o - mu) * lax.rsqrt(var + GN_EPS)) * gain
    return jax.nn.silu(rg) * on


def _retention_prompt_kernel(rq_ref, rkt_ref, rv_ref, rg_ref, gain_ref, dmat_ref, qdec_ref, kdec_ref, cdec_ref,
                             o_ref, st_ref, state_ref, *, chunk, n_heads):
    j = pl.program_id(1)

    @pl.when(j == 0)
    def _():
        state_ref[...] = jnp.zeros_like(state_ref)

    n_cc = rq_ref.shape[0] // chunk
    dot = functools.partial(jnp.dot, preferred_element_type=F32)
    tiles = [(cc, h, slice(cc * chunk, (cc + 1) * chunk), slice(h * RET_DV, (h + 1) * RET_DV))
             for cc in range(n_cc) for h in range(n_heads)]
    inner, update = {}, {}
    for cc, h, rows, cols in tiles:
        kt = rkt_ref[cols, rows]
        inner[cc, h] = (dot(rq_ref[rows, cols], kt) * dmat_ref[h]).astype(BF16)
        update[cc, h] = dot((kt.astype(F32) * kdec_ref[h]).astype(BF16), rv_ref[rows, cols])
    cross = {}
    for h in range(n_heads):
        state = state_ref[h]
        for cc in range(n_cc):
            rows, cols = slice(cc * chunk, (cc + 1) * chunk), slice(h * RET_DV, (h + 1) * RET_DV)
            cross[cc, h] = dot(rq_ref[rows, cols], state.astype(BF16)) * qdec_ref[h]
            state = cdec_ref[h] * state + update[cc, h]
        state_ref[h] = state
    for cc, h, rows, cols in tiles:
        o = dot(inner[cc, h], rv_ref[rows, cols]) + cross[cc, h]
        o_ref[rows, cols] = _group_norm_gate(o, rg_ref[rows, cols], gain_ref[:, cols]).astype(o_ref.dtype)

    @pl.when(j == pl.num_programs(1) - 1)
    def _():
        st_ref[...] = state_ref[...]


def _retention_prompt(rq, rkt, rv, rg, g_retn, *, nb, seq, tr):
    n_ret = rq.shape[1]
    n_heads = n_ret // RET_DV
    chunk = RET_CHUNK
    decay, q_dec, k_dec, c_dec = _retention_tables(chunk, n_heads)
    decay = jnp.asarray(decay, F32)
    qdec_b = jnp.asarray(np.broadcast_to(q_dec[:, :, None], (n_heads, chunk, LANES)), F32)
    kdec_b = jnp.asarray(k_dec[:, None, :], F32)
    cdec_b = jnp.asarray(np.broadcast_to(c_dec[:, None, None], (n_heads, 1, LANES)), F32)
    spb = seq // tr
    row = lambda: pl.BlockSpec((tr, n_ret), lambda b, j: (b * spb + j, 0))
    return pl.pallas_call(
        functools.partial(_retention_prompt_kernel, chunk=chunk, n_heads=n_heads),
        out_shape=(jax.ShapeDtypeStruct((nb * seq, n_ret), BF16),
                   jax.ShapeDtypeStruct((nb, n_heads, RET_DK, RET_DV), F32)),
        grid=(nb, spb),
        in_specs=[row(), pl.BlockSpec((None, n_ret, tr), lambda b, j: (b, 0, j)), row(), row(),
                  _const_spec((1, n_ret)), _const_spec((n_heads, chunk, chunk)),
                  _const_spec((n_heads, chunk, LANES)), _const_spec((n_heads, 1, chunk)),
                  _const_spec((n_heads, 1, LANES))],
        out_specs=(row(), pl.BlockSpec((None, n_heads, RET_DK, RET_DV), lambda b, j: (b, 0, 0, 0))),
        scratch_shapes=[pltpu.VMEM((n_heads, RET_DK, RET_DV), F32)],
        compiler_params=_cparams("parallel", "arbitrary"),
        name="retention_prompt",
    )(rq, rkt, rv, rg, g_retn.reshape(1, n_ret), decay, qdec_b, kdec_b, cdec_b)


def _rms(x, g):
    return (x * lax.rsqrt(jnp.mean(x * x, axis=-1, keepdims=True) + EPS)) * g


def _tail_kernel(h_ref, attn_ref, ret_ref, p_ref, wo_ref, g_ffn_ref, w_ffn_in_ref, w_down_ref,
                 g_ple_ref, w_pg_ref, w_pp_ref, g_fin_ref, y_ref):
    dot = functools.partial(jnp.dot, preferred_element_type=F32)
    n_attn = attn_ref.shape[1]
    d_ff = w_down_ref.shape[0]
    h = h_ref[...] + dot(attn_ref[...], wo_ref[:n_attn, :]) + dot(ret_ref[...], wo_ref[n_attn:, :])
    f = _rms(h, g_ffn_ref[...]).astype(BF16)
    act = (jax.nn.silu(dot(f, w_ffn_in_ref[:, :d_ff])) * dot(f, w_ffn_in_ref[:, d_ff:])).astype(BF16)
    h = h + dot(act, w_down_ref[...])
    gate = jax.nn.sigmoid(dot(_rms(h, g_ple_ref[...]).astype(BF16), w_pg_ref[...]))
    h = h + gate * dot(p_ref[...].astype(BF16), w_pp_ref[...])
    y_ref[...] = _rms(h, g_fin_ref[...])


def _tail(h2d, attn, ret, p2d, w_out, g_ffn, w_ffn_in, w_ffn_out, g_ple, w_ple_gate, w_ple_proj, g_final, *, tm):
    rows, d = h2d.shape
    n_attn = attn.shape[1]
    assert w_ffn_in.shape[1] == 2 * w_ffn_out.shape[0] and w_ffn_out.shape[0] % LANES == 0
    ws = [w_out.astype(BF16), g_ffn.reshape(1, d), w_ffn_in.astype(BF16), w_ffn_out.astype(BF16),
          g_ple.reshape(1, d), w_ple_gate.astype(BF16), w_ple_proj.astype(BF16), g_final.reshape(1, d)]
    row = lambda w: pl.BlockSpec((tm, w), lambda i: (i, 0))
    return pl.pallas_call(
        _tail_kernel,
        out_shape=jax.ShapeDtypeStruct((rows, d), F32),
        grid=(rows // tm,),
        in_specs=[row(d), row(n_attn), row(ret.shape[1]), row(p2d.shape[1])] + [_const_spec(w.shape) for w in ws],
        out_specs=row(d),
        compiler_params=_cparams("parallel"),
        name="tail",
    )(h2d, attn, ret, p2d, *ws)


def _page_copies(pools, bufs, sems, pt_ref, b, c, slot, pages_per_chunk):
    copies = []
    for pool, buf in zip(pools, bufs):
        for p in range(pages_per_chunk):
            page = pt_ref[b, c * pages_per_chunk + p]
            copies.append(pltpu.make_async_copy(pool.at[page], buf.at[slot, p], sems.at[slot]))
    return copies


def _chunk_keys_on_lanes(pages):
    return jnp.concatenate([pages[p] for p in range(pages.shape[0])], axis=1).astype(BF16)


def _stream_chunks(pools, bufs, sems, pt_ref, n_chunks, pages_per_chunk, compute):
    b = pl.program_id(0)
    nb = pl.num_programs(0)

    def start(bb, cc, slot):
        for cp in _page_copies(pools, bufs, sems, pt_ref, bb, cc, slot, pages_per_chunk):
            cp.start()

    @pl.when(b == 0)
    def _():
        start(0, 0, 0)

    assert n_chunks % 2 == 0
    for c in range(n_chunks):
        slot = c % 2
        if c + 1 < n_chunks:
            start(b, c + 1, 1 - slot)
        else:
            @pl.when(b + 1 < nb)
            def _():
                start(b + 1, 0, 1 - slot)
        for cp in _page_copies(pools, bufs, sems, pt_ref, b, c, slot, pages_per_chunk):
            cp.wait()
        compute(c, slot)


def _tile_lanes(x, width):
    return x if width == LANES else jnp.concatenate([x] * (width // LANES), axis=1)


def _nt_dot(a, b):
    return lax.dot_general(a, b, (((1,), (1,)), ((), ())), preferred_element_type=F32)


def _head_sum(x, n_t):
    return jnp.sum(x.reshape(n_t, IDX_HEADS, x.shape[-1]), axis=1)


def _sample_scores_kernel(pt_ref, qi_ref, wi_ref, kinew_ref, pool_ref, s_ref, buf, sems,
                          *, n_t, n_chunks, pages_per_chunk):
    ck = pages_per_chunk * PAGE_SIZE
    qi = qi_ref[...]
    w = wi_ref[...]

    def compute(c, slot):
        a = jnp.dot(qi, _chunk_keys_on_lanes(buf[slot]), preferred_element_type=F32)
        s_ref[:, c * ck:(c + 1) * ck] = _head_sum(jnp.maximum(a, 0.0) * w, n_t)

    _stream_chunks([pool_ref], [buf], sems, pt_ref, n_chunks, pages_per_chunk, compute)

    a = jnp.dot(qi, kinew_ref[...], preferred_element_type=F32)
    s_new = _head_sum(jnp.maximum(a, 0.0) * w, n_t)
    t_q = lax.broadcasted_iota(jnp.int32, s_new.shape, 0)
    t_k = lax.broadcasted_iota(jnp.int32, s_new.shape, 1)
    s_ref[:, n_chunks * ck:] = jnp.where(t_k <= t_q, s_new, NEG)


def _sample_scores(page_table, qi_s, wi_s, kinew, pool_ik, *, n_t, pages_per_chunk):
    db, n_pages = page_table.shape
    n_chunks = n_pages // pages_per_chunk
    past = n_pages * PAGE_SIZE
    rows = n_t * IDX_HEADS
    grid_spec = pltpu.PrefetchScalarGridSpec(
        num_scalar_prefetch=1,
        grid=(db,),
        in_specs=[pl.BlockSpec((None, rows, IDX_DIM), lambda b, pt: (b, 0, 0)),
                  pl.BlockSpec((None, rows, 1), lambda b, pt: (b, 0, 0)),
                  pl.BlockSpec((None, IDX_DIM, PAGE_SIZE), lambda b, pt: (b, 0, 0)),
                  pl.BlockSpec(memory_space=pl.ANY)],
        out_specs=pl.BlockSpec((None, n_t, past + PAGE_SIZE), lambda b, pt: (b, 0, 0)),
        scratch_shapes=[pltpu.VMEM((2, pages_per_chunk, IDX_DIM, PAGE_SIZE), F32),
                        pltpu.SemaphoreType.DMA((2,))],
    )
    return pl.pallas_call(
        functools.partial(_sample_scores_kernel, n_t=n_t, n_chunks=n_chunks, pages_per_chunk=pages_per_chunk),
        out_shape=jax.ShapeDtypeStruct((db, n_t, past + PAGE_SIZE), F32),
        grid_spec=grid_spec,
        compiler_params=_cparams("arbitrary"),
        name="sample_scores",
    )(page_table, qi_s, wi_s, kinew, pool_ik)


def _sample_select_kernel(s_in_ref, s_out_ref, thr_ref, st_ref, thr8_ref, *, ck, topk, n_fast):
    nq, nkeys = s_in_ref.shape
    nch = nkeys // ck + 0 * pl.program_id(0)
    chunk = lambda c: pl.ds(pl.multiple_of(c * ck, ck), ck)

    def to_lanes(c, _):
        st_ref[chunk(c), :] = s_in_ref[:, chunk(c)].T
        return 0
    lax.fori_loop(0, nch, to_lanes, 0)

    mn, mx = _fold_keys(
        st_ref, nch, ck, (jnp.full((SUBLANES, nq), POS, F32), jnp.full((SUBLANES, nq), NEG, F32)),
        lambda cr, blk, _r: (jnp.minimum(cr[0], jnp.where(blk > 0.5 * NEG, blk, POS)), jnp.maximum(cr[1], blk)),
        lambda x, y: (jnp.minimum(x[0], y[0]), jnp.maximum(x[1], y[1])))
    _select_threshold(st_ref, thr8_ref, nch, ck, _all_sublanes(mn, jnp.minimum), _all_sublanes(mx, jnp.maximum),
                      topk, n_fast)
    def to_rows(c, _):
        s_out_ref[:, chunk(c)] = st_ref[chunk(c), :].T
        return 0
    lax.fori_loop(0, nch, to_rows, 0)
    thr_ref[...] = jnp.broadcast_to(thr8_ref[0:1, :], (LANES, nq)).T


def _sample_select(s2d, *, topk, n_fast):
    rows, ncols = s2d.shape
    ck = max(w for w in (LANES, 2 * LANES, 3 * LANES, 4 * LANES) if ncols % w == 0)
    full = pl.BlockSpec((rows, ncols), lambda i: (0, 0), pipeline_mode=pl.Buffered(1))
    return pl.pallas_call(
        functools.partial(_sample_select_kernel, ck=ck, topk=topk, n_fast=n_fast),
        out_shape=(jax.ShapeDtypeStruct((rows, ncols), F32), jax.ShapeDtypeStruct((rows, LANES), F32)),
        grid=(1,),
        in_specs=[full],
        out_specs=(full, pl.BlockSpec((rows, LANES), lambda i: (0, 0))),
        scratch_shapes=[pltpu.VMEM((ncols, rows), F32), pltpu.VMEM((SUBLANES, rows), F32)],
        compiler_params=_cparams("arbitrary"),
        name="sample_select",
    )(s2d)


def _sample_attend_kernel(pt_ref, q_ref, s_ref, thr_ref, knew_ref, vnew_ref, poolk_ref, poolv_ref, o_ref,
                          kbuf, vbuf, sems, *, n_t, n_chunks, pages_per_chunk):
    ck = pages_per_chunk * PAGE_SIZE
    rows = q_ref.shape[1]
    reps = rows // n_t
    thr = jnp.concatenate([thr_ref[...]] * reps, axis=0)
    state = [[jnp.full((rows, 1), SOFTMAX_NEG, F32), jnp.zeros((rows, 1), F32), jnp.zeros((rows, HEAD_DIM), F32)]
             for _ in range(KV_HEADS)]

    def attend(kt, vt, s_blk):
        width = s_blk.shape[1]
        sel = jnp.concatenate([s_blk] * reps, axis=0) >= _tile_lanes(thr, width)
        lgs = [jnp.where(sel, jnp.dot(q_ref[g], kt[g], preferred_element_type=F32), SOFTMAX_NEG)
               for g in range(KV_HEADS)]
        stats = []
        for g in range(KV_HEADS):
            m, l, _ = state[g]
            m_new = jnp.maximum(m, jnp.max(lgs[g], axis=1, keepdims=True))
            alpha = jnp.exp2(m - m_new)
            pr = jnp.exp2(lgs[g] - m_new)
            stats.append((m_new, alpha, alpha * l + jnp.sum(pr, axis=1, keepdims=True), pr.astype(BF16)))
        for g in range(KV_HEADS):
            m_new, alpha, l_new, pr = stats[g]
            state[g] = [m_new, l_new, alpha * state[g][2] + _nt_dot(pr, vt[g])]

    def compute(c, slot):
        attend([_chunk_keys_on_lanes(kbuf[slot, :, g]) for g in range(KV_HEADS)],
               [_chunk_keys_on_lanes(vbuf[slot, :, g]) for g in range(KV_HEADS)], s_ref[:, c * ck:(c + 1) * ck])

    _stream_chunks([poolk_ref, poolv_ref], [kbuf, vbuf], sems, pt_ref, n_chunks, pages_per_chunk, compute)
    attend(knew_ref[...], vnew_ref[...], s_ref[:, n_chunks * ck:])
    for g in range(KV_HEADS):
        o_ref[g] = state[g][2] / state[g][1]


def _sample_attend(page_table, q_bd, s3d, thr3d, knew, vnew, pool_k, pool_v, *, n_t, pages_per_chunk):
    db, n_pages = page_table.shape
    n_chunks = n_pages // pages_per_chunk
    rows = q_bd.shape[2]
    ncols = s3d.shape[2]
    per_b = lambda r, w: pl.BlockSpec((None, r, w), lambda b, pt: (b, 0, 0))
    per_b4 = lambda r, w: pl.BlockSpec((None, KV_HEADS, r, w), lambda b, pt: (b, 0, 0, 0))
    grid_spec = pltpu.PrefetchScalarGridSpec(
        num_scalar_prefetch=1,
        grid=(db,),
        in_specs=[per_b4(rows, HEAD_DIM), per_b(n_t, ncols), per_b(n_t, LANES), per_b4(HEAD_DIM, PAGE_SIZE),
                  per_b4(HEAD_DIM, PAGE_SIZE), pl.BlockSpec(memory_space=pl.ANY), pl.BlockSpec(memory_space=pl.ANY)],
        out_specs=per_b4(rows, HEAD_DIM),
        scratch_shapes=[pltpu.VMEM((2, pages_per_chunk, KV_HEADS, HEAD_DIM, PAGE_SIZE), F32),
                        pltpu.VMEM((2, pages_per_chunk, KV_HEADS, HEAD_DIM, PAGE_SIZE), F32),
                        pltpu.SemaphoreType.DMA((2,))],
    )
    return pl.pallas_call(
        functools.partial(_sample_attend_kernel, n_t=n_t, n_chunks=n_chunks, pages_per_chunk=pages_per_chunk),
        out_shape=jax.ShapeDtypeStruct((db, KV_HEADS, rows, HEAD_DIM), F32),
        grid_spec=grid_spec,
        compiler_params=_cparams("arbitrary"),
        name="sample_attend",
    )(page_table, q_bd, s3d, thr3d, knew, vnew, pool_k, pool_v)


def _retention_sample_kernel(rq_ref, rkt_ref, rv_ref, rg_ref, gain_ref, dfull_ref, qdec_ref, kdec_ref, cdec_ref,
                             st_in_ref, o_ref, st_out_ref, *, n_t, n_heads):
    b = pl.program_id(0)

    @pl.when(b == 0)
    def _():
        o_ref[...] = jnp.zeros_like(o_ref)

    n_rows = rq_ref.shape[0]
    row = lax.broadcasted_iota(jnp.int32, (n_rows, RET_DV), 0)
    mine = (row >= b * n_t) & (row < (b + 1) * n_t)
    for h in range(n_heads):
        cols = slice(h * RET_DV, (h + 1) * RET_DV)
        q = jnp.where(mine, rq_ref[:, cols], 0.0).astype(BF16)
        v = rv_ref[:, cols]
        kt = rkt_ref[cols, :]
        state = st_in_ref[h]
        inner = jnp.dot(q, kt, preferred_element_type=F32) * dfull_ref[h]
        o = jnp.dot(inner.astype(BF16), v, preferred_element_type=F32)
        o = o + jnp.dot(q, state.astype(BF16), preferred_element_type=F32) * qdec_ref[h]
        vd = jnp.where(mine, v.astype(F32) * kdec_ref[h], 0.0).astype(BF16)
        st_out_ref[h] = cdec_ref[h] * state + jnp.dot(kt, vd, preferred_element_type=F32)
        gated = _group_norm_gate(o, rg_ref[:, cols], gain_ref[:, cols])
        o_ref[:, cols] = jnp.where(mine, gated, o_ref[:, cols].astype(F32)).astype(o_ref.dtype)


def _retention_sample(rq, rkt, rv, rg, g_retn, state, *, db, n_t):
    n_rows, n_ret = rq.shape
    n_heads = n_ret // RET_DV
    decay, q_dec, k_dec, c_dec = _retention_tables(n_t, n_heads)
    same = np.kron(np.eye(db), np.ones((n_t, n_t)))
    dfull = jnp.asarray(same[None] * np.tile(decay, (1, db, db)), F32)
    qdec_b = jnp.asarray(np.broadcast_to(np.tile(q_dec, (1, db))[:, :, None], (n_heads, n_rows, RET_DV)), F32)
    kdec_b = jnp.asarray(np.broadcast_to(np.tile(k_dec, (1, db))[:, :, None], (n_heads, n_rows, RET_DV)), F32)
    cdec_b = jnp.asarray(np.broadcast_to(c_dec[:, None, None], (n_heads, 1, RET_DV)), F32)
    const = lambda shape: pl.BlockSpec(shape, lambda b: (0,) * len(shape))
    st_spec = pl.BlockSpec((None, n_heads, RET_DK, RET_DV), lambda b: (b, 0, 0, 0))
    return pl.pallas_call(
        functools.partial(_retention_sample_kernel, n_t=n_t, n_heads=n_heads),
        out_shape=(jax.ShapeDtypeStruct((n_rows, n_ret), BF16),
                   jax.ShapeDtypeStruct((db, n_heads, RET_DK, RET_DV), F32)),
        grid=(db,),
        in_specs=[const((n_rows, n_ret)), const((n_ret, n_rows)), const((n_rows, n_ret)), const((n_rows, n_ret)),
                  const((1, n_ret)), const((n_heads, n_rows, n_rows)), const((n_heads, n_rows, RET_DV)),
                  const((n_heads, n_rows, RET_DV)), const((n_heads, 1, RET_DV)), st_spec],
        out_specs=(const((n_rows, n_ret)), st_spec),
        compiler_params=_cparams("arbitrary"),
        name="retention_sample",
    )(rq, rkt, rv, rg, g_retn.reshape(1, n_ret), dfull, qdec_b, kdec_b, cdec_b, state)


def _mixer_dims(w_in, d_model):
    n_q = d_model // 2
    n_ret = d_model // 2
    n_kv = KV_HEADS * HEAD_DIM
    n_qi = IDX_HEADS * IDX_DIM
    assert n_kv == LANES and 2 * n_kv + n_q + n_qi + IDX_DIM + IDX_HEADS + 4 * n_ret == w_in.shape[1]
    return n_q, n_kv, n_qi, n_ret


def _prompt_layer(x, p, weights, *, n_fast=24):
    g_mix, w_in, g_retn, w_out, g_ffn, w_ffn_in, w_ffn_out, g_ple, w_ple_gate, w_ple_proj, g_final = weights
    nb, seq, d = x.shape
    dims = _mixer_dims(w_in, d)
    w_packed = _pack_w_in(w_in, *dims)
    x2d = x.reshape(nb * seq, d)
    tm = min(512, seq)
    (qt, kt, vt, qit, kit, wit, rq, rkt, rv, rg, kp, vtt, kip) = _inproj(
        x2d, np.arange(seq), g_mix, w_packed, seq=seq, tm=tm, dims=dims)
    topk = min(TOPK_MAX, seq // 4)
    attn = _dsa_prompt(kip, qit, wit, kp, qt, vtt, nb=nb, seq=seq, tq=min(256, seq), topk=topk, n_fast=n_fast)
    ret, st = _retention_prompt(rq, rkt, rv, rg, g_retn, nb=nb, seq=seq, tr=min(512, seq))
    y = _tail(x2d, attn, ret, p.reshape(nb * seq, -1), w_out, g_ffn, w_ffn_in, w_ffn_out,
              g_ple, w_ple_gate, w_ple_proj, g_final, tm=min(256, nb * seq))
    heads_last = lambda a: a.reshape(nb, KV_HEADS, HEAD_DIM, seq).transpose(0, 3, 1, 2)
    return y.reshape(nb, seq, d), heads_last(kt), heads_last(vt), kit.transpose(0, 2, 1), st


def _sample_layer(x, p, pool_k, pool_v, pool_ik, state, page_table, weights, *, n_fast=24):
    g_mix, w_in, g_retn, w_out, g_ffn, w_ffn_in, w_ffn_out, g_ple, w_ple_gate, w_ple_proj, g_final = weights
    db, n_t, d = x.shape
    n_rows = db * n_t
    n_pages = page_table.shape[1]
    past = n_pages * PAGE_SIZE
    dims = _mixer_dims(w_in, d)
    n_q = dims[0]
    w_packed = _pack_w_in(w_in, *dims)
    x2d = x.reshape(n_rows, d)
    pos = np.tile(past + np.arange(n_t), db)
    (qt, kt, vt, qit, kit, wit, rq, rkt, rv, rg, _, _, _) = _inproj(
        x2d, pos, g_mix, w_packed, seq=n_rows, tm=n_rows, dims=dims)
    q, qi, wi = qt[0].T, qit[0].T, wit[0].T

    def new_pages(at, heads):
        a = at[0].reshape(heads, -1, db, n_t).transpose(2, 0, 1, 3)
        return jnp.pad(a, ((0, 0), (0, 0), (0, 0), (0, PAGE_SIZE - n_t))).astype(BF16)
    group = n_q // HEAD_DIM // KV_HEADS
    qg = q.reshape(db, n_t, KV_HEADS, group, HEAD_DIM).transpose(0, 2, 3, 1, 4)
    qg = qg.reshape(db, KV_HEADS, group * n_t, HEAD_DIM)
    pages_per_chunk = max(1, min(32, n_pages // 2))

    s3d = _sample_scores(page_table, qi.reshape(db, n_t * IDX_HEADS, IDX_DIM), wi.reshape(db, n_t * IDX_HEADS, 1),
                         new_pages(kit, 1)[:, 0], pool_ik.transpose(0, 2, 1), n_t=n_t, pages_per_chunk=pages_per_chunk)
    topk = min(TOPK_MAX, (past + n_t) // 4)
    s_sel, thr = _sample_select(s3d.reshape(n_rows, -1), topk=topk, n_fast=n_fast)
    o = _sample_attend(page_table, qg, s_sel.reshape(db, n_t, -1), thr.reshape(db, n_t, LANES),
                       new_pages(kt, KV_HEADS), new_pages(vt, KV_HEADS), pool_k.transpose(0, 2, 3, 1),
                       pool_v.transpose(0, 2, 3, 1), n_t=n_t, pages_per_chunk=pages_per_chunk)
    attn = o.reshape(db, KV_HEADS, group, n_t, HEAD_DIM).transpose(0, 3, 1, 2, 4).reshape(n_rows, n_q).astype(BF16)

    ret, st = _retention_sample(rq, rkt[0], rv, rg, g_retn, state, db=db, n_t=n_t)
    y = _tail(x2d, attn, ret, p.reshape(n_rows, -1), w_out, g_ffn, w_ffn_in, w_ffn_out,
              g_ple, w_ple_gate, w_ple_proj, g_final, tm=n_rows)
    heads_last = lambda a: a[0].reshape(KV_HEADS, HEAD_DIM, db, n_t).transpose(2, 3, 0, 1)
    return y.reshape(db, n_t, d), heads_last(kt), heads_last(vt), kit[0].T.reshape(db, n_t, IDX_DIM), st


def kernel(x_prompt, x_sample, cache_k, cache_v, cache_idx_k, state_retn, page_table, p_prompt, p_sample, g_mix, w_in,
           g_retn, w_out, g_ffn, w_ffn_in, w_ffn_out, g_ple, w_ple_gate, w_ple_proj, g_final):
    depth = w_in.shape[0]
    assert depth == 1, "the final RMSNorm is fused into the single layer's tail kernel"
    weights = (g_mix[0], w_in[0], g_retn[0], w_out[0], g_ffn[0], w_ffn_in[0], w_ffn_out[0], g_ple[0], w_ple_gate[0],
               w_ple_proj[0], g_final)
    y_p, k_p, v_p, ik_p, st_p = _prompt_layer(x_prompt, p_prompt[0], weights)
    y_s, k_s, v_s, ik_s, st_s = _sample_layer(x_sample, p_sample[0], cache_k[0], cache_v[0], cache_idx_k[0],
                                              state_retn[0], page_table, weights)
    return (y_p, y_s, k_p[None], v_p[None], ik_p[None], st_p[None], k_s[None], v_s[None], ik_s[None], st_s[None])
```

```python
import functools

import jax
import jax.numpy as jnp
import numpy as np
from jax import lax
from jax.experimental import pallas as pl
from jax.experimental.pallas import tpu as pltpu

F32 = jnp.float32
BF16 = jnp.bfloat16

HEAD_DIM = 64
KV_HEADS = 2
IDX_HEADS = 8
IDX_DIM = 64
TOPK_MAX = 256
RET_DK = 128
RET_DV = 128
RET_CHUNK = 128
PAGE_SIZE = 128
ROPE_THETA = 10000.0
EPS = 1e-6
GN_EPS = 1e-5

LANES = 128
SUBLANES = 8
VMEM_LIMIT_BYTES = 56 * 1024 * 1024

Q_SCALE = HEAD_DIM ** -0.5 * 1.4426950408889634
ONES_ROWS = 16

NEG = -3.0e38
POS = 3.0e38
SOFTMAX_NEG = -1.0e30


def _cparams(*sem):
    return pltpu.CompilerParams(dimension_semantics=sem, vmem_limit_bytes=VMEM_LIMIT_BYTES)


def _const_spec(shape):
    zeros = (0,) * len(shape)
    return pl.BlockSpec(shape, lambda *_: zeros, pipeline_mode=pl.Buffered(1))


def _rope_tables(pos, dim):
    half = dim // 2
    inv = ROPE_THETA ** (-np.arange(half, dtype=np.float64) / half)
    ang = np.asarray(pos, np.float64)[:, None] * inv[None, :]
    cos, sin = np.cos(ang), np.sin(ang)
    reps = LANES // dim
    cos_t = np.tile(np.concatenate([cos, cos], axis=1), (1, reps))
    sin_t = np.tile(np.concatenate([-sin, sin], axis=1), (1, reps))
    return jnp.asarray(cos_t, F32), jnp.asarray(sin_t, F32)


def _swap_halves_64(x):
    lane = lax.broadcasted_iota(jnp.int32, x.shape, 1)
    first = (lane % 64) < 32
    return jnp.where(first, pltpu.roll(x, 96, 1), pltpu.roll(x, 32, 1))


def _pad_pair_slabs(x, swapped, low):
    return (jnp.where(low, x, 0.0), jnp.where(low, 0.0, swapped), jnp.where(low, swapped, 0.0), jnp.where(low, 0.0, x))


def _inproj_kernel(x_ref, g_ref, w_ref, c64_ref, s64_ref, c128_ref, s128_ref,
                   qt_ref, kt_ref, vt_ref, qit_ref, kit_ref, wit_ref, rq_ref, rkt_ref, rv_ref, rg_ref,
                   kp_ref, vtt_ref, kip_ref, *, n_q, n_qi, n_ret):
    x = x_ref[...]
    tm = x.shape[0]
    ms = jnp.mean(x * x, axis=-1, keepdims=True)
    a = ((x * lax.rsqrt(ms + EPS)) * g_ref[...]).astype(BF16)
    c64, s64 = c64_ref[...], s64_ref[...]
    c128, s128 = c128_ref[...], s128_ref[...]
    low = lax.broadcasted_iota(jnp.int32, (tm, LANES), 1) < HEAD_DIM
    zt = jnp.zeros((HEAD_DIM, tm), BF16)

    col = [0]

    def segment(width):
        y = jnp.dot(a, w_ref[:, col[0]:col[0] + width], preferred_element_type=F32)
        col[0] += width
        return [y[:, j * LANES:(j + 1) * LANES] for j in range(width // LANES)]

    def rope64(y):
        return y * c64 + _swap_halves_64(y) * s64

    def rope128(y):
        return y * c128 + pltpu.roll(y, 64, 1) * s128

    for j, y in enumerate(segment(n_q)):
        qt_ref[j * LANES:(j + 1) * LANES, :] = (rope64(y) * Q_SCALE).T.astype(BF16)
    kk, vv = segment(2 * LANES)
    kk = rope64(kk)
    kt_ref[...] = kk.T
    for s, slab in enumerate(_pad_pair_slabs(kk, pltpu.roll(kk, HEAD_DIM, 1), low)):
        kp_ref[:, s * LANES:(s + 1) * LANES] = slab.astype(BF16)
    vvt = vv.T
    vt_ref[...] = vvt
    vvt = vvt.astype(BF16)
    for g in range(KV_HEADS):
        vtg = vvt[g * HEAD_DIM:(g + 1) * HEAD_DIM, :]
        base = g * 4 * HEAD_DIM
        vtt_ref[base:base + HEAD_DIM, :] = vtg
        vtt_ref[base + HEAD_DIM:base + 2 * HEAD_DIM, :] = zt
        vtt_ref[base + 2 * HEAD_DIM:base + 3 * HEAD_DIM, :] = zt
        vtt_ref[base + 3 * HEAD_DIM:base + 4 * HEAD_DIM, :] = vtg
    for j, y in enumerate(segment(n_qi)):
        qit_ref[j * LANES:(j + 1) * LANES, :] = (rope64(y) * (IDX_DIM ** -0.5)).T.astype(BF16)
    y, _ = segment(2 * LANES)
    ki = rope64(y)
    kit_ref[...] = ki.T[:IDX_DIM, :]
    ki_lo, ki_hi, _, _ = _pad_pair_slabs(ki, pltpu.roll(ki, IDX_DIM, 1), low)
    kip_ref[:, 0:LANES] = ki_lo.astype(BF16)
    kip_ref[:, LANES:2 * LANES] = ki_hi.astype(BF16)
    wit_ref[...] = y.T[IDX_DIM:IDX_DIM + IDX_HEADS, :] * (IDX_HEADS ** -0.5)
    for j, y in enumerate(segment(n_ret)):
        rq_ref[:, j * LANES:(j + 1) * LANES] = rope128(y).astype(BF16)
    for j, y in enumerate(segment(n_ret)):
        rkt_ref[j * LANES:(j + 1) * LANES, :] = (rope128(y) * (RET_DK ** -0.5)).T.astype(BF16)
    for j, y in enumerate(segment(n_ret)):
        rv_ref[:, j * LANES:(j + 1) * LANES] = y.astype(BF16)
    for j, y in enumerate(segment(n_ret)):
        rg_ref[:, j * LANES:(j + 1) * LANES] = y


def _pack_w_in(w_in, n_q, n_kv, n_qi, n_ret):
    d = w_in.shape[0]
    sizes = (n_q, n_kv, n_kv, n_qi, IDX_DIM, IDX_HEADS, n_ret, n_ret, n_ret, n_ret)
    offs = np.concatenate([[0], np.cumsum(sizes)])
    seg = [w_in[:, offs[i]:offs[i + 1]] for i in range(len(sizes))]
    pad = jnp.zeros((d, 2 * LANES - IDX_DIM - IDX_HEADS), w_in.dtype)
    packed = jnp.concatenate(seg[:4] + [seg[4], seg[5], pad] + seg[6:], axis=1)
    return packed.astype(BF16)


def _inproj(x2d, pos, g_mix, w_packed, *, seq, tm, dims):
    n_q, n_kv, n_qi, n_ret = dims
    rows, d = x2d.shape
    nb = rows // seq
    spt = seq // tm
    c64, s64 = _rope_tables(pos, HEAD_DIM)
    c128, s128 = _rope_tables(pos, RET_DK)
    wcols = w_packed.shape[1]

    row = lambda w: pl.BlockSpec((tm, w), lambda i: (i, 0))
    tab = pl.BlockSpec((tm, LANES), lambda i: (i % spt, 0))
    tr = lambda h: pl.BlockSpec((None, h, tm), lambda i: (i // spt, 0, i % spt))
    out_shape = (
        jax.ShapeDtypeStruct((nb, n_q, seq), BF16),
        jax.ShapeDtypeStruct((nb, n_kv, seq), F32),
        jax.ShapeDtypeStruct((nb, n_kv, seq), F32),
        jax.ShapeDtypeStruct((nb, n_qi, seq), BF16),
        jax.ShapeDtypeStruct((nb, IDX_DIM, seq), F32),
        jax.ShapeDtypeStruct((nb, IDX_HEADS, seq), F32),
        jax.ShapeDtypeStruct((rows, n_ret), BF16),
        jax.ShapeDtypeStruct((nb, n_ret, seq), BF16),
        jax.ShapeDtypeStruct((rows, n_ret), BF16),
        jax.ShapeDtypeStruct((rows, n_ret), F32),
        jax.ShapeDtypeStruct((rows, 4 * n_kv), BF16),
        jax.ShapeDtypeStruct((nb, 4 * n_kv, seq), BF16),
        jax.ShapeDtypeStruct((rows, 2 * LANES), BF16),
    )
    out_specs = (tr(n_q), tr(n_kv), tr(n_kv), tr(n_qi), tr(IDX_DIM), tr(IDX_HEADS),
                 row(n_ret), tr(n_ret), row(n_ret), row(n_ret), row(4 * n_kv), tr(4 * n_kv), row(2 * LANES))
    return pl.pallas_call(
        functools.partial(_inproj_kernel, n_q=n_q, n_qi=n_qi, n_ret=n_ret),
        out_shape=out_shape,
        grid=(rows // tm,),
        in_specs=[row(d), _const_spec((1, d)), _const_spec((d, wcols)), tab, tab, tab, tab],
        out_specs=out_specs,
        compiler_params=_cparams("parallel"),
        name="inproj",
    )(x2d, g_mix.reshape(1, d), w_packed, c64, s64, c128, s128)


N_ACC = 4


def _fold_keys(st_ref, nch, ck, init, fn, combine):
    def body(c, carries):
        carries = list(carries)
        base = pl.multiple_of(c * ck, ck)
        chunk = st_ref[pl.ds(base, ck), :]
        for j in range(ck // SUBLANES):
            blk = chunk[j * SUBLANES:(j + 1) * SUBLANES, :]
            carries[j % N_ACC] = fn(carries[j % N_ACC], blk, base + j * SUBLANES)
        return tuple(carries)
    carries = lax.fori_loop(0, nch, body, (init,) * N_ACC)
    out = carries[0]
    for other in carries[1:]:
        out = combine(out, other)
    return out


def _all_sublanes(x, op):
    for shift in (4, 2, 1):
        x = op(x, pltpu.roll(x, shift, 0))
    return x


def _any(mask):
    return jnp.max(jnp.where(mask, 1.0, 0.0)) > 0.5


def _count(st_ref, nch, ck, pred):
    nq = st_ref.shape[1]
    acc = _fold_keys(st_ref, nch, ck, jnp.zeros((SUBLANES, nq), F32),
                     lambda a, blk, row0: a + jnp.where(pred(blk, row0), 1.0, 0.0), jnp.add)
    return _all_sublanes(acc, jnp.add)


def _select_threshold(st_ref, thr_ref, nch, ck, rmin, rmax, topk, n_fast):
    nq = st_ref.shape[1]
    kf = float(topk)
    full = lambda v: jnp.full((SUBLANES, nq), v, F32)

    def fast_body(_, carry):
        lo, hi = carry
        mid = lo + (hi - lo) * 0.5
        up = _count(st_ref, nch, ck, lambda blk, _r: blk > mid) >= kf
        return jnp.where(up, mid, lo), jnp.where(up, hi, mid)

    lo, hi = lax.fori_loop(0, n_fast, fast_body, (rmin, rmax))
    unres = _count(st_ref, nch, ck, lambda blk, _r: blk >= lo) > kf
    thr_ref[...] = lo

    @pl.when(_any(unres))
    def _exact():
        pair_minmax = lambda x, y: (jnp.minimum(x[0], y[0]), jnp.maximum(x[1], y[1]))
        lo2 = lo
        ub = jnp.where(unres, hi, lo)

        def body(carry):
            lo2, ub, _ = carry
            mid = lo2 + (ub - lo2) * 0.5
            mid = jnp.where(mid >= ub, lo2, mid)

            def step(cr, blk, _r):
                gt = blk > mid
                return (cr[0] + jnp.where(gt, 1.0, 0.0), jnp.minimum(cr[1], jnp.where(gt, blk, POS)),
                        jnp.maximum(cr[2], jnp.where(gt, NEG, blk)))
            cnt, a, b = _fold_keys(st_ref, nch, ck, (full(0.0), full(POS), full(NEG)), step,
                                   lambda x, y: (x[0] + y[0],) + pair_minmax(x[1:], y[1:]))
            up = _all_sublanes(cnt, jnp.add) >= kf
            active = lo2 < ub
            lo2n = jnp.where(active & up, _all_sublanes(a, jnp.minimum), lo2)
            ubn = jnp.where(active & jnp.logical_not(up), _all_sublanes(b, jnp.maximum), ub)
            return lo2n, ubn, _any(lo2n < ubn).astype(jnp.int32)

        v, _, _ = lax.while_loop(lambda carry: carry[2] > 0, body, (lo2, ub, _any(lo2 < ub).astype(jnp.int32)))
        thr_ref[...] = jnp.where(unres, v, lo)
        tied = unres & (_count(st_ref, nch, ck, lambda blk, _r: blk >= v) > kf)

        @pl.when(_any(tied))
        def _ties():
            _drop_excess_ties(st_ref, nch, ck, v, tied, kf)


def _drop_excess_ties(st_ref, nch, ck, v, tied, kf):
    nq = st_ref.shape[1]
    full = lambda x: jnp.full((SUBLANES, nq), x, F32)
    need = kf - _count(st_ref, nch, ck, lambda blk, _r: blk > v)
    nkeys = st_ref.shape[0]
    sub = lax.broadcasted_iota(jnp.int32, (SUBLANES, nq), 0)
    key_index = lambda row0: (sub + row0).astype(F32)

    def j_body(_, carry):
        lo_j, hi_j = carry
        mid_j = jnp.floor((lo_j + hi_j) * 0.5)
        ok = _count(st_ref, nch, ck, lambda blk, row0: (blk == v) & (key_index(row0) <= mid_j)) >= need
        return jnp.where(ok, lo_j, mid_j), jnp.where(ok, mid_j, hi_j)

    n_j = int(np.ceil(np.log2(nkeys))) + 1
    _, j_last = lax.fori_loop(0, n_j, j_body, (full(-1.0), full(float(nkeys - 1))))

    def drop_body(c, _):
        base = pl.multiple_of(c * ck, ck)
        for j in range(ck // SUBLANES):
            rows = pl.ds(base + j * SUBLANES, SUBLANES)
            blk = st_ref[rows, :]
            drop = tied & (blk == v) & (key_index(base + j * SUBLANES) > j_last)
            st_ref[rows, :] = jnp.where(drop, NEG, blk)
        return 0
    lax.fori_loop(0, nch, drop_body, 0)


def _rows_min(x):
    return jnp.min(x.reshape(x.shape[0] // SUBLANES, SUBLANES, x.shape[1]), axis=0)


def _rows_max(x):
    return jnp.max(x.reshape(x.shape[0] // SUBLANES, SUBLANES, x.shape[1]), axis=0)


def _rows_sum(x):
    return jnp.sum(x.reshape(x.shape[0] // SUBLANES, SUBLANES, x.shape[1]), axis=0)


def _dsa_prompt_kernel(kip_ref, qit_ref, wit_ref, kp_ref, qt_ref, vtt_ref, o_ref, st_ref, thr_ref,
                       m_ref, l_ref, acc_ref, lg_ref, pr_ref, *, tq, topk, n_fast, rs):
    ck = tq
    i = pl.program_id(1)
    nch = i + 1
    n_pairs = qt_ref.shape[0] // LANES
    pairs_per_kv = n_pairs // KV_HEADS
    bcast = lambda row, n: jnp.broadcast_to(row, (n, tq))

    key_l = lax.broadcasted_iota(jnp.int32, (rs, tq), 0)
    qry_l = lax.broadcasted_iota(jnp.int32, (rs, tq), 1)
    w_rows = [bcast(wit_ref[h:h + 1, :], rs) for h in range(IDX_HEADS)]

    def score_body(c, carry):
        mn, mx = carry
        base = pl.multiple_of(c * ck, ck)
        off_diag = c < i
        for r in range(ck // rs):
            rows = pl.ds(base + r * rs, rs)
            k_lo = kip_ref[rows, 0:LANES]
            k_hi = kip_ref[rows, LANES:2 * LANES]
            acc = jnp.zeros((rs, tq), F32)
            for p in range(IDX_HEADS // 2):
                rhs = qit_ref[p * LANES:(p + 1) * LANES, :]
                acc = acc + jnp.maximum(jnp.dot(k_lo, rhs, preferred_element_type=F32), 0.0) * w_rows[2 * p]
                acc = acc + jnp.maximum(jnp.dot(k_hi, rhs, preferred_element_type=F32), 0.0) * w_rows[2 * p + 1]
            valid = off_diag | (key_l + r * rs <= qry_l)
            st_ref[rows, :] = jnp.where(valid, acc, NEG)
            mn = jnp.minimum(mn, _rows_min(jnp.where(valid, acc, POS)))
            mx = jnp.maximum(mx, _rows_max(jnp.where(valid, acc, NEG)))
        return mn, mx

    mn, mx = lax.fori_loop(0, nch, score_body,
                           (jnp.full((SUBLANES, tq), POS, F32), jnp.full((SUBLANES, tq), NEG, F32)))
    _select_threshold(st_ref, thr_ref, nch, ck, _all_sublanes(mn, jnp.minimum), _all_sublanes(mx, jnp.maximum),
                      topk, n_fast)

    thr = bcast(thr_ref[0:1, :], ck)
    m_ref[...] = jnp.full(m_ref.shape, SOFTMAX_NEG, F32)
    l_ref[...] = jnp.zeros(l_ref.shape, F32)
    acc_ref[...] = jnp.zeros(acc_ref.shape, F32)

    def att_body(c, _):
        rows = pl.ds(pl.multiple_of(c * ck, ck), ck)
        sel = st_ref[rows, :] >= thr
        slab_of = lambda h: 2 * (h // 2 // pairs_per_kv) + h % 2
        n_heads = 2 * n_pairs
        for h in range(n_heads):
            slab = slab_of(h)
            kx = kp_ref[rows, slab * LANES:(slab + 1) * LANES]
            qt_pair = qt_ref[(h // 2) * LANES:(h // 2 + 1) * LANES, :]
            lg_ref[h] = jnp.where(sel, jnp.dot(kx, qt_pair, preferred_element_type=F32), SOFTMAX_NEG)
        alpha = []
        for h in range(n_heads):
            lg = lg_ref[h]
            m_old = m_ref[h]
            m_new = jnp.maximum(m_old, _all_sublanes(_rows_max(lg), jnp.maximum))
            alpha.append(jnp.exp2(m_old - m_new))
            m_ref[h] = m_new
            pr_ref[h] = jnp.exp2(lg - bcast(m_new[0:1, :], ck)).astype(BF16)
        ones = jnp.ones((ONES_ROWS, ck), BF16)
        for p in range(n_pairs):
            pv = []
            for h in (2 * p, 2 * p + 1):
                vt1 = jnp.concatenate([vtt_ref[slab_of(h) * LANES:(slab_of(h) + 1) * LANES, rows], ones], axis=0)
                pv1 = jnp.dot(vt1, pr_ref[h], preferred_element_type=F32)
                l_ref[h] = alpha[h] * l_ref[h] + pv1[LANES:LANES + SUBLANES, :]
                pv.append(pv1[:LANES, :])
            scale = jnp.concatenate([bcast(alpha[2 * p][0:1, :], HEAD_DIM), bcast(alpha[2 * p + 1][0:1, :], HEAD_DIM)],
                                    axis=0)
            acc_ref[p] = acc_ref[p] * scale + pv[0] + pv[1]
        return 0

    lax.fori_loop(0, nch, att_body, 0)
    for p in range(n_pairs):
        denom = jnp.concatenate([bcast(l_ref[2 * p, 0:1, :], HEAD_DIM), bcast(l_ref[2 * p + 1, 0:1, :], HEAD_DIM)],
                                axis=0)
        o_ref[:, p * LANES:(p + 1) * LANES] = (acc_ref[p] / denom).T.astype(o_ref.dtype)


def _dsa_prompt(kip, qit, wit, kp, qt, vtt, *, nb, seq, tq, topk, n_fast):
    n_q = qt.shape[1]
    spb = seq // tq
    per_b_rows = lambda a: pl.BlockSpec((seq, a.shape[1]), lambda b, i: (b, 0))
    q_cols = lambda a: pl.BlockSpec((None, a.shape[1], tq), lambda b, i: (b, 0, i))
    return pl.pallas_call(
        functools.partial(_dsa_prompt_kernel, tq=tq, topk=topk, n_fast=n_fast, rs=min(tq, 128)),
        out_shape=jax.ShapeDtypeStruct((nb * seq, n_q), BF16),
        grid=(nb, spb),
        in_specs=[per_b_rows(kip), q_cols(qit), q_cols(wit), per_b_rows(kp), q_cols(qt),
                  pl.BlockSpec((None, vtt.shape[1], seq), lambda b, i: (b, 0, 0))],
        out_specs=pl.BlockSpec((tq, n_q), lambda b, i: (b * spb + i, 0)),
        scratch_shapes=[pltpu.VMEM((seq, tq), F32), pltpu.VMEM((SUBLANES, tq), F32),
                        pltpu.VMEM((n_q // HEAD_DIM, SUBLANES, tq), F32),
                        pltpu.VMEM((n_q // HEAD_DIM, SUBLANES, tq), F32),
                        pltpu.VMEM((n_q // LANES, LANES, tq), F32),
                        pltpu.VMEM((n_q // HEAD_DIM, tq, tq), F32),
                        pltpu.VMEM((n_q // HEAD_DIM, tq, tq), BF16)],
        compiler_params=_cparams("parallel", "arbitrary"),
        name="dsa_prompt",
    )(kip, qit, wit, kp, qt, vtt)


def _retention_tables(chunk, n_heads):
    log_g = np.log1p(-np.exp2(-5.0 - np.arange(n_heads, dtype=np.float64)))
    i = np.arange(chunk, dtype=np.float64)
    diff = i[:, None] - i[None, :]
    decay = np.where(diff >= 0, np.exp(log_g[:, None, None] * np.maximum(diff, 0.0)), 0.0)
    q_dec = np.exp(log_g[:, None] * (i[None, :] + 1.0))
    k_dec = np.exp(log_g[:, None] * (chunk - 1.0 - i)[None, :])
    c_dec = np.exp(log_g * chunk)
    return decay, q_dec, k_dec, c_dec


def _group_norm_gate(o, rg, gain):
    mu = jnp.mean(o, axis=-1, keepdims=True)
    var = jnp.mean(jnp.square(o - mu), axis=-1, keepdims=True)
    on = ((o - mu) * lax.rsqrt(var + GN_EPS)) * gain
    return jax.nn.silu(rg) * on


def _retention_prompt_kernel(rq_ref, rkt_ref, rv_ref, rg_ref, gain_ref, dmat_ref, qdec_ref, kdec_ref, cdec_ref,
                             o_ref, st_ref, state_ref, *, chunk, n_heads):
    j = pl.program_id(1)

    @pl.when(j == 0)
    def _():
        state_ref[...] = jnp.zeros_like(state_ref)

    n_cc = rq_ref.shape[0] // chunk
    dot = functools.partial(jnp.dot, preferred_element_type=F32)
    tiles = [(cc, h, slice(cc * chunk, (cc + 1) * chunk), slice(h * RET_DV, (h + 1) * RET_DV))
             for cc in range(n_cc) for h in range(n_heads)]
    inner, update = {}, {}
    for cc, h, rows, cols in tiles:
        kt = rkt_ref[cols, rows]
        inner[cc, h] = (dot(rq_ref[rows, cols], kt) * dmat_ref[h]).astype(BF16)
        update[cc, h] = dot((kt.astype(F32) * kdec_ref[h]).astype(BF16), rv_ref[rows, cols])
    cross = {}
    for h in range(n_heads):
        state = state_ref[h]
        for cc in range(n_cc):
            rows, cols = slice(cc * chunk, (cc + 1) * chunk), slice(h * RET_DV, (h + 1) * RET_DV)
            cross[cc, h] = dot(rq_ref[rows, cols], state.astype(BF16)) * qdec_ref[h]
            state = cdec_ref[h] * state + update[cc, h]
        state_ref[h] = state
    for cc, h, rows, cols in tiles:
        o = dot(inner[cc, h], rv_ref[rows, cols]) + cross[cc, h]
        o_ref[rows, cols] = _group_norm_gate(o, rg_ref[rows, cols], gain_ref[:, cols]).astype(o_ref.dtype)

    @pl.when(j == pl.num_programs(1) - 1)
    def _():
        st_ref[...] = state_ref[...]


def _retention_prompt(rq, rkt, rv, rg, g_retn, *, nb, seq, tr):
    n_ret = rq.shape[1]
    n_heads = n_ret // RET_DV
    chunk = RET_CHUNK
    decay, q_dec, k_dec, c_dec = _retention_tables(chunk, n_heads)
    decay = jnp.asarray(decay, F32)
    qdec_b = jnp.asarray(np.broadcast_to(q_dec[:, :, None], (n_heads, chunk, LANES)), F32)
    kdec_b = jnp.asarray(k_dec[:, None, :], F32)
    cdec_b = jnp.asarray(np.broadcast_to(c_dec[:, None, None], (n_heads, 1, LANES)), F32)
    spb = seq // tr
    row = lambda: pl.BlockSpec((tr, n_ret), lambda b, j: (b * spb + j, 0))
    return pl.pallas_call(
        functools.partial(_retention_prompt_kernel, chunk=chunk, n_heads=n_heads),
        out_shape=(jax.ShapeDtypeStruct((nb * seq, n_ret), BF16),
                   jax.ShapeDtypeStruct((nb, n_heads, RET_DK, RET_DV), F32)),
        grid=(nb, spb),
        in_specs=[row(), pl.BlockSpec((None, n_ret, tr), lambda b, j: (b, 0, j)), row(), row(),
                  _const_spec((1, n_ret)), _const_spec((n_heads, chunk, chunk)),
                  _const_spec((n_heads, chunk, LANES)), _const_spec((n_heads, 1, chunk)),
                  _const_spec((n_heads, 1, LANES))],
        out_specs=(row(), pl.BlockSpec((None, n_heads, RET_DK, RET_DV), lambda b, j: (b, 0, 0, 0))),
        scratch_shapes=[pltpu.VMEM((n_heads, RET_DK, RET_DV), F32)],
        compiler_params=_cparams("parallel", "arbitrary"),
        name="retention_prompt",
    )(rq, rkt, rv, rg, g_retn.reshape(1, n_ret), decay, qdec_b, kdec_b, cdec_b)


def _rms(x, g):
    return (x * lax.rsqrt(jnp.mean(x * x, axis=-1, keepdims=True) + EPS)) * g


def _tail_kernel(h_ref, attn_ref, ret_ref, p_ref, wo_ref, g_ffn_ref, w_ffn_in_ref, w_down_ref,
                 g_ple_ref, w_pg_ref, w_pp_ref, g_fin_ref, y_ref):
    dot = functools.partial(jnp.dot, preferred_element_type=F32)
    n_attn = attn_ref.shape[1]
    d_ff = w_down_ref.shape[0]
    h = h_ref[...] + dot(attn_ref[...], wo_ref[:n_attn, :]) + dot(ret_ref[...], wo_ref[n_attn:, :])
    f = _rms(h, g_ffn_ref[...]).astype(BF16)
    act = (jax.nn.silu(dot(f, w_ffn_in_ref[:, :d_ff])) * dot(f, w_ffn_in_ref[:, d_ff:])).astype(BF16)
    h = h + dot(act, w_down_ref[...])
    gate = jax.nn.sigmoid(dot(_rms(h, g_ple_ref[...]).astype(BF16), w_pg_ref[...]))
    h = h + gate * dot(p_ref[...].astype(BF16), w_pp_ref[...])
    y_ref[...] = _rms(h, g_fin_ref[...])


def _tail(h2d, attn, ret, p2d, w_out, g_ffn, w_ffn_in, w_ffn_out, g_ple, w_ple_gate, w_ple_proj, g_final, *, tm):
    rows, d = h2d.shape
    n_attn = attn.shape[1]
    assert w_ffn_in.shape[1] == 2 * w_ffn_out.shape[0] and w_ffn_out.shape[0] % LANES == 0
    ws = [w_out.astype(BF16), g_ffn.reshape(1, d), w_ffn_in.astype(BF16), w_ffn_out.astype(BF16),
          g_ple.reshape(1, d), w_ple_gate.astype(BF16), w_ple_proj.astype(BF16), g_final.reshape(1, d)]
    row = lambda w: pl.BlockSpec((tm, w), lambda i: (i, 0))
    return pl.pallas_call(
        _tail_kernel,
        out_shape=jax.ShapeDtypeStruct((rows, d), F32),
        grid=(rows // tm,),
        in_specs=[row(d), row(n_attn), row(ret.shape[1]), row(p2d.shape[1])] + [_const_spec(w.shape) for w in ws],
        out_specs=row(d),
        compiler_params=_cparams("parallel"),
        name="tail",
    )(h2d, attn, ret, p2d, *ws)


def _page_copies(pools, bufs, sems, pt_ref, b, c, slot, pages_per_chunk):
    copies = []
    for pool, buf in zip(pools, bufs):
        for p in range(pages_per_chunk):
            page = pt_ref[b, c * pages_per_chunk + p]
            copies.append(pltpu.make_async_copy(pool.at[page], buf.at[slot, p], sems.at[slot]))
    return copies


def _chunk_keys_on_lanes(pages):
    return jnp.concatenate([pages[p] for p in range(pages.shape[0])], axis=1).astype(BF16)


def _stream_chunks(pools, bufs, sems, pt_ref, n_chunks, pages_per_chunk, compute):
    b = pl.program_id(0)
    nb = pl.num_programs(0)

    def start(bb, cc, slot):
        for cp in _page_copies(pools, bufs, sems, pt_ref, bb, cc, slot, pages_per_chunk):
            cp.start()

    @pl.when(b == 0)
    def _():
        start(0, 0, 0)

    first = 0 if n_chunks % 2 == 0 else b % 2
    for c in range(n_chunks):
        slot = (first + c) % 2
        if c + 1 < n_chunks:
            start(b, c + 1, 1 - slot)
        else:
            @pl.when(b + 1 < nb)
            def _():
                start(b + 1, 0, 1 - slot)
        for cp in _page_copies(pools, bufs, sems, pt_ref, b, c, slot, pages_per_chunk):
            cp.wait()
        compute(c, slot)


def _tile_lanes(x, width):
    return x if width == LANES else jnp.concatenate([x] * (width // LANES), axis=1)


def _nt_dot(a, b):
    return lax.dot_general(a, b, (((1,), (1,)), ((), ())), preferred_element_type=F32)


def _head_sum(x, n_t):
    return jnp.sum(x.reshape(n_t, IDX_HEADS, x.shape[-1]), axis=1)


def _sample_scores_kernel(pt_ref, qi_ref, wi_ref, kinew_ref, pool_ref, s_ref, buf, sems,
                          *, n_t, n_chunks, pages_per_chunk, pages_per_dot):
    ck = pages_per_chunk * PAGE_SIZE
    qi = qi_ref[...]
    w = wi_ref[...]

    def compute(c, slot):
        for p0 in range(0, pages_per_chunk, pages_per_dot):
            kit = _chunk_keys_on_lanes(buf[slot, p0:p0 + pages_per_dot])
            a = jnp.dot(qi, kit, preferred_element_type=F32)
            col0 = c * ck + p0 * PAGE_SIZE
            s_ref[:, col0:col0 + pages_per_dot * PAGE_SIZE] = _head_sum(jnp.maximum(a, 0.0) * w, n_t)

    _stream_chunks([pool_ref], [buf], sems, pt_ref, n_chunks, pages_per_chunk, compute)

    a = jnp.dot(qi, kinew_ref[...], preferred_element_type=F32)
    s_new = _head_sum(jnp.maximum(a, 0.0) * w, n_t)
    t_q = lax.broadcasted_iota(jnp.int32, s_new.shape, 0)
    t_k = lax.broadcasted_iota(jnp.int32, s_new.shape, 1)
    s_ref[:, n_chunks * ck:] = jnp.where(t_k <= t_q, s_new, NEG)


def _sample_scores(page_table, qi_s, wi_s, kinew, pool_ik, *, n_t, pages_per_chunk):
    db, n_pages = page_table.shape
    n_chunks = n_pages // pages_per_chunk
    past = n_pages * PAGE_SIZE
    rows = n_t * IDX_HEADS
    grid_spec = pltpu.PrefetchScalarGridSpec(
        num_scalar_prefetch=1,
        grid=(db,),
        in_specs=[pl.BlockSpec((None, rows, IDX_DIM), lambda b, pt: (b, 0, 0)),
                  pl.BlockSpec((None, rows, 1), lambda b, pt: (b, 0, 0)),
                  pl.BlockSpec((None, IDX_DIM, PAGE_SIZE), lambda b, pt: (b, 0, 0)),
                  pl.BlockSpec(memory_space=pl.ANY)],
        out_specs=pl.BlockSpec((None, n_t, past + PAGE_SIZE), lambda b, pt: (b, 0, 0)),
        scratch_shapes=[pltpu.VMEM((2, pages_per_chunk, IDX_DIM, PAGE_SIZE), F32),
                        pltpu.SemaphoreType.DMA((2,))],
    )
    return pl.pallas_call(
        functools.partial(_sample_scores_kernel, n_t=n_t, n_chunks=n_chunks, pages_per_chunk=pages_per_chunk,
                          pages_per_dot=min(16, pages_per_chunk)),
        out_shape=jax.ShapeDtypeStruct((db, n_t, past + PAGE_SIZE), F32),
        grid_spec=grid_spec,
        compiler_params=_cparams("arbitrary"),
        name="sample_scores",
    )(page_table, qi_s, wi_s, kinew, pool_ik)


def _sample_select_kernel(s_in_ref, s_out_ref, thr_ref, st_ref, thr8_ref, *, ck, topk, n_fast):
    nq, nkeys = s_in_ref.shape
    nch = nkeys // ck + 0 * pl.program_id(0)
    chunk = lambda c: pl.ds(pl.multiple_of(c * ck, ck), ck)

    def to_lanes(c, _):
        st_ref[chunk(c), :] = s_in_ref[:, chunk(c)].T
        return 0
    lax.fori_loop(0, nch, to_lanes, 0)

    mn, mx = _fold_keys(
        st_ref, nch, ck, (jnp.full((SUBLANES, nq), POS, F32), jnp.full((SUBLANES, nq), NEG, F32)),
        lambda cr, blk, _r: (jnp.minimum(cr[0], jnp.where(blk > 0.5 * NEG, blk, POS)), jnp.maximum(cr[1], blk)),
        lambda x, y: (jnp.minimum(x[0], y[0]), jnp.maximum(x[1], y[1])))
    _select_threshold(st_ref, thr8_ref, nch, ck, _all_sublanes(mn, jnp.minimum), _all_sublanes(mx, jnp.maximum),
                      topk, n_fast)
    def to_rows(c, _):
        s_out_ref[:, chunk(c)] = st_ref[chunk(c), :].T
        return 0
    lax.fori_loop(0, nch, to_rows, 0)
    thr_ref[...] = jnp.broadcast_to(thr8_ref[0:1, :], (LANES, nq)).T


def _sample_select(s2d, *, topk, n_fast):
    rows, ncols = s2d.shape
    ck = max(w for w in (LANES, 2 * LANES, 3 * LANES, 4 * LANES) if ncols % w == 0)
    full = pl.BlockSpec((rows, ncols), lambda i: (0, 0), pipeline_mode=pl.Buffered(1))
    return pl.pallas_call(
        functools.partial(_sample_select_kernel, ck=ck, topk=topk, n_fast=n_fast),
        out_shape=(jax.ShapeDtypeStruct((rows, ncols), F32), jax.ShapeDtypeStruct((rows, LANES), F32)),
        grid=(1,),
        in_specs=[full],
        out_specs=(full, pl.BlockSpec((rows, LANES), lambda i: (0, 0))),
        scratch_shapes=[pltpu.VMEM((ncols, rows), F32), pltpu.VMEM((SUBLANES, rows), F32)],
        compiler_params=_cparams("arbitrary"),
        name="sample_select",
    )(s2d)


def _sample_attend_kernel(pt_ref, q_ref, s_ref, thr_ref, knew_ref, vnew_ref, poolk_ref, poolv_ref, o_ref,
                          kbuf, vbuf, sems, *, n_t, n_chunks, pages_per_chunk):
    ck = pages_per_chunk * PAGE_SIZE
    rows = q_ref.shape[1]
    reps = rows // n_t
    thr = jnp.concatenate([thr_ref[...]] * reps, axis=0)
    state = [[jnp.full((rows, 1), SOFTMAX_NEG, F32), jnp.zeros((rows, 1), F32), jnp.zeros((rows, HEAD_DIM), F32)]
             for _ in range(KV_HEADS)]

    def attend(kt, vt, s_blk):
        width = s_blk.shape[1]
        sel = jnp.concatenate([s_blk] * reps, axis=0) >= _tile_lanes(thr, width)
        lgs = [jnp.where(sel, jnp.dot(q_ref[g], kt[g], preferred_element_type=F32), SOFTMAX_NEG)
               for g in range(KV_HEADS)]
        stats = []
        for g in range(KV_HEADS):
            m, l, _ = state[g]
            m_new = jnp.maximum(m, jnp.max(lgs[g], axis=1, keepdims=True))
            alpha = jnp.exp2(m - m_new)
            pr = jnp.exp2(lgs[g] - m_new)
            stats.append((m_new, alpha, alpha * l + jnp.sum(pr, axis=1, keepdims=True), pr.astype(BF16)))
        for g in range(KV_HEADS):
            m_new, alpha, l_new, pr = stats[g]
            state[g] = [m_new, l_new, alpha * state[g][2] + _nt_dot(pr, vt[g])]

    def compute(c, slot):
        attend([_chunk_keys_on_lanes(kbuf[slot, :, g]) for g in range(KV_HEADS)],
               [_chunk_keys_on_lanes(vbuf[slot, :, g]) for g in range(KV_HEADS)], s_ref[:, c * ck:(c + 1) * ck])

    _stream_chunks([poolk_ref, poolv_ref], [kbuf, vbuf], sems, pt_ref, n_chunks, pages_per_chunk, compute)
    attend(knew_ref[...], vnew_ref[...], s_ref[:, n_chunks * ck:])
    for g in range(KV_HEADS):
        o_ref[g] = state[g][2] / state[g][1]


def _sample_attend(page_table, q_bd, s3d, thr3d, knew, vnew, pool_k, pool_v, *, n_t, pages_per_chunk):
    db, n_pages = page_table.shape
    n_chunks = n_pages // pages_per_chunk
    rows = q_bd.shape[2]
    ncols = s3d.shape[2]
    per_b = lambda r, w: pl.BlockSpec((None, r, w), lambda b, pt: (b, 0, 0))
    per_b4 = lambda r, w: pl.BlockSpec((None, KV_HEADS, r, w), lambda b, pt: (b, 0, 0, 0))
    grid_spec = pltpu.PrefetchScalarGridSpec(
        num_scalar_prefetch=1,
        grid=(db,),
        in_specs=[per_b4(rows, HEAD_DIM), per_b(n_t, ncols), per_b(n_t, LANES), per_b4(HEAD_DIM, PAGE_SIZE),
                  per_b4(HEAD_DIM, PAGE_SIZE), pl.BlockSpec(memory_space=pl.ANY), pl.BlockSpec(memory_space=pl.ANY)],
        out_specs=per_b4(rows, HEAD_DIM),
        scratch_shapes=[pltpu.VMEM((2, pages_per_chunk, KV_HEADS, HEAD_DIM, PAGE_SIZE), F32),
                        pltpu.VMEM((2, pages_per_chunk, KV_HEADS, HEAD_DIM, PAGE_SIZE), F32),
                        pltpu.SemaphoreType.DMA((2,))],
    )
    return pl.pallas_call(
        functools.partial(_sample_attend_kernel, n_t=n_t, n_chunks=n_chunks, pages_per_chunk=pages_per_chunk),
        out_shape=jax.ShapeDtypeStruct((db, KV_HEADS, rows, HEAD_DIM), F32),
        grid_spec=grid_spec,
        compiler_params=_cparams("arbitrary"),
        name="sample_attend",
    )(page_table, q_bd, s3d, thr3d, knew, vnew, pool_k, pool_v)


def _retention_sample_kernel(rq_ref, rkt_ref, rv_ref, rg_ref, gain_ref, dfull_ref, qdec_ref, kdec_ref, cdec_ref,
                             st_in_ref, o_ref, st_out_ref, *, n_t, n_heads):
    b = pl.program_id(0)

    @pl.when(b == 0)
    def _():
        o_ref[...] = jnp.zeros_like(o_ref)

    n_rows = rq_ref.shape[0]
    row = lax.broadcasted_iota(jnp.int32, (n_rows, RET_DV), 0)
    mine = (row >= b * n_t) & (row < (b + 1) * n_t)
    for h in range(n_heads):
        cols = slice(h * RET_DV, (h + 1) * RET_DV)
        q = jnp.where(mine, rq_ref[:, cols], 0.0).astype(BF16)
        v = rv_ref[:, cols]
        kt = rkt_ref[cols, :]
        state = st_in_ref[h]
        inner = jnp.dot(q, kt, preferred_element_type=F32) * dfull_ref[h]
        o = jnp.dot(inner.astype(BF16), v, preferred_element_type=F32)
        o = o + jnp.dot(q, state.astype(BF16), preferred_element_type=F32) * qdec_ref[h]
        vd = jnp.where(mine, v.astype(F32) * kdec_ref[h], 0.0).astype(BF16)
        st_out_ref[h] = cdec_ref[h] * state + jnp.dot(kt, vd, preferred_element_type=F32)
        gated = _group_norm_gate(o, rg_ref[:, cols], gain_ref[:, cols])
        o_ref[:, cols] = jnp.where(mine, gated, o_ref[:, cols].astype(F32)).astype(o_ref.dtype)


def _retention_sample(rq, rkt, rv, rg, g_retn, state, *, db, n_t):
    n_rows, n_ret = rq.shape
    n_heads = n_ret // RET_DV
    decay, q_dec, k_dec, c_dec = _retention_tables(n_t, n_heads)
    same = np.kron(np.eye(db), np.ones((n_t, n_t)))
    dfull = jnp.asarray(same[None] * np.tile(decay, (1, db, db)), F32)
    qdec_b = jnp.asarray(np.broadcast_to(np.tile(q_dec, (1, db))[:, :, None], (n_heads, n_rows, RET_DV)), F32)
    kdec_b = jnp.asarray(np.broadcast_to(np.tile(k_dec, (1, db))[:, :, None], (n_heads, n_rows, RET_DV)), F32)
    cdec_b = jnp.asarray(np.broadcast_to(c_dec[:, None, None], (n_heads, 1, RET_DV)), F32)
    const = lambda shape: pl.BlockSpec(shape, lambda b: (0,) * len(shape))
    st_spec = pl.BlockSpec((None, n_heads, RET_DK, RET_DV), lambda b: (b, 0, 0, 0))
    return pl.pallas_call(
        functools.partial(_retention_sample_kernel, n_t=n_t, n_heads=n_heads),
        out_shape=(jax.ShapeDtypeStruct((n_rows, n_ret), BF16),
                   jax.ShapeDtypeStruct((db, n_heads, RET_DK, RET_DV), F32)),
        grid=(db,),
        in_specs=[const((n_rows, n_ret)), const((n_ret, n_rows)), const((n_rows, n_ret)), const((n_rows, n_ret)),
                  const((1, n_ret)), const((n_heads, n_rows, n_rows)), const((n_heads, n_rows, RET_DV)),
                  const((n_heads, n_rows, RET_DV)), const((n_heads, 1, RET_DV)), st_spec],
        out_specs=(const((n_rows, n_ret)), st_spec),
        compiler_params=_cparams("arbitrary"),
        name="retention_sample",
    )(rq, rkt, rv, rg, g_retn.reshape(1, n_ret), dfull, qdec_b, kdec_b, cdec_b, state)


def _mixer_dims(w_in, d_model):
    n_q = d_model // 2
    n_ret = d_model // 2
    n_kv = KV_HEADS * HEAD_DIM
    n_qi = IDX_HEADS * IDX_DIM
    assert n_kv == LANES and 2 * n_kv + n_q + n_qi + IDX_DIM + IDX_HEADS + 4 * n_ret == w_in.shape[1]
    return n_q, n_kv, n_qi, n_ret


def _prompt_layer(x, p, weights, *, n_fast=24):
    g_mix, w_in, g_retn, w_out, g_ffn, w_ffn_in, w_ffn_out, g_ple, w_ple_gate, w_ple_proj, g_final = weights
    nb, seq, d = x.shape
    dims = _mixer_dims(w_in, d)
    w_packed = _pack_w_in(w_in, *dims)
    x2d = x.reshape(nb * seq, d)
    tm = min(512, seq)
    (qt, kt, vt, qit, kit, wit, rq, rkt, rv, rg, kp, vtt, kip) = _inproj(
        x2d, np.arange(seq), g_mix, w_packed, seq=seq, tm=tm, dims=dims)
    topk = min(TOPK_MAX, seq // 4)
    attn = _dsa_prompt(kip, qit, wit, kp, qt, vtt, nb=nb, seq=seq, tq=min(256, seq), topk=topk, n_fast=n_fast)
    ret, st = _retention_prompt(rq, rkt, rv, rg, g_retn, nb=nb, seq=seq, tr=min(512, seq))
    y = _tail(x2d, attn, ret, p.reshape(nb * seq, -1), w_out, g_ffn, w_ffn_in, w_ffn_out,
              g_ple, w_ple_gate, w_ple_proj, g_final, tm=min(256, nb * seq))
    heads_last = lambda a: a.reshape(nb, KV_HEADS, HEAD_DIM, seq).transpose(0, 3, 1, 2)
    return y.reshape(nb, seq, d), heads_last(kt), heads_last(vt), kit.transpose(0, 2, 1), st


def _sample_layer(x, p, pool_k, pool_v, pool_ik, state, page_table, weights, *, n_fast=24):
    g_mix, w_in, g_retn, w_out, g_ffn, w_ffn_in, w_ffn_out, g_ple, w_ple_gate, w_ple_proj, g_final = weights
    db, n_t, d = x.shape
    n_rows = db * n_t
    n_pages = page_table.shape[1]
    past = n_pages * PAGE_SIZE
    dims = _mixer_dims(w_in, d)
    n_q = dims[0]
    w_packed = _pack_w_in(w_in, *dims)
    x2d = x.reshape(n_rows, d)
    pos = np.tile(past + np.arange(n_t), db)
    (qt, kt, vt, qit, kit, wit, rq, rkt, rv, rg, _, _, _) = _inproj(
        x2d, pos, g_mix, w_packed, seq=n_rows, tm=n_rows, dims=dims)
    q, qi, wi = qt[0].T, qit[0].T, wit[0].T

    def new_pages(at, heads):
        a = at[0].reshape(heads, -1, db, n_t).transpose(2, 0, 1, 3)
        return jnp.pad(a, ((0, 0), (0, 0), (0, 0), (0, PAGE_SIZE - n_t))).astype(BF16)
    group = n_q // HEAD_DIM // KV_HEADS
    qg = q.reshape(db, n_t, KV_HEADS, group, HEAD_DIM).transpose(0, 2, 3, 1, 4)
    qg = qg.reshape(db, KV_HEADS, group * n_t, HEAD_DIM)
    pages_per_chunk = max(1, min(64, n_pages // 2))

    s3d = _sample_scores(page_table, qi.reshape(db, n_t * IDX_HEADS, IDX_DIM), wi.reshape(db, n_t * IDX_HEADS, 1),
                         new_pages(kit, 1)[:, 0], pool_ik.transpose(0, 2, 1), n_t=n_t, pages_per_chunk=n_pages)
    topk = min(TOPK_MAX, (past + n_t) // 4)
    s_sel, thr = _sample_select(s3d.reshape(n_rows, -1), topk=topk, n_fast=n_fast)
    o = _sample_attend(page_table, qg, s_sel.reshape(db, n_t, -1), thr.reshape(db, n_t, LANES),
                       new_pages(kt, KV_HEADS), new_pages(vt, KV_HEADS), pool_k.transpose(0, 2, 3, 1),
                       pool_v.transpose(0, 2, 3, 1), n_t=n_t, pages_per_chunk=pages_per_chunk)
    attn = o.reshape(db, KV_HEADS, group, n_t, HEAD_DIM).transpose(0, 3, 1, 2, 4).reshape(n_rows, n_q).astype(BF16)

    ret, st = _retention_sample(rq, rkt[0], rv, rg, g_retn, state, db=db, n_t=n_t)
    y = _tail(x2d, attn, ret, p.reshape(n_rows, -1), w_out, g_ffn, w_ffn_in, w_ffn_out,
              g_ple, w_ple_gate, w_ple_proj, g_final, tm=n_rows)
    heads_last = lambda a: a[0].reshape(KV_HEADS, HEAD_DIM, db, n_t).transpose(2, 3, 0, 1)
    return y.reshape(db, n_t, d), heads_last(kt), heads_last(vt), kit[0].T.reshape(db, n_t, IDX_DIM), st


def kernel(x_prompt, x_sample, cache_k, cache_v, cache_idx_k, state_retn, page_table, p_prompt, p_sample, g_mix, w_in,
           g_retn, w_out, g_ffn, w_ffn_in, w_ffn_out, g_ple, w_ple_gate, w_ple_proj, g_final):
    depth = w_in.shape[0]
    assert depth == 1, "the final RMSNorm is fused into the single layer's tail kernel"
    weights = (g_mix[0], w_in[0], g_retn[0], w_out[0], g_ffn[0], w_ffn_in[0], w_ffn_out[0], g_ple[0], w_ple_gate[0],
               w_ple_proj[0], g_final)
    y_p, k_p, v_p, ik_p, st_p = _prompt_layer(x_prompt, p_prompt[0], weights)
    y_s, k_s, v_s, ik_s, st_s = _sample_layer(x_sample, p_sample[0], cache_k[0], cache_v[0], cache_idx_k[0],
                                              state_retn[0], page_table, weights)
    return (y_p, y_s, k_p[None], v_p[None], ik_p[None], st_p[None], k_s[None], v_s[None], ik_s[None], st_s[None])
```

```python
import functools

import jax
import jax.numpy as jnp
import numpy as np
from jax import lax
from jax.experimental import pallas as pl
from jax.experimental.pallas import tpu as pltpu

F32 = jnp.float32
BF16 = jnp.bfloat16

HEAD_DIM = 64
KV_HEADS = 2
IDX_HEADS = 8
IDX_DIM = 64
TOPK_MAX = 256
RET_DK = 128
RET_DV = 128
RET_CHUNK = 128
PAGE_SIZE = 128
ROPE_THETA = 10000.0
EPS = 1e-6
GN_EPS = 1e-5

LANES = 128
SUBLANES = 8
VMEM_LIMIT_BYTES = 56 * 1024 * 1024

Q_SCALE = HEAD_DIM ** -0.5 * 1.4426950408889634
ONES_ROWS = 16

NEG = -3.0e38
POS = 3.0e38
SOFTMAX_NEG = -1.0e30
NO_TIE = 1.0e9


def _cparams(*sem):
    return pltpu.CompilerParams(dimension_semantics=sem, vmem_limit_bytes=VMEM_LIMIT_BYTES)


def _const_spec(shape):
    zeros = (0,) * len(shape)
    return pl.BlockSpec(shape, lambda *_: zeros, pipeline_mode=pl.Buffered(1))


def _rope_tables(pos, dim):
    half = dim // 2
    inv = ROPE_THETA ** (-np.arange(half, dtype=np.float64) / half)
    ang = np.asarray(pos, np.float64)[:, None] * inv[None, :]
    cos, sin = np.cos(ang), np.sin(ang)
    reps = LANES // dim
    cos_t = np.tile(np.concatenate([cos, cos], axis=1), (1, reps))
    sin_t = np.tile(np.concatenate([-sin, sin], axis=1), (1, reps))
    return jnp.asarray(cos_t, F32), jnp.asarray(sin_t, F32)


def _swap_halves_64(x):
    lane = lax.broadcasted_iota(jnp.int32, x.shape, 1)
    first = (lane % 64) < 32
    return jnp.where(first, pltpu.roll(x, 96, 1), pltpu.roll(x, 32, 1))


def _pad_pair_slabs(x, swapped, low):
    return (jnp.where(low, x, 0.0), jnp.where(low, 0.0, swapped), jnp.where(low, swapped, 0.0), jnp.where(low, 0.0, x))


def _inproj_kernel(x_ref, g_ref, w_ref, c64_ref, s64_ref, c128_ref, s128_ref,
                   qt_ref, kt_ref, vt_ref, qit_ref, kit_ref, wit_ref, rq_ref, rkt_ref, rv_ref, rg_ref,
                   kp_ref, vtt_ref, kip_ref, *, n_q, n_qi, n_ret):
    x = x_ref[...]
    tm = x.shape[0]
    ms = jnp.mean(x * x, axis=-1, keepdims=True)
    a = ((x * lax.rsqrt(ms + EPS)) * g_ref[...]).astype(BF16)
    c64, s64 = c64_ref[...], s64_ref[...]
    c128, s128 = c128_ref[...], s128_ref[...]
    low = lax.broadcasted_iota(jnp.int32, (tm, LANES), 1) < HEAD_DIM
    zt = jnp.zeros((HEAD_DIM, tm), BF16)

    col = [0]

    def segment(width):
        y = jnp.dot(a, w_ref[:, col[0]:col[0] + width], preferred_element_type=F32)
        col[0] += width
        return [y[:, j * LANES:(j + 1) * LANES] for j in range(width // LANES)]

    def rope64(y):
        return y * c64 + _swap_halves_64(y) * s64

    def rope128(y):
        return y * c128 + pltpu.roll(y, 64, 1) * s128

    for j, y in enumerate(segment(n_q)):
        qt_ref[j * LANES:(j + 1) * LANES, :] = (rope64(y) * Q_SCALE).T.astype(BF16)
    kk, vv = segment(2 * LANES)
    kk = rope64(kk)
    kt_ref[...] = kk.T
    for s, slab in enumerate(_pad_pair_slabs(kk, pltpu.roll(kk, HEAD_DIM, 1), low)):
        kp_ref[:, s * LANES:(s + 1) * LANES] = slab.astype(BF16)
    vvt = vv.T
    vt_ref[...] = vvt
    vvt = vvt.astype(BF16)
    for g in range(KV_HEADS):
        vtg = vvt[g * HEAD_DIM:(g + 1) * HEAD_DIM, :]
        base = g * 4 * HEAD_DIM
        vtt_ref[base:base + HEAD_DIM, :] = vtg
        vtt_ref[base + HEAD_DIM:base + 2 * HEAD_DIM, :] = zt
        vtt_ref[base + 2 * HEAD_DIM:base + 3 * HEAD_DIM, :] = zt
        vtt_ref[base + 3 * HEAD_DIM:base + 4 * HEAD_DIM, :] = vtg
    for j, y in enumerate(segment(n_qi)):
        qit_ref[j * LANES:(j + 1) * LANES, :] = (rope64(y) * (IDX_DIM ** -0.5)).T.astype(BF16)
    y, _ = segment(2 * LANES)
    ki = rope64(y)
    kit_ref[...] = ki.T[:IDX_DIM, :]
    ki_lo, ki_hi, _, _ = _pad_pair_slabs(ki, pltpu.roll(ki, IDX_DIM, 1), low)
    kip_ref[:, 0:LANES] = ki_lo.astype(BF16)
    kip_ref[:, LANES:2 * LANES] = ki_hi.astype(BF16)
    wit_ref[...] = y.T[IDX_DIM:IDX_DIM + IDX_HEADS, :] * (IDX_HEADS ** -0.5)
    for j, y in enumerate(segment(n_ret)):
        rq_ref[:, j * LANES:(j + 1) * LANES] = rope128(y).astype(BF16)
    for j, y in enumerate(segment(n_ret)):
        rkt_ref[j * LANES:(j + 1) * LANES, :] = (rope128(y) * (RET_DK ** -0.5)).T.astype(BF16)
    for j, y in enumerate(segment(n_ret)):
        rv_ref[:, j * LANES:(j + 1) * LANES] = y.astype(BF16)
    for j, y in enumerate(segment(n_ret)):
        rg_ref[:, j * LANES:(j + 1) * LANES] = y


def _pack_w_in(w_in, n_q, n_kv, n_qi, n_ret):
    d = w_in.shape[0]
    sizes = (n_q, n_kv, n_kv, n_qi, IDX_DIM, IDX_HEADS, n_ret, n_ret, n_ret, n_ret)
    offs = np.concatenate([[0], np.cumsum(sizes)])
    seg = [w_in[:, offs[i]:offs[i + 1]] for i in range(len(sizes))]
    pad = jnp.zeros((d, 2 * LANES - IDX_DIM - IDX_HEADS), w_in.dtype)
    packed = jnp.concatenate(seg[:4] + [seg[4], seg[5], pad] + seg[6:], axis=1)
    return packed.astype(BF16)


def _inproj(x2d, pos, g_mix, w_packed, *, seq, tm, dims):
    n_q, n_kv, n_qi, n_ret = dims
    rows, d = x2d.shape
    nb = rows // seq
    spt = seq // tm
    c64, s64 = _rope_tables(pos, HEAD_DIM)
    c128, s128 = _rope_tables(pos, RET_DK)
    wcols = w_packed.shape[1]

    row = lambda w: pl.BlockSpec((tm, w), lambda i: (i, 0))
    tab = pl.BlockSpec((tm, LANES), lambda i: (i % spt, 0))
    tr = lambda h: pl.BlockSpec((None, h, tm), lambda i: (i // spt, 0, i % spt))
    out_shape = (
        jax.ShapeDtypeStruct((nb, n_q, seq), BF16),
        jax.ShapeDtypeStruct((nb, n_kv, seq), F32),
        jax.ShapeDtypeStruct((nb, n_kv, seq), F32),
        jax.ShapeDtypeStruct((nb, n_qi, seq), BF16),
        jax.ShapeDtypeStruct((nb, IDX_DIM, seq), F32),
        jax.ShapeDtypeStruct((nb, IDX_HEADS, seq), F32),
        jax.ShapeDtypeStruct((rows, n_ret), BF16),
        jax.ShapeDtypeStruct((nb, n_ret, seq), BF16),
        jax.ShapeDtypeStruct((rows, n_ret), BF16),
        jax.ShapeDtypeStruct((rows, n_ret), F32),
        jax.ShapeDtypeStruct((rows, 4 * n_kv), BF16),
        jax.ShapeDtypeStruct((nb, 4 * n_kv, seq), BF16),
        jax.ShapeDtypeStruct((rows, 2 * LANES), BF16),
    )
    out_specs = (tr(n_q), tr(n_kv), tr(n_kv), tr(n_qi), tr(IDX_DIM), tr(IDX_HEADS),
                 row(n_ret), tr(n_ret), row(n_ret), row(n_ret), row(4 * n_kv), tr(4 * n_kv), row(2 * LANES))
    return pl.pallas_call(
        functools.partial(_inproj_kernel, n_q=n_q, n_qi=n_qi, n_ret=n_ret),
        out_shape=out_shape,
        grid=(rows // tm,),
        in_specs=[row(d), _const_spec((1, d)), _const_spec((d, wcols)), tab, tab, tab, tab],
        out_specs=out_specs,
        compiler_params=_cparams("parallel"),
        name="inproj",
    )(x2d, g_mix.reshape(1, d), w_packed, c64, s64, c128, s128)


N_ACC = 4


def _fold_keys(st_ref, nch, ck, init, fn, combine):
    def body(c, carries):
        carries = list(carries)
        base = pl.multiple_of(c * ck, ck)
        chunk = st_ref[pl.ds(base, ck), :]
        for j in range(ck // SUBLANES):
            blk = chunk[j * SUBLANES:(j + 1) * SUBLANES, :]
            carries[j % N_ACC] = fn(carries[j % N_ACC], blk, base + j * SUBLANES)
        return tuple(carries)
    carries = lax.fori_loop(0, nch, body, (init,) * N_ACC)
    out = carries[0]
    for other in carries[1:]:
        out = combine(out, other)
    return out


def _all_sublanes(x, op):
    for shift in (4, 2, 1):
        x = op(x, pltpu.roll(x, shift, 0))
    return x


def _any(mask):
    return jnp.max(jnp.where(mask, 1.0, 0.0)) > 0.5


def _count(st_ref, nch, ck, pred):
    nq = st_ref.shape[1]
    acc = _fold_keys(st_ref, nch, ck, jnp.zeros((SUBLANES, nq), F32),
                     lambda a, blk, row0: a + jnp.where(pred(blk, row0), 1.0, 0.0), jnp.add)
    return _all_sublanes(acc, jnp.add)


def _select_threshold(st_ref, thr_ref, tie_ref, nch, ck, rmin, rmax, topk, n_fast):
    nq = st_ref.shape[1]
    kf = float(topk)
    full = lambda v: jnp.full((SUBLANES, nq), v, F32)

    def fast_body(_, carry):
        lo, hi = carry
        mid = lo + (hi - lo) * 0.5
        up = _count(st_ref, nch, ck, lambda blk, _r: blk > mid) >= kf
        return jnp.where(up, mid, lo), jnp.where(up, hi, mid)

    lo, hi = lax.fori_loop(0, n_fast, fast_body, (rmin, rmax))
    unres = _count(st_ref, nch, ck, lambda blk, _r: blk >= lo) > kf
    thr_ref[...] = lo

    @pl.when(_any(unres))
    def _exact():
        pair_minmax = lambda x, y: (jnp.minimum(x[0], y[0]), jnp.maximum(x[1], y[1]))
        a, b = _fold_keys(
            st_ref, nch, ck, (full(POS), full(NEG)),
            lambda cr, blk, _r: (jnp.minimum(cr[0], jnp.where(blk >= lo, blk, POS)),
                                 jnp.maximum(cr[1], jnp.where(blk <= hi, blk, NEG))),
            pair_minmax)
        lo2 = jnp.where(unres, _all_sublanes(a, jnp.minimum), lo)
        ub = jnp.where(unres, _all_sublanes(b, jnp.maximum), lo)

        def body(carry):
            lo2, ub, _ = carry
            mid = lo2 + (ub - lo2) * 0.5
            mid = jnp.where(mid >= ub, lo2, mid)

            def step(cr, blk, _r):
                gt = blk > mid
                return (cr[0] + jnp.where(gt, 1.0, 0.0), jnp.minimum(cr[1], jnp.where(gt, blk, POS)),
                        jnp.maximum(cr[2], jnp.where(gt, NEG, blk)))
            cnt, a, b = _fold_keys(st_ref, nch, ck, (full(0.0), full(POS), full(NEG)), step,
                                   lambda x, y: (x[0] + y[0],) + pair_minmax(x[1:], y[1:]))
            up = _all_sublanes(cnt, jnp.add) >= kf
            active = lo2 < ub
            lo2n = jnp.where(active & up, _all_sublanes(a, jnp.minimum), lo2)
            ubn = jnp.where(active & jnp.logical_not(up), _all_sublanes(b, jnp.maximum), ub)
            return lo2n, ubn, _any(lo2n < ubn).astype(jnp.int32)

        v, _, _ = lax.while_loop(lambda carry: carry[2] > 0, body, (lo2, ub, _any(lo2 < ub).astype(jnp.int32)))
        thr_ref[...] = jnp.where(unres, v, lo)
        tied = unres & (_count(st_ref, nch, ck, lambda blk, _r: blk >= v) > kf)

        @pl.when(_any(tied))
        def _ties():
            _drop_excess_ties(st_ref, tie_ref, nch, ck, v, tied, kf)


def _drop_excess_ties(st_ref, tie_ref, nch, ck, v, tied, kf):
    nq = st_ref.shape[1]
    full = lambda x: jnp.full((SUBLANES, nq), x, F32)
    need = kf - _count(st_ref, nch, ck, lambda blk, _r: blk > v)
    nkeys = st_ref.shape[0]
    key_in_chunk = lax.broadcasted_iota(jnp.int32, (ck, nq), 0)
    rep = lambda x: jnp.broadcast_to(x[0:1, :], (ck, nq))
    v_c = rep(v)

    def each_chunk(fn):
        def body(c, _):
            base = pl.multiple_of(c * ck, ck)
            fn(pl.ds(base, ck), base)
            return 0
        lax.fori_loop(0, nch, body, 0)

    def mark(rows, row0):
        tie_ref[rows, :] = jnp.where(st_ref[rows, :] == v_c, (key_in_chunk + row0).astype(F32), NO_TIE)
    each_chunk(mark)

    n_j = int(np.ceil(np.log2(nkeys))) + 1
    need_t = jnp.where(tied, need, 0.0)
    need_max = jnp.max(need_t)

    def by_walk():
        def body(carry):
            j_prev, k = carry
            nxt = _fold_keys(tie_ref, nch, ck, full(NO_TIE),
                             lambda a, blk, _r: jnp.minimum(a, jnp.where(blk > j_prev, blk, NO_TIE)), jnp.minimum)
            return jnp.where(need_t > k, _all_sublanes(nxt, jnp.minimum), j_prev), k + 1.0
        return lax.while_loop(lambda carry: carry[1] < need_max, body, (full(-1.0), jnp.float32(0.0)))[0]

    def by_bisection():
        def body(_, carry):
            lo_j, hi_j = carry
            mid_j = jnp.floor((lo_j + hi_j) * 0.5)
            ok = _count(tie_ref, nch, ck, lambda blk, _r: blk <= mid_j) >= need
            return jnp.where(ok, lo_j, mid_j), jnp.where(ok, mid_j, hi_j)
        return lax.fori_loop(0, n_j, body, (full(-1.0), full(float(nkeys - 1))))[1]

    j_last = lax.cond(need_max <= float(n_j), by_walk, by_bisection)

    j_c = rep(jnp.where(tied, j_last, NO_TIE))

    def drop(rows, _row0):
        t = tie_ref[rows, :]
        s = st_ref[rows, :]
        st_ref[rows, :] = jnp.where(t > j_c, jnp.where(t < NO_TIE, NEG, s), s)
    each_chunk(drop)


def _rows_min(x):
    return jnp.min(x.reshape(x.shape[0] // SUBLANES, SUBLANES, x.shape[1]), axis=0)


def _rows_max(x):
    return jnp.max(x.reshape(x.shape[0] // SUBLANES, SUBLANES, x.shape[1]), axis=0)


def _rows_sum(x):
    return jnp.sum(x.reshape(x.shape[0] // SUBLANES, SUBLANES, x.shape[1]), axis=0)


def _dsa_prompt_kernel(kip_ref, qit_ref, wit_ref, kp_ref, qt_ref, vtt_ref, o_ref, st_ref, thr_ref,
                       m_ref, l_ref, acc_ref, lg_ref, pr_ref, tie_ref, *, tq, topk, n_fast, rs):
    ck = tq
    i = pl.program_id(1)
    nch = i + 1
    n_pairs = qt_ref.shape[0] // LANES
    pairs_per_kv = n_pairs // KV_HEADS
    bcast = lambda row, n: jnp.broadcast_to(row, (n, tq))

    key_l = lax.broadcasted_iota(jnp.int32, (rs, tq), 0)
    qry_l = lax.broadcasted_iota(jnp.int32, (rs, tq), 1)
    w_rows = [bcast(wit_ref[h:h + 1, :], rs) for h in range(IDX_HEADS)]

    def score_body(c, carry):
        mn, mx = carry
        base = pl.multiple_of(c * ck, ck)
        slack = jnp.where(c < i, ck, 0)
        for r in range(ck // rs):
            rows = pl.ds(base + r * rs, rs)
            k_lo = kip_ref[rows, 0:LANES]
            k_hi = kip_ref[rows, LANES:2 * LANES]
            acc = jnp.zeros((rs, tq), F32)
            for p in range(IDX_HEADS // 2):
                rhs = qit_ref[p * LANES:(p + 1) * LANES, :]
                acc = acc + jnp.maximum(jnp.dot(k_lo, rhs, preferred_element_type=F32), 0.0) * w_rows[2 * p]
                acc = acc + jnp.maximum(jnp.dot(k_hi, rhs, preferred_element_type=F32), 0.0) * w_rows[2 * p + 1]
            valid = key_l + r * rs <= qry_l + slack
            st_ref[rows, :] = jnp.where(valid, acc, NEG)
            mn = jnp.minimum(mn, _rows_min(jnp.where(valid, acc, POS)))
            mx = jnp.maximum(mx, _rows_max(jnp.where(valid, acc, NEG)))
        return mn, mx

    mn, mx = lax.fori_loop(0, nch, score_body,
                           (jnp.full((SUBLANES, tq), POS, F32), jnp.full((SUBLANES, tq), NEG, F32)))
    _select_threshold(st_ref, thr_ref, tie_ref, nch, ck,
                      _all_sublanes(mn, jnp.minimum), _all_sublanes(mx, jnp.maximum), topk, n_fast)

    thr = bcast(thr_ref[0:1, :], ck)
    m_ref[...] = jnp.full(m_ref.shape, SOFTMAX_NEG, F32)
    l_ref[...] = jnp.zeros(l_ref.shape, F32)
    acc_ref[...] = jnp.zeros(acc_ref.shape, F32)

    def att_body(c, _):
        rows = pl.ds(pl.multiple_of(c * ck, ck), ck)
        sel = st_ref[rows, :] >= thr
        slab_of = lambda h: 2 * (h // 2 // pairs_per_kv) + h % 2
        n_heads = 2 * n_pairs
        for h in range(n_heads):
            slab = slab_of(h)
            kx = kp_ref[rows, slab * LANES:(slab + 1) * LANES]
            qt_pair = qt_ref[(h // 2) * LANES:(h // 2 + 1) * LANES, :]
            lg_ref[h] = jnp.where(sel, jnp.dot(kx, qt_pair, preferred_element_type=F32), SOFTMAX_NEG)
        alpha = []
        for h in range(n_heads):
            lg = lg_ref[h]
            m_old = m_ref[h]
            m_new = jnp.maximum(m_old, _all_sublanes(_rows_max(lg), jnp.maximum))
            alpha.append(jnp.exp2(m_old - m_new))
            m_ref[h] = m_new
            pr_ref[h] = jnp.exp2(lg - bcast(m_new[0:1, :], ck)).astype(BF16)
        ones = jnp.ones((ONES_ROWS, ck), BF16)
        for p in range(n_pairs):
            pv = []
            for h in (2 * p, 2 * p + 1):
                vt1 = jnp.concatenate([vtt_ref[slab_of(h) * LANES:(slab_of(h) + 1) * LANES, rows], ones], axis=0)
                pv1 = jnp.dot(vt1, pr_ref[h], preferred_element_type=F32)
                l_ref[h] = alpha[h] * l_ref[h] + pv1[LANES:LANES + SUBLANES, :]
                pv.append(pv1[:LANES, :])
            scale = jnp.concatenate([bcast(alpha[2 * p][0:1, :], HEAD_DIM), bcast(alpha[2 * p + 1][0:1, :], HEAD_DIM)],
                                    axis=0)
            acc_ref[p] = acc_ref[p] * scale + pv[0] + pv[1]
        return 0

    lax.fori_loop(0, nch, att_body, 0)
    for p in range(n_pairs):
        denom = jnp.concatenate([bcast(l_ref[2 * p, 0:1, :], HEAD_DIM), bcast(l_ref[2 * p + 1, 0:1, :], HEAD_DIM)],
                                axis=0)
        o_ref[:, p * LANES:(p + 1) * LANES] = (acc_ref[p] / denom).T.astype(o_ref.dtype)


def _dsa_prompt(kip, qit, wit, kp, qt, vtt, *, nb, seq, tq, topk, n_fast):
    n_q = qt.shape[1]
    spb = seq // tq
    per_b_rows = lambda a: pl.BlockSpec((seq, a.shape[1]), lambda b, i: (b, 0))
    q_cols = lambda a: pl.BlockSpec((None, a.shape[1], tq), lambda b, i: (b, 0, i))
    return pl.pallas_call(
        functools.partial(_dsa_prompt_kernel, tq=tq, topk=topk, n_fast=n_fast, rs=min(tq, 128)),
        out_shape=jax.ShapeDtypeStruct((nb * seq, n_q), BF16),
        grid=(nb, spb),
        in_specs=[per_b_rows(kip), q_cols(qit), q_cols(wit), per_b_rows(kp), q_cols(qt),
                  pl.BlockSpec((None, vtt.shape[1], seq), lambda b, i: (b, 0, 0))],
        out_specs=pl.BlockSpec((tq, n_q), lambda b, i: (b * spb + i, 0)),
        scratch_shapes=[pltpu.VMEM((seq, tq), F32), pltpu.VMEM((SUBLANES, tq), F32),
                        pltpu.VMEM((n_q // HEAD_DIM, SUBLANES, tq), F32),
                        pltpu.VMEM((n_q // HEAD_DIM, SUBLANES, tq), F32),
                        pltpu.VMEM((n_q // LANES, LANES, tq), F32),
                        pltpu.VMEM((n_q // HEAD_DIM, tq, tq), F32),
                        pltpu.VMEM((n_q // HEAD_DIM, tq, tq), BF16),
                        pltpu.VMEM((seq, tq), F32)],
        compiler_params=_cparams("parallel", "arbitrary"),
        name="dsa_prompt",
    )(kip, qit, wit, kp, qt, vtt)


def _retention_tables(chunk, n_heads):
    log_g = np.log1p(-np.exp2(-5.0 - np.arange(n_heads, dtype=np.float64)))
    i = np.arange(chunk, dtype=np.float64)
    diff = i[:, None] - i[None, :]
    decay = np.where(diff >= 0, np.exp(log_g[:, None, None] * np.maximum(diff, 0.0)), 0.0)
    q_dec = np.exp(log_g[:, None] * (i[None, :] + 1.0))
    k_dec = np.exp(log_g[:, None] * (chunk - 1.0 - i)[None, :])
    c_dec = np.exp(log_g * chunk)
    return decay, q_dec, k_dec, c_dec


def _group_norm_gate(o, rg, gain):
    mu = jnp.mean(o, axis=-1, keepdims=True)
    var = jnp.mean(jnp.square(o - mu), axis=-1, keepdims=True)
    on = ((o - mu) * lax.rsqrt(var + GN_EPS)) * gain
    return jax.nn.silu(rg) * on


def _retention_prompt_kernel(rq_ref, rkt_ref, rv_ref, rg_ref, gain_ref, dmat_ref, qdec_ref, kdec_ref, cdec_ref,
                             o_ref, st_ref, state_ref, *, chunk, n_heads):
    j = pl.program_id(1)

    @pl.when(j == 0)
    def _():
        state_ref[...] = jnp.zeros_like(state_ref)

    n_cc = rq_ref.shape[0] // chunk
    dot = functools.partial(jnp.dot, preferred_element_type=F32)
    tiles = [(cc, h, slice(cc * chunk, (cc + 1) * chunk), slice(h * RET_DV, (h + 1) * RET_DV))
             for cc in range(n_cc) for h in range(n_heads)]
    inner, update = {}, {}
    for cc, h, rows, cols in tiles:
        kt = rkt_ref[cols, rows]
        inner[cc, h] = (dot(rq_ref[rows, cols], kt) * dmat_ref[h]).astype(BF16)
        update[cc, h] = dot((kt.astype(F32) * kdec_ref[h]).astype(BF16), rv_ref[rows, cols])
    cross = {}
    for h in range(n_heads):
        state = state_ref[h]
        for cc in range(n_cc):
            rows, cols = slice(cc * chunk, (cc + 1) * chunk), slice(h * RET_DV, (h + 1) * RET_DV)
            cross[cc, h] = dot(rq_ref[rows, cols], state.astype(BF16)) * qdec_ref[h]
            state = cdec_ref[h] * state + update[cc, h]
        state_ref[h] = state
    for cc, h, rows, cols in tiles:
        o = dot(inner[cc, h], rv_ref[rows, cols]) + cross[cc, h]
        o_ref[rows, cols] = _group_norm_gate(o, rg_ref[rows, cols], gain_ref[:, cols]).astype(o_ref.dtype)

    @pl.when(j == pl.num_programs(1) - 1)
    def _():
        st_ref[...] = state_ref[...]


def _retention_prompt(rq, rkt, rv, rg, g_retn, *, nb, seq, tr):
    n_ret = rq.shape[1]
    n_heads = n_ret // RET_DV
    chunk = RET_CHUNK
    decay, q_dec, k_dec, c_dec = _retention_tables(chunk, n_heads)
    decay = jnp.asarray(decay, F32)
    qdec_b = jnp.asarray(np.broadcast_to(q_dec[:, :, None], (n_heads, chunk, LANES)), F32)
    kdec_b = jnp.asarray(k_dec[:, None, :], F32)
    cdec_b = jnp.asarray(np.broadcast_to(c_dec[:, None, None], (n_heads, 1, LANES)), F32)
    spb = seq // tr
    row = lambda: pl.BlockSpec((tr, n_ret), lambda b, j: (b * spb + j, 0))
    return pl.pallas_call(
        functools.partial(_retention_prompt_kernel, chunk=chunk, n_heads=n_heads),
        out_shape=(jax.ShapeDtypeStruct((nb * seq, n_ret), BF16),
                   jax.ShapeDtypeStruct((nb, n_heads, RET_DK, RET_DV), F32)),
        grid=(nb, spb),
        in_specs=[row(), pl.BlockSpec((None, n_ret, tr), lambda b, j: (b, 0, j)), row(), row(),
                  _const_spec((1, n_ret)), _const_spec((n_heads, chunk, chunk)),
                  _const_spec((n_heads, chunk, LANES)), _const_spec((n_heads, 1, chunk)),
                  _const_spec((n_heads, 1, LANES))],
        out_specs=(row(), pl.BlockSpec((None, n_heads, RET_DK, RET_DV), lambda b, j: (b, 0, 0, 0))),
        scratch_shapes=[pltpu.VMEM((n_heads, RET_DK, RET_DV), F32)],
        compiler_params=_cparams("parallel", "arbitrary"),
        name="retention_prompt",
    )(rq, rkt, rv, rg, g_retn.reshape(1, n_ret), decay, qdec_b, kdec_b, cdec_b)


def _rms(x, g):
    return (x * lax.rsqrt(jnp.mean(x * x, axis=-1, keepdims=True) + EPS)) * g


def _tail_kernel(h_ref, attn_ref, ret_ref, p_ref, wo_ref, g_ffn_ref, w_ffn_in_ref, w_down_ref,
                 g_ple_ref, w_pg_ref, w_pp_ref, g_fin_ref, y_ref):
    dot = functools.partial(jnp.dot, preferred_element_type=F32)
    n_attn = attn_ref.shape[1]
    d_ff = w_down_ref.shape[0]
    h = h_ref[...] + dot(attn_ref[...], wo_ref[:n_attn, :]) + dot(ret_ref[...], wo_ref[n_attn:, :])
    f = _rms(h, g_ffn_ref[...]).astype(BF16)
    act = (jax.nn.silu(dot(f, w_ffn_in_ref[:, :d_ff])) * dot(f, w_ffn_in_ref[:, d_ff:])).astype(BF16)
    h = h + dot(act, w_down_ref[...])
    gate = jax.nn.sigmoid(dot(_rms(h, g_ple_ref[...]).astype(BF16), w_pg_ref[...]))
    h = h + gate * dot(p_ref[...].astype(BF16), w_pp_ref[...])
    y_ref[...] = _rms(h, g_fin_ref[...])


def _tail(h2d, attn, ret, p2d, w_out, g_ffn, w_ffn_in, w_ffn_out, g_ple, w_ple_gate, w_ple_proj, g_final, *, tm):
    rows, d = h2d.shape
    n_attn = attn.shape[1]
    assert w_ffn_in.shape[1] == 2 * w_ffn_out.shape[0] and w_ffn_out.shape[0] % LANES == 0
    ws = [w_out.astype(BF16), g_ffn.reshape(1, d), w_ffn_in.astype(BF16), w_ffn_out.astype(BF16),
          g_ple.reshape(1, d), w_ple_gate.astype(BF16), w_ple_proj.astype(BF16), g_final.reshape(1, d)]
    row = lambda w: pl.BlockSpec((tm, w), lambda i: (i, 0))
    return pl.pallas_call(
        _tail_kernel,
        out_shape=jax.ShapeDtypeStruct((rows, d), F32),
        grid=(rows // tm,),
        in_specs=[row(d), row(n_attn), row(ret.shape[1]), row(p2d.shape[1])] + [_const_spec(w.shape) for w in ws],
        out_specs=row(d),
        compiler_params=_cparams("parallel"),
        name="tail",
    )(h2d, attn, ret, p2d, *ws)


def _page_copies(pools, bufs, sems, pt_ref, b, c, slot, pages_per_chunk):
    copies = []
    for pool, buf in zip(pools, bufs):
        for p in range(pages_per_chunk):
            page = pt_ref[b, c * pages_per_chunk + p]
            copies.append(pltpu.make_async_copy(pool.at[page], buf.at[slot, p], sems.at[slot]))
    return copies


def _chunk_keys_on_lanes(pages):
    return jnp.concatenate([pages[p] for p in range(pages.shape[0])], axis=1).astype(BF16)


def _stream_chunks(pools, bufs, sems, pt_ref, n_chunks, pages_per_chunk, compute):
    b = pl.program_id(0)
    nb = pl.num_programs(0)

    def start(bb, cc, slot):
        for cp in _page_copies(pools, bufs, sems, pt_ref, bb, cc, slot, pages_per_chunk):
            cp.start()

    @pl.when(b == 0)
    def _():
        start(0, 0, 0)

    first = 0 if n_chunks % 2 == 0 else b % 2
    for c in range(n_chunks):
        slot = (first + c) % 2
        if c + 1 < n_chunks:
            start(b, c + 1, 1 - slot)
        else:
            @pl.when(b + 1 < nb)
            def _():
                start(b + 1, 0, 1 - slot)
        for cp in _page_copies(pools, bufs, sems, pt_ref, b, c, slot, pages_per_chunk):
            cp.wait()
        compute(c, slot)


def _tile_lanes(x, width):
    return x if width == LANES else jnp.concatenate([x] * (width // LANES), axis=1)


def _nt_dot(a, b):
    return lax.dot_general(a, b, (((1,), (1,)), ((), ())), preferred_element_type=F32)


def _head_sum(x, n_t):
    return jnp.sum(x.reshape(n_t, IDX_HEADS, x.shape[-1]), axis=1)


def _sample_scores_kernel(pt_ref, qi_ref, wi_ref, kinew_ref, pool_ref, s_ref, buf, sems,
                          *, n_t, n_chunks, pages_per_chunk, pages_per_dot):
    ck = pages_per_chunk * PAGE_SIZE
    qi = qi_ref[...]
    w = wi_ref[...]

    def compute(c, slot):
        for p0 in range(0, pages_per_chunk, pages_per_dot):
            kit = _chunk_keys_on_lanes(buf[slot, p0:p0 + pages_per_dot])
            a = jnp.dot(qi, kit, preferred_element_type=F32)
            col0 = c * ck + p0 * PAGE_SIZE
            s_ref[:, col0:col0 + pages_per_dot * PAGE_SIZE] = _head_sum(jnp.maximum(a, 0.0) * w, n_t)

    _stream_chunks([pool_ref], [buf], sems, pt_ref, n_chunks, pages_per_chunk, compute)

    a = jnp.dot(qi, kinew_ref[...], preferred_element_type=F32)
    s_new = _head_sum(jnp.maximum(a, 0.0) * w, n_t)
    t_q = lax.broadcasted_iota(jnp.int32, s_new.shape, 0)
    t_k = lax.broadcasted_iota(jnp.int32, s_new.shape, 1)
    s_ref[:, n_chunks * ck:] = jnp.where(t_k <= t_q, s_new, NEG)


def _sample_scores(page_table, qi_s, wi_s, kinew, pool_ik, *, n_t, pages_per_chunk):
    db, n_pages = page_table.shape
    n_chunks = n_pages // pages_per_chunk
    past = n_pages * PAGE_SIZE
    rows = n_t * IDX_HEADS
    grid_spec = pltpu.PrefetchScalarGridSpec(
        num_scalar_prefetch=1,
        grid=(db,),
        in_specs=[pl.BlockSpec((None, rows, IDX_DIM), lambda b, pt: (b, 0, 0)),
                  pl.BlockSpec((None, rows, 1), lambda b, pt: (b, 0, 0)),
                  pl.BlockSpec((None, IDX_DIM, PAGE_SIZE), lambda b, pt: (b, 0, 0)),
                  pl.BlockSpec(memory_space=pl.ANY)],
        out_specs=pl.BlockSpec((None, n_t, past + PAGE_SIZE), lambda b, pt: (b, 0, 0)),
        scratch_shapes=[pltpu.VMEM((2, pages_per_chunk, IDX_DIM, PAGE_SIZE), F32),
                        pltpu.SemaphoreType.DMA((2,))],
    )
    return pl.pallas_call(
        functools.partial(_sample_scores_kernel, n_t=n_t, n_chunks=n_chunks, pages_per_chunk=pages_per_chunk,
                          pages_per_dot=min(16, pages_per_chunk)),
        out_shape=jax.ShapeDtypeStruct((db, n_t, past + PAGE_SIZE), F32),
        grid_spec=grid_spec,
        compiler_params=_cparams("arbitrary"),
        name="sample_scores",
    )(page_table, qi_s, wi_s, kinew, pool_ik)


def _sample_select_kernel(s_in_ref, s_out_ref, thr_ref, st_ref, thr8_ref, tie_ref, *, ck, topk, n_fast):
    nq, nkeys = s_in_ref.shape
    nch = nkeys // ck + 0 * pl.program_id(0)
    chunk = lambda c: pl.ds(pl.multiple_of(c * ck, ck), ck)

    def to_lanes(c, _):
        st_ref[chunk(c), :] = s_in_ref[:, chunk(c)].T
        return 0
    lax.fori_loop(0, nch, to_lanes, 0)

    mn, mx = _fold_keys(
        st_ref, nch, ck, (jnp.full((SUBLANES, nq), POS, F32), jnp.full((SUBLANES, nq), NEG, F32)),
        lambda cr, blk, _r: (jnp.minimum(cr[0], jnp.where(blk > 0.5 * NEG, blk, POS)), jnp.maximum(cr[1], blk)),
        lambda x, y: (jnp.minimum(x[0], y[0]), jnp.maximum(x[1], y[1])))
    _select_threshold(st_ref, thr8_ref, tie_ref, nch, ck, _all_sublanes(mn, jnp.minimum),
                      _all_sublanes(mx, jnp.maximum), topk, n_fast)
    def to_rows(c, _):
        s_out_ref[:, chunk(c)] = st_ref[chunk(c), :].T
        return 0
    lax.fori_loop(0, nch, to_rows, 0)
    thr_ref[...] = jnp.broadcast_to(thr8_ref[0:1, :], (LANES, nq)).T


def _sample_select(s2d, *, topk, n_fast):
    rows, ncols = s2d.shape
    ck = max(w for w in (LANES, 2 * LANES, 3 * LANES, 4 * LANES) if ncols % w == 0)
    full = pl.BlockSpec((rows, ncols), lambda i: (0, 0), pipeline_mode=pl.Buffered(1))
    return pl.pallas_call(
        functools.partial(_sample_select_kernel, ck=ck, topk=topk, n_fast=n_fast),
        out_shape=(jax.ShapeDtypeStruct((rows, ncols), F32), jax.ShapeDtypeStruct((rows, LANES), F32)),
        grid=(1,),
        in_specs=[full],
        out_specs=(full, pl.BlockSpec((rows, LANES), lambda i: (0, 0))),
        scratch_shapes=[pltpu.VMEM((ncols, rows), F32), pltpu.VMEM((SUBLANES, rows), F32),
                        pltpu.VMEM((ncols, rows), F32)],
        compiler_params=_cparams("arbitrary"),
        name="sample_select",
    )(s2d)


def _sample_attend_kernel(pt_ref, q_ref, s_ref, thr_ref, knew_ref, vnew_ref, poolk_ref, poolv_ref, o_ref,
                          kbuf, vbuf, sems, *, n_t, n_chunks, pages_per_chunk):
    ck = pages_per_chunk * PAGE_SIZE
    rows = q_ref.shape[1]
    reps = rows // n_t
    thr = jnp.concatenate([thr_ref[...]] * reps, axis=0)
    state = [[jnp.full((rows, 1), SOFTMAX_NEG, F32), jnp.zeros((rows, 1), F32), jnp.zeros((rows, HEAD_DIM), F32)]
             for _ in range(KV_HEADS)]

    def attend(kt, vt, s_blk):
        width = s_blk.shape[1]
        sel = jnp.concatenate([s_blk] * reps, axis=0) >= _tile_lanes(thr, width)
        lgs = [jnp.where(sel, jnp.dot(q_ref[g], kt[g], preferred_element_type=F32), SOFTMAX_NEG)
               for g in range(KV_HEADS)]
        stats = []
        for g in range(KV_HEADS):
            m, l, _ = state[g]
            m_new = jnp.maximum(m, jnp.max(lgs[g], axis=1, keepdims=True))
            alpha = jnp.exp2(m - m_new)
            pr = jnp.exp2(lgs[g] - m_new)
            stats.append((m_new, alpha, alpha * l + jnp.sum(pr, axis=1, keepdims=True), pr.astype(BF16)))
        for g in range(KV_HEADS):
            m_new, alpha, l_new, pr = stats[g]
            state[g] = [m_new, l_new, alpha * state[g][2] + _nt_dot(pr, vt[g])]

    def compute(c, slot):
        attend([_chunk_keys_on_lanes(kbuf[slot, :, g]) for g in range(KV_HEADS)],
               [_chunk_keys_on_lanes(vbuf[slot, :, g]) for g in range(KV_HEADS)], s_ref[:, c * ck:(c + 1) * ck])

    _stream_chunks([poolk_ref, poolv_ref], [kbuf, vbuf], sems, pt_ref, n_chunks, pages_per_chunk, compute)
    attend(knew_ref[...], vnew_ref[...], s_ref[:, n_chunks * ck:])
    for g in range(KV_HEADS):
        o_ref[g] = state[g][2] / state[g][1]


def _sample_attend(page_table, q_bd, s3d, thr3d, knew, vnew, pool_k, pool_v, *, n_t, pages_per_chunk):
    db, n_pages = page_table.shape
    n_chunks = n_pages // pages_per_chunk
    rows = q_bd.shape[2]
    ncols = s3d.shape[2]
    per_b = lambda r, w: pl.BlockSpec((None, r, w), lambda b, pt: (b, 0, 0))
    per_b4 = lambda r, w: pl.BlockSpec((None, KV_HEADS, r, w), lambda b, pt: (b, 0, 0, 0))
    grid_spec = pltpu.PrefetchScalarGridSpec(
        num_scalar_prefetch=1,
        grid=(db,),
        in_specs=[per_b4(rows, HEAD_DIM), per_b(n_t, ncols), per_b(n_t, LANES), per_b4(HEAD_DIM, PAGE_SIZE),
                  per_b4(HEAD_DIM, PAGE_SIZE), pl.BlockSpec(memory_space=pl.ANY), pl.BlockSpec(memory_space=pl.ANY)],
        out_specs=per_b4(rows, HEAD_DIM),
        scratch_shapes=[pltpu.VMEM((2, pages_per_chunk, KV_HEADS, HEAD_DIM, PAGE_SIZE), F32),
                        pltpu.VMEM((2, pages_per_chunk, KV_HEADS, HEAD_DIM, PAGE_SIZE), F32),
                        pltpu.SemaphoreType.DMA((2,))],
    )
    return pl.pallas_call(
        functools.partial(_sample_attend_kernel, n_t=n_t, n_chunks=n_chunks, pages_per_chunk=pages_per_chunk),
        out_shape=jax.ShapeDtypeStruct((db, KV_HEADS, rows, HEAD_DIM), F32),
        grid_spec=grid_spec,
        compiler_params=_cparams("arbitrary"),
        name="sample_attend",
    )(page_table, q_bd, s3d, thr3d, knew, vnew, pool_k, pool_v)


def _retention_sample_kernel(rq_ref, rkt_ref, rv_ref, rg_ref, gain_ref, dfull_ref, qdec_ref, kdec_ref, cdec_ref,
                             st_in_ref, o_ref, st_out_ref, *, n_t, n_heads):
    b = pl.program_id(0)

    @pl.when(b == 0)
    def _():
        o_ref[...] = jnp.zeros_like(o_ref)

    n_rows = rq_ref.shape[0]
    row = lax.broadcasted_iota(jnp.int32, (n_rows, RET_DV), 0)
    mine = (row >= b * n_t) & (row < (b + 1) * n_t)
    for h in range(n_heads):
        cols = slice(h * RET_DV, (h + 1) * RET_DV)
        q = jnp.where(mine, rq_ref[:, cols], 0.0).astype(BF16)
        v = rv_ref[:, cols]
        kt = rkt_ref[cols, :]
        state = st_in_ref[h]
        inner = jnp.dot(q, kt, preferred_element_type=F32) * dfull_ref[h]
        o = jnp.dot(inner.astype(BF16), v, preferred_element_type=F32)
        o = o + jnp.dot(q, state.astype(BF16), preferred_element_type=F32) * qdec_ref[h]
        vd = jnp.where(mine, v.astype(F32) * kdec_ref[h], 0.0).astype(BF16)
        st_out_ref[h] = cdec_ref[h] * state + jnp.dot(kt, vd, preferred_element_type=F32)
        gated = _group_norm_gate(o, rg_ref[:, cols], gain_ref[:, cols])
        o_ref[:, cols] = jnp.where(mine, gated, o_ref[:, cols].astype(F32)).astype(o_ref.dtype)


def _retention_sample(rq, rkt, rv, rg, g_retn, state, *, db, n_t):
    n_rows, n_ret = rq.shape
    n_heads = n_ret // RET_DV
    decay, q_dec, k_dec, c_dec = _retention_tables(n_t, n_heads)
    same = np.kron(np.eye(db), np.ones((n_t, n_t)))
    dfull = jnp.asarray(same[None] * np.tile(decay, (1, db, db)), F32)
    qdec_b = jnp.asarray(np.broadcast_to(np.tile(q_dec, (1, db))[:, :, None], (n_heads, n_rows, RET_DV)), F32)
    kdec_b = jnp.asarray(np.broadcast_to(np.tile(k_dec, (1, db))[:, :, None], (n_heads, n_rows, RET_DV)), F32)
    cdec_b = jnp.asarray(np.broadcast_to(c_dec[:, None, None], (n_heads, 1, RET_DV)), F32)
    const = lambda shape: pl.BlockSpec(shape, lambda b: (0,) * len(shape))
    st_spec = pl.BlockSpec((None, n_heads, RET_DK, RET_DV), lambda b: (b, 0, 0, 0))
    return pl.pallas_call(
        functools.partial(_retention_sample_kernel, n_t=n_t, n_heads=n_heads),
        out_shape=(jax.ShapeDtypeStruct((n_rows, n_ret), BF16),
                   jax.ShapeDtypeStruct((db, n_heads, RET_DK, RET_DV), F32)),
        grid=(db,),
        in_specs=[const((n_rows, n_ret)), const((n_ret, n_rows)), const((n_rows, n_ret)), const((n_rows, n_ret)),
                  const((1, n_ret)), const((n_heads, n_rows, n_rows)), const((n_heads, n_rows, RET_DV)),
                  const((n_heads, n_rows, RET_DV)), const((n_heads, 1, RET_DV)), st_spec],
        out_specs=(const((n_rows, n_ret)), st_spec),
        compiler_params=_cparams("arbitrary"),
        name="retention_sample",
    )(rq, rkt, rv, rg, g_retn.reshape(1, n_ret), dfull, qdec_b, kdec_b, cdec_b, state)


def _mixer_dims(w_in, d_model):
    n_q = d_model // 2
    n_ret = d_model // 2
    n_kv = KV_HEADS * HEAD_DIM
    n_qi = IDX_HEADS * IDX_DIM
    assert n_kv == LANES and 2 * n_kv + n_q + n_qi + IDX_DIM + IDX_HEADS + 4 * n_ret == w_in.shape[1]
    return n_q, n_kv, n_qi, n_ret


def _prompt_layer(x, p, weights, *, n_fast=24):
    g_mix, w_in, g_retn, w_out, g_ffn, w_ffn_in, w_ffn_out, g_ple, w_ple_gate, w_ple_proj, g_final = weights
    nb, seq, d = x.shape
    dims = _mixer_dims(w_in, d)
    w_packed = _pack_w_in(w_in, *dims)
    x2d = x.reshape(nb * seq, d)
    tm = min(512, seq)
    (qt, kt, vt, qit, kit, wit, rq, rkt, rv, rg, kp, vtt, kip) = _inproj(
        x2d, np.arange(seq), g_mix, w_packed, seq=seq, tm=tm, dims=dims)
    topk = min(TOPK_MAX, seq // 4)
    attn = _dsa_prompt(kip, qit, wit, kp, qt, vtt, nb=nb, seq=seq, tq=min(256, seq), topk=topk, n_fast=n_fast)
    ret, st = _retention_prompt(rq, rkt, rv, rg, g_retn, nb=nb, seq=seq, tr=min(512, seq))
    y = _tail(x2d, attn, ret, p.reshape(nb * seq, -1), w_out, g_ffn, w_ffn_in, w_ffn_out,
              g_ple, w_ple_gate, w_ple_proj, g_final, tm=min(256, nb * seq))
    heads_last = lambda a: a.reshape(nb, KV_HEADS, HEAD_DIM, seq).transpose(0, 3, 1, 2)
    return y.reshape(nb, seq, d), heads_last(kt), heads_last(vt), kit.transpose(0, 2, 1), st


def _sample_layer(x, p, pool_k, pool_v, pool_ik, state, page_table, weights, *, n_fast=24):
    g_mix, w_in, g_retn, w_out, g_ffn, w_ffn_in, w_ffn_out, g_ple, w_ple_gate, w_ple_proj, g_final = weights
    db, n_t, d = x.shape
    n_rows = db * n_t
    n_pages = page_table.shape[1]
    past = n_pages * PAGE_SIZE
    dims = _mixer_dims(w_in, d)
    n_q = dims[0]
    w_packed = _pack_w_in(w_in, *dims)
    x2d = x.reshape(n_rows, d)
    pos = np.tile(past + np.arange(n_t), db)
    (qt, kt, vt, qit, kit, wit, rq, rkt, rv, rg, _, _, _) = _inproj(
        x2d, pos, g_mix, w_packed, seq=n_rows, tm=n_rows, dims=dims)
    q, qi, wi = qt[0].T, qit[0].T, wit[0].T

    def new_pages(at, heads):
        a = at[0].reshape(heads, -1, db, n_t).transpose(2, 0, 1, 3)
        return jnp.pad(a, ((0, 0), (0, 0), (0, 0), (0, PAGE_SIZE - n_t))).astype(BF16)
    group = n_q // HEAD_DIM // KV_HEADS
    qg = q.reshape(db, n_t, KV_HEADS, group, HEAD_DIM).transpose(0, 2, 3, 1, 4)
    qg = qg.reshape(db, KV_HEADS, group * n_t, HEAD_DIM)
    pages_per_chunk = max(1, min(64, n_pages // 2))

    s3d = _sample_scores(page_table, qi.reshape(db, n_t * IDX_HEADS, IDX_DIM), wi.reshape(db, n_t * IDX_HEADS, 1),
                         new_pages(kit, 1)[:, 0], pool_ik.transpose(0, 2, 1), n_t=n_t, pages_per_chunk=n_pages)
    topk = min(TOPK_MAX, (past + n_t) // 4)
    s_sel, thr = _sample_select(s3d.reshape(n_rows, -1), topk=topk, n_fast=n_fast)
    o = _sample_attend(page_table, qg, s_sel.reshape(db, n_t, -1), thr.reshape(db, n_t, LANES),
                       new_pages(kt, KV_HEADS), new_pages(vt, KV_HEADS), pool_k.transpose(0, 2, 3, 1),
                       pool_v.transpose(0, 2, 3, 1), n_t=n_t, pages_per_chunk=pages_per_chunk)
    attn = o.reshape(db, KV_HEADS, group, n_t, HEAD_DIM).transpose(0, 3, 1, 2, 4).reshape(n_rows, n_q).astype(BF16)

    ret, st = _retention_sample(rq, rkt[0], rv, rg, g_retn, state, db=db, n_t=n_t)
    y = _tail(x2d, attn, ret, p.reshape(n_rows, -1), w_out, g_ffn, w_ffn_in, w_ffn_out,
              g_ple, w_ple_gate, w_ple_proj, g_final, tm=n_rows)
    heads_last = lambda a: a[0].reshape(KV_HEADS, HEAD_DIM, db, n_t).transpose(2, 3, 0, 1)
    return y.reshape(db, n_t, d), heads_last(kt), heads_last(vt), kit[0].T.reshape(db, n_t, IDX_DIM), st


def kernel(x_prompt, x_sample, cache_k, cache_v, cache_idx_k, state_retn, page_table, p_prompt, p_sample, g_mix, w_in,
           g_retn, w_out, g_ffn, w_ffn_in, w_ffn_out, g_ple, w_ple_gate, w_ple_proj, g_final):
    depth = w_in.shape[0]
    assert depth == 1, "the final RMSNorm is fused into the single layer's tail kernel"
    weights = (g_mix[0], w_in[0], g_retn[0], w_out[0], g_ffn[0], w_ffn_in[0], w_ffn_out[0], g_ple[0], w_ple_gate[0],
               w_ple_proj[0], g_final)
    y_p, k_p, v_p, ik_p, st_p = _prompt_layer(x_prompt, p_prompt[0], weights)
    y_s, k_s, v_s, ik_s, st_s = _sample_layer(x_sample, p_sample[0], cache_k[0], cache_v[0], cache_idx_k[0],
                                              state_retn[0], page_table, weights)
    return (y_p, y_s, k_p[None], v_p[None], ik_p[None], st_p[None], k_s[None], v_s[None], ik_s[None], st_s[None])
```

```python
import functools

import jax
import jax.numpy as jnp
import numpy as np
from jax import lax
from jax.experimental import pallas as pl
from jax.experimental.pallas import tpu as pltpu

F32 = jnp.float32
BF16 = jnp.bfloat16

HEAD_DIM = 64
KV_HEADS = 2
IDX_HEADS = 8
IDX_DIM = 64
TOPK_MAX = 256
RET_DK = 128
RET_DV = 128
RET_CHUNK = 128
PAGE_SIZE = 128
ROPE_THETA = 10000.0
EPS = 1e-6
GN_EPS = 1e-5

LANES = 128
SUBLANES = 8
VMEM_LIMIT_BYTES = 56 * 1024 * 1024

Q_SCALE = HEAD_DIM ** -0.5 * 1.4426950408889634
ONES_ROWS = 16

NEG = -3.0e38
POS = 3.0e38
SOFTMAX_NEG = -1.0e30
NO_TIE = 1.0e9


def _cparams(*sem):
    return pltpu.CompilerParams(dimension_semantics=sem, vmem_limit_bytes=VMEM_LIMIT_BYTES)


def _const_spec(shape):
    zeros = (0,) * len(shape)
    return pl.BlockSpec(shape, lambda *_: zeros, pipeline_mode=pl.Buffered(1))


def _rope_tables(pos, dim):
    half = dim // 2
    inv = ROPE_THETA ** (-np.arange(half, dtype=np.float64) / half)
    ang = np.asarray(pos, np.float64)[:, None] * inv[None, :]
    cos, sin = np.cos(ang), np.sin(ang)
    reps = LANES // dim
    cos_t = np.tile(np.concatenate([cos, cos], axis=1), (1, reps))
    sin_t = np.tile(np.concatenate([-sin, sin], axis=1), (1, reps))
    return jnp.asarray(cos_t, F32), jnp.asarray(sin_t, F32)


def _swap_halves_64(x):
    lane = lax.broadcasted_iota(jnp.int32, x.shape, 1)
    first = (lane % 64) < 32
    return jnp.where(first, pltpu.roll(x, 96, 1), pltpu.roll(x, 32, 1))


def _pad_pair_slabs(x, swapped, low):
    return (jnp.where(low, x, 0.0), jnp.where(low, 0.0, swapped), jnp.where(low, swapped, 0.0), jnp.where(low, 0.0, x))


def _inproj_kernel(x_ref, g_ref, w_ref, c64_ref, s64_ref, c128_ref, s128_ref,
                   qt_ref, kt_ref, vt_ref, qit_ref, kit_ref, wit_ref, rq_ref, rkt_ref, rv_ref, rg_ref,
                   kp_ref, vtt_ref, kip_ref, *, n_q, n_qi, n_ret):
    x = x_ref[...]
    tm = x.shape[0]
    ms = jnp.mean(x * x, axis=-1, keepdims=True)
    a = ((x * lax.rsqrt(ms + EPS)) * g_ref[...]).astype(BF16)
    c64, s64 = c64_ref[...], s64_ref[...]
    c128, s128 = c128_ref[...], s128_ref[...]
    low = lax.broadcasted_iota(jnp.int32, (tm, LANES), 1) < HEAD_DIM

    col = [0]

    def segment(width):
        y = jnp.dot(a, w_ref[:, col[0]:col[0] + width], preferred_element_type=F32)
        col[0] += width
        return [y[:, j * LANES:(j + 1) * LANES] for j in range(width // LANES)]

    def rope64(y):
        return y * c64 + _swap_halves_64(y) * s64

    def rope128(y):
        return y * c128 + pltpu.roll(y, 64, 1) * s128

    for j, y in enumerate(segment(n_q)):
        qt_ref[j * LANES:(j + 1) * LANES, :] = (rope64(y) * Q_SCALE).T.astype(BF16)
    kk, vv = segment(2 * LANES)
    kk = rope64(kk)
    kt_ref[...] = kk.T
    for s, slab in enumerate(_pad_pair_slabs(kk, pltpu.roll(kk, HEAD_DIM, 1), low)):
        kp_ref[:, s * LANES:(s + 1) * LANES] = slab.astype(BF16)
    vvt = vv.T
    vt_ref[...] = vvt
    vtt_ref[...] = vvt.astype(BF16)
    for j, y in enumerate(segment(n_qi)):
        qit_ref[j * LANES:(j + 1) * LANES, :] = (rope64(y) * (IDX_DIM ** -0.5)).T.astype(BF16)
    y, _ = segment(2 * LANES)
    ki = rope64(y)
    kit_ref[...] = ki.T[:IDX_DIM, :]
    ki_lo, ki_hi, _, _ = _pad_pair_slabs(ki, pltpu.roll(ki, IDX_DIM, 1), low)
    kip_ref[:, 0:LANES] = ki_lo.astype(BF16)
    kip_ref[:, LANES:2 * LANES] = ki_hi.astype(BF16)
    wit_ref[...] = y.T[IDX_DIM:IDX_DIM + IDX_HEADS, :] * (IDX_HEADS ** -0.5)
    for j, y in enumerate(segment(n_ret)):
        rq_ref[:, j * LANES:(j + 1) * LANES] = rope128(y).astype(BF16)
    for j, y in enumerate(segment(n_ret)):
        rkt_ref[j * LANES:(j + 1) * LANES, :] = (rope128(y) * (RET_DK ** -0.5)).T.astype(BF16)
    for j, y in enumerate(segment(n_ret)):
        rv_ref[:, j * LANES:(j + 1) * LANES] = y.astype(BF16)
    for j, y in enumerate(segment(n_ret)):
        rg_ref[:, j * LANES:(j + 1) * LANES] = y


def _pack_w_in(w_in, n_q, n_kv, n_qi, n_ret):
    d = w_in.shape[0]
    sizes = (n_q, n_kv, n_kv, n_qi, IDX_DIM, IDX_HEADS, n_ret, n_ret, n_ret, n_ret)
    offs = np.concatenate([[0], np.cumsum(sizes)])
    seg = [w_in[:, offs[i]:offs[i + 1]] for i in range(len(sizes))]
    pad = jnp.zeros((d, 2 * LANES - IDX_DIM - IDX_HEADS), w_in.dtype)
    packed = jnp.concatenate(seg[:4] + [seg[4], seg[5], pad] + seg[6:], axis=1)
    return packed.astype(BF16)


def _inproj(x2d, pos, g_mix, w_packed, *, seq, tm, dims):
    n_q, n_kv, n_qi, n_ret = dims
    rows, d = x2d.shape
    nb = rows // seq
    spt = seq // tm
    c64, s64 = _rope_tables(pos, HEAD_DIM)
    c128, s128 = _rope_tables(pos, RET_DK)
    wcols = w_packed.shape[1]

    row = lambda w: pl.BlockSpec((tm, w), lambda i: (i, 0))
    tab = pl.BlockSpec((tm, LANES), lambda i: (i % spt, 0))
    tr = lambda h: pl.BlockSpec((None, h, tm), lambda i: (i // spt, 0, i % spt))
    out_shape = (
        jax.ShapeDtypeStruct((nb, n_q, seq), BF16),
        jax.ShapeDtypeStruct((nb, n_kv, seq), F32),
        jax.ShapeDtypeStruct((nb, n_kv, seq), F32),
        jax.ShapeDtypeStruct((nb, n_qi, seq), BF16),
        jax.ShapeDtypeStruct((nb, IDX_DIM, seq), F32),
        jax.ShapeDtypeStruct((nb, IDX_HEADS, seq), F32),
        jax.ShapeDtypeStruct((rows, n_ret), BF16),
        jax.ShapeDtypeStruct((nb, n_ret, seq), BF16),
        jax.ShapeDtypeStruct((rows, n_ret), BF16),
        jax.ShapeDtypeStruct((rows, n_ret), F32),
        jax.ShapeDtypeStruct((rows, 4 * n_kv), BF16),
        jax.ShapeDtypeStruct((nb, n_kv, seq), BF16),
        jax.ShapeDtypeStruct((rows, 2 * LANES), BF16),
    )
    out_specs = (tr(n_q), tr(n_kv), tr(n_kv), tr(n_qi), tr(IDX_DIM), tr(IDX_HEADS),
                 row(n_ret), tr(n_ret), row(n_ret), row(n_ret), row(4 * n_kv), tr(n_kv), row(2 * LANES))
    return pl.pallas_call(
        functools.partial(_inproj_kernel, n_q=n_q, n_qi=n_qi, n_ret=n_ret),
        out_shape=out_shape,
        grid=(rows // tm,),
        in_specs=[row(d), _const_spec((1, d)), _const_spec((d, wcols)), tab, tab, tab, tab],
        out_specs=out_specs,
        compiler_params=_cparams("parallel"),
        name="inproj",
    )(x2d, g_mix.reshape(1, d), w_packed, c64, s64, c128, s128)


N_ACC = 4


def _fold_keys(st_ref, nch, ck, init, fn, combine):
    def body(c, carries):
        carries = list(carries)
        base = pl.multiple_of(c * ck, ck)
        chunk = st_ref[pl.ds(base, ck), :]
        for j in range(ck // SUBLANES):
            blk = chunk[j * SUBLANES:(j + 1) * SUBLANES, :]
            carries[j % N_ACC] = fn(carries[j % N_ACC], blk, base + j * SUBLANES)
        return tuple(carries)
    carries = lax.fori_loop(0, nch, body, (init,) * N_ACC)
    out = carries[0]
    for other in carries[1:]:
        out = combine(out, other)
    return out


def _all_sublanes(x, op):
    for shift in (4, 2, 1):
        x = op(x, pltpu.roll(x, shift, 0))
    return x


def _any(mask):
    return jnp.max(jnp.where(mask, 1.0, 0.0)) > 0.5


def _count(st_ref, nch, ck, pred):
    nq = st_ref.shape[1]
    acc = _fold_keys(st_ref, nch, ck, jnp.zeros((SUBLANES, nq), F32),
                     lambda a, blk, row0: a + jnp.where(pred(blk, row0), 1.0, 0.0), jnp.add)
    return _all_sublanes(acc, jnp.add)


def _select_threshold(st_ref, thr_ref, tie_ref, nch, ck, rmin, rmax, topk, n_fast):
    nq = st_ref.shape[1]
    kf = float(topk)
    full = lambda v: jnp.full((SUBLANES, nq), v, F32)

    def fast_body(_, carry):
        lo, hi = carry
        mid = lo + (hi - lo) * 0.5
        up = _count(st_ref, nch, ck, lambda blk, _r: blk > mid) >= kf
        return jnp.where(up, mid, lo), jnp.where(up, hi, mid)

    lo, hi = lax.fori_loop(0, n_fast, fast_body, (rmin, rmax))
    unres = _count(st_ref, nch, ck, lambda blk, _r: blk >= lo) > kf
    thr_ref[...] = lo

    @pl.when(_any(unres))
    def _exact():
        pair_minmax = lambda x, y: (jnp.minimum(x[0], y[0]), jnp.maximum(x[1], y[1]))
        a, b = _fold_keys(
            st_ref, nch, ck, (full(POS), full(NEG)),
            lambda cr, blk, _r: (jnp.minimum(cr[0], jnp.where(blk >= lo, blk, POS)),
                                 jnp.maximum(cr[1], jnp.where(blk <= hi, blk, NEG))),
            pair_minmax)
        lo2 = jnp.where(unres, _all_sublanes(a, jnp.minimum), lo)
        ub = jnp.where(unres, _all_sublanes(b, jnp.maximum), lo)

        def body(carry):
            lo2, ub, _ = carry
            mid = lo2 + (ub - lo2) * 0.5
            mid = jnp.where(mid >= ub, lo2, mid)

            def step(cr, blk, _r):
                gt = blk > mid
                return (cr[0] + jnp.where(gt, 1.0, 0.0), jnp.minimum(cr[1], jnp.where(gt, blk, POS)),
                        jnp.maximum(cr[2], jnp.where(gt, NEG, blk)))
            cnt, a, b = _fold_keys(st_ref, nch, ck, (full(0.0), full(POS), full(NEG)), step,
                                   lambda x, y: (x[0] + y[0],) + pair_minmax(x[1:], y[1:]))
            up = _all_sublanes(cnt, jnp.add) >= kf
            active = lo2 < ub
            lo2n = jnp.where(active & up, _all_sublanes(a, jnp.minimum), lo2)
            ubn = jnp.where(active & jnp.logical_not(up), _all_sublanes(b, jnp.maximum), ub)
            return lo2n, ubn, _any(lo2n < ubn).astype(jnp.int32)

        v, _, _ = lax.while_loop(lambda carry: carry[2] > 0, body, (lo2, ub, _any(lo2 < ub).astype(jnp.int32)))
        thr_ref[...] = jnp.where(unres, v, lo)
        n_ge, n_gt = _fold_keys(
            st_ref, nch, ck, (full(0.0), full(0.0)),
            lambda cr, blk, _r: (cr[0] + jnp.where(blk >= v, 1.0, 0.0), cr[1] + jnp.where(blk > v, 1.0, 0.0)),
            lambda x, y: (x[0] + y[0], x[1] + y[1]))
        tied = unres & (_all_sublanes(n_ge, jnp.add) > kf)

        @pl.when(_any(tied))
        def _ties():
            _drop_excess_ties(st_ref, tie_ref, nch, ck, v, tied, kf - _all_sublanes(n_gt, jnp.add))


def _drop_excess_ties(st_ref, tie_ref, nch, ck, v, tied, need):
    nq = st_ref.shape[1]
    full = lambda x: jnp.full((SUBLANES, nq), x, F32)
    nkeys = st_ref.shape[0]
    key_in_chunk = lax.broadcasted_iota(jnp.int32, (ck, nq), 0)
    rep = lambda x: jnp.broadcast_to(x[0:1, :], (ck, nq))
    v_c = rep(v)

    def each_chunk(fn):
        def body(c, _):
            base = pl.multiple_of(c * ck, ck)
            fn(pl.ds(base, ck), base)
            return 0
        lax.fori_loop(0, nch, body, 0)

    def mark(rows, row0):
        tie_ref[rows, :] = jnp.where(st_ref[rows, :] == v_c, (key_in_chunk + row0).astype(F32), NO_TIE)
    each_chunk(mark)

    n_j = int(np.ceil(np.log2(nkeys))) + 1
    need_t = jnp.where(tied, need, 0.0)
    need_max = jnp.max(need_t)

    def by_walk():
        def body(carry):
            j_prev, k = carry
            nxt = _fold_keys(tie_ref, nch, ck, full(NO_TIE),
                             lambda a, blk, _r: jnp.minimum(a, jnp.where(blk > j_prev, blk, NO_TIE)), jnp.minimum)
            return jnp.where(need_t > k, _all_sublanes(nxt, jnp.minimum), j_prev), k + 1.0
        return lax.while_loop(lambda carry: carry[1] < need_max, body, (full(-1.0), jnp.float32(0.0)))[0]

    def by_bisection():
        def body(_, carry):
            lo_j, hi_j = carry
            mid_j = jnp.floor((lo_j + hi_j) * 0.5)
            ok = _count(tie_ref, nch, ck, lambda blk, _r: blk <= mid_j) >= need
            return jnp.where(ok, lo_j, mid_j), jnp.where(ok, mid_j, hi_j)
        return lax.fori_loop(0, n_j, body, (full(-1.0), full(float(nkeys - 1))))[1]

    j_last = lax.cond(need_max <= float(n_j), by_walk, by_bisection)

    j_c = rep(jnp.where(tied, j_last, NO_TIE))

    def drop(rows, _row0):
        t = tie_ref[rows, :]
        s = st_ref[rows, :]
        st_ref[rows, :] = jnp.where(t > j_c, jnp.where(t < NO_TIE, NEG, s), s)
    each_chunk(drop)


def _rows_min(x):
    return jnp.min(x.reshape(x.shape[0] // SUBLANES, SUBLANES, x.shape[1]), axis=0)


def _rows_max(x):
    return jnp.max(x.reshape(x.shape[0] // SUBLANES, SUBLANES, x.shape[1]), axis=0)


def _rows_sum(x):
    return jnp.sum(x.reshape(x.shape[0] // SUBLANES, SUBLANES, x.shape[1]), axis=0)


def _dsa_prompt_kernel(kip_ref, qit_ref, wit_ref, kp_ref, qt_ref, vtt_ref, o_ref, st_ref, thr_ref,
                       m_ref, l_ref, acc_ref, lg_ref, pr_ref, tie_ref, *, tq, topk, n_fast, rs):
    ck = tq
    i = pl.program_id(1)
    nch = i + 1
    n_pairs = qt_ref.shape[0] // LANES
    pairs_per_kv = n_pairs // KV_HEADS
    bcast = lambda row, n: jnp.broadcast_to(row, (n, tq))

    key_l = lax.broadcasted_iota(jnp.int32, (rs, tq), 0)
    qry_l = lax.broadcasted_iota(jnp.int32, (rs, tq), 1)
    w_rows = [bcast(wit_ref[h:h + 1, :], rs) for h in range(IDX_HEADS)]

    def score_body(c, carry):
        mn, mx = carry
        base = pl.multiple_of(c * ck, ck)
        slack = jnp.where(c < i, ck, 0)
        for r in range(ck // rs):
            rows = pl.ds(base + r * rs, rs)
            k_lo = kip_ref[rows, 0:LANES]
            k_hi = kip_ref[rows, LANES:2 * LANES]
            acc = jnp.zeros((rs, tq), F32)
            for p in range(IDX_HEADS // 2):
                rhs = qit_ref[p * LANES:(p + 1) * LANES, :]
                acc = acc + jnp.maximum(jnp.dot(k_lo, rhs, preferred_element_type=F32), 0.0) * w_rows[2 * p]
                acc = acc + jnp.maximum(jnp.dot(k_hi, rhs, preferred_element_type=F32), 0.0) * w_rows[2 * p + 1]
            valid = key_l + r * rs <= qry_l + slack
            st_ref[rows, :] = jnp.where(valid, acc, NEG)
            mn = jnp.minimum(mn, _rows_min(jnp.where(valid, acc, POS)))
            mx = jnp.maximum(mx, _rows_max(jnp.where(valid, acc, NEG)))
        return mn, mx

    mn, mx = lax.fori_loop(0, nch, score_body,
                           (jnp.full((SUBLANES, tq), POS, F32), jnp.full((SUBLANES, tq), NEG, F32)))
    _select_threshold(st_ref, thr_ref, tie_ref, nch, ck,
                      _all_sublanes(mn, jnp.minimum), _all_sublanes(mx, jnp.maximum), topk, n_fast)

    thr = bcast(thr_ref[0:1, :], ck)
    m_ref[...] = jnp.full(m_ref.shape, SOFTMAX_NEG, F32)
    l_ref[...] = jnp.zeros(l_ref.shape, F32)
    acc_ref[...] = jnp.zeros(acc_ref.shape, F32)

    def att_body(c, _):
        rows = pl.ds(pl.multiple_of(c * ck, ck), ck)
        sel = st_ref[rows, :] >= thr
        slab_of = lambda h: 2 * (h // 2 // pairs_per_kv) + h % 2
        n_heads = 2 * n_pairs
        for h in range(n_heads):
            slab = slab_of(h)
            kx = kp_ref[rows, slab * LANES:(slab + 1) * LANES]
            qt_pair = qt_ref[(h // 2) * LANES:(h // 2 + 1) * LANES, :]
            lg_ref[h] = jnp.where(sel, jnp.dot(kx, qt_pair, preferred_element_type=F32), SOFTMAX_NEG)
        alpha = []
        for h in range(n_heads):
            lg = lg_ref[h]
            m_old = m_ref[h]
            m_new = jnp.maximum(m_old, _all_sublanes(_rows_max(lg), jnp.maximum))
            alpha.append(jnp.exp2(m_old - m_new))
            m_ref[h] = m_new
            pr_ref[h] = jnp.exp2(lg - bcast(m_new[0:1, :], ck)).astype(BF16)
        ones = jnp.ones((ONES_ROWS, ck), BF16)
        for p in range(n_pairs):
            pv = []
            g = p // pairs_per_kv
            vt1 = jnp.concatenate([vtt_ref[g * HEAD_DIM:(g + 1) * HEAD_DIM, rows], ones], axis=0)
            for h in (2 * p, 2 * p + 1):
                pv1 = jnp.dot(vt1, pr_ref[h], preferred_element_type=F32)
                l_ref[h] = alpha[h] * l_ref[h] + pv1[HEAD_DIM:HEAD_DIM + SUBLANES, :]
                pv.append(pv1[:HEAD_DIM, :])
            scale = jnp.concatenate([bcast(alpha[2 * p][0:1, :], HEAD_DIM), bcast(alpha[2 * p + 1][0:1, :], HEAD_DIM)],
                                    axis=0)
            acc_ref[p] = acc_ref[p] * scale + jnp.concatenate(pv, axis=0)
        return 0

    lax.fori_loop(0, nch, att_body, 0)
    for p in range(n_pairs):
        denom = jnp.concatenate([bcast(l_ref[2 * p, 0:1, :], HEAD_DIM), bcast(l_ref[2 * p + 1, 0:1, :], HEAD_DIM)],
                                axis=0)
        o_ref[:, p * LANES:(p + 1) * LANES] = (acc_ref[p] / denom).T.astype(o_ref.dtype)


def _dsa_prompt(kip, qit, wit, kp, qt, vtt, *, nb, seq, tq, topk, n_fast):
    n_q = qt.shape[1]
    spb = seq // tq
    per_b_rows = lambda a: pl.BlockSpec((seq, a.shape[1]), lambda b, i: (b, 0))
    q_cols = lambda a: pl.BlockSpec((None, a.shape[1], tq), lambda b, i: (b, 0, i))
    return pl.pallas_call(
        functools.partial(_dsa_prompt_kernel, tq=tq, topk=topk, n_fast=n_fast, rs=min(tq, 128)),
        out_shape=jax.ShapeDtypeStruct((nb * seq, n_q), BF16),
        grid=(nb, spb),
        in_specs=[per_b_rows(kip), q_cols(qit), q_cols(wit), per_b_rows(kp), q_cols(qt),
                  pl.BlockSpec((None, vtt.shape[1], seq), lambda b, i: (b, 0, 0))],
        out_specs=pl.BlockSpec((tq, n_q), lambda b, i: (b * spb + i, 0)),
        scratch_shapes=[pltpu.VMEM((seq, tq), F32), pltpu.VMEM((SUBLANES, tq), F32),
                        pltpu.VMEM((n_q // HEAD_DIM, SUBLANES, tq), F32),
                        pltpu.VMEM((n_q // HEAD_DIM, SUBLANES, tq), F32),
                        pltpu.VMEM((n_q // LANES, LANES, tq), F32),
                        pltpu.VMEM((n_q // HEAD_DIM, tq, tq), F32),
                        pltpu.VMEM((n_q // HEAD_DIM, tq, tq), BF16),
                        pltpu.VMEM((seq, tq), F32)],
        compiler_params=_cparams("parallel", "arbitrary"),
        name="dsa_prompt",
    )(kip, qit, wit, kp, qt, vtt)


def _retention_tables(chunk, n_heads):
    log_g = np.log1p(-np.exp2(-5.0 - np.arange(n_heads, dtype=np.float64)))
    i = np.arange(chunk, dtype=np.float64)
    diff = i[:, None] - i[None, :]
    decay = np.where(diff >= 0, np.exp(log_g[:, None, None] * np.maximum(diff, 0.0)), 0.0)
    q_dec = np.exp(log_g[:, None] * (i[None, :] + 1.0))
    k_dec = np.exp(log_g[:, None] * (chunk - 1.0 - i)[None, :])
    c_dec = np.exp(log_g * chunk)
    return decay, q_dec, k_dec, c_dec


def _group_norm_gate(o, rg, gain):
    mu = jnp.mean(o, axis=-1, keepdims=True)
    var = jnp.mean(jnp.square(o - mu), axis=-1, keepdims=True)
    on = ((o - mu) * lax.rsqrt(var + GN_EPS)) * gain
    return jax.nn.silu(rg) * on


def _retention_prompt_kernel(rq_ref, rkt_ref, rv_ref, rg_ref, gain_ref, dmat_ref, qdec_ref, kdec_ref, cdec_ref,
                             o_ref, st_ref, state_ref, *, chunk, n_heads):
    j = pl.program_id(1)

    @pl.when(j == 0)
    def _():
        state_ref[...] = jnp.zeros_like(state_ref)

    n_cc = rq_ref.shape[0] // chunk
    dot = functools.partial(jnp.dot, preferred_element_type=F32)
    tiles = [(cc, h, slice(cc * chunk, (cc + 1) * chunk), slice(h * RET_DV, (h + 1) * RET_DV))
             for cc in range(n_cc) for h in range(n_heads)]
    inner, update = {}, {}
    for cc, h, rows, cols in tiles:
        kt = rkt_ref[cols, rows]
        inner[cc, h] = (dot(rq_ref[rows, cols], kt) * dmat_ref[h]).astype(BF16)
        update[cc, h] = dot((kt.astype(F32) * kdec_ref[h]).astype(BF16), rv_ref[rows, cols])
    cross = {}
    for h in range(n_heads):
        state = state_ref[h]
        for cc in range(n_cc):
            rows, cols = slice(cc * chunk, (cc + 1) * chunk), slice(h * RET_DV, (h + 1) * RET_DV)
            cross[cc, h] = dot(rq_ref[rows, cols], state.astype(BF16)) * qdec_ref[h]
            state = cdec_ref[h] * state + update[cc, h]
        state_ref[h] = state
    for cc, h, rows, cols in tiles:
        o = dot(inner[cc, h], rv_ref[rows, cols]) + cross[cc, h]
        o_ref[rows, cols] = _group_norm_gate(o, rg_ref[rows, cols], gain_ref[:, cols]).astype(o_ref.dtype)

    @pl.when(j == pl.num_programs(1) - 1)
    def _():
        st_ref[...] = state_ref[...]


def _retention_prompt(rq, rkt, rv, rg, g_retn, *, nb, seq, tr):
    n_ret = rq.shape[1]
    n_heads = n_ret // RET_DV
    chunk = RET_CHUNK
    decay, q_dec, k_dec, c_dec = _retention_tables(chunk, n_heads)
    decay = jnp.asarray(decay, F32)
    qdec_b = jnp.asarray(np.broadcast_to(q_dec[:, :, None], (n_heads, chunk, LANES)), F32)
    kdec_b = jnp.asarray(k_dec[:, None, :], F32)
    cdec_b = jnp.asarray(np.broadcast_to(c_dec[:, None, None], (n_heads, 1, LANES)), F32)
    spb = seq // tr
    row = lambda: pl.BlockSpec((tr, n_ret), lambda b, j: (b * spb + j, 0))
    return pl.pallas_call(
        functools.partial(_retention_prompt_kernel, chunk=chunk, n_heads=n_heads),
        out_shape=(jax.ShapeDtypeStruct((nb * seq, n_ret), BF16),
                   jax.ShapeDtypeStruct((nb, n_heads, RET_DK, RET_DV), F32)),
        grid=(nb, spb),
        in_specs=[row(), pl.BlockSpec((None, n_ret, tr), lambda b, j: (b, 0, j)), row(), row(),
                  _const_spec((1, n_ret)), _const_spec((n_heads, chunk, chunk)),
                  _const_spec((n_heads, chunk, LANES)), _const_spec((n_heads, 1, chunk)),
                  _const_spec((n_heads, 1, LANES))],
        out_specs=(row(), pl.BlockSpec((None, n_heads, RET_DK, RET_DV), lambda b, j: (b, 0, 0, 0))),
        scratch_shapes=[pltpu.VMEM((n_heads, RET_DK, RET_DV), F32)],
        compiler_params=_cparams("parallel", "arbitrary"),
        name="retention_prompt",
    )(rq, rkt, rv, rg, g_retn.reshape(1, n_ret), decay, qdec_b, kdec_b, cdec_b)


def _rms(x, g):
    return (x * lax.rsqrt(jnp.mean(x * x, axis=-1, keepdims=True) + EPS)) * g


def _tail_kernel(h_ref, attn_ref, ret_ref, p_ref, wo_ref, g_ffn_ref, w_ffn_in_ref, w_down_ref,
                 g_ple_ref, w_pg_ref, w_pp_ref, g_fin_ref, y_ref):
    dot = functools.partial(jnp.dot, preferred_element_type=F32)
    n_attn = attn_ref.shape[1]
    d_ff = w_down_ref.shape[0]
    h = h_ref[...] + dot(attn_ref[...], wo_ref[:n_attn, :]) + dot(ret_ref[...], wo_ref[n_attn:, :])
    f = _rms(h, g_ffn_ref[...]).astype(BF16)
    act = (jax.nn.silu(dot(f, w_ffn_in_ref[:, :d_ff])) * dot(f, w_ffn_in_ref[:, d_ff:])).astype(BF16)
    h = h + dot(act, w_down_ref[...])
    gate = jax.nn.sigmoid(dot(_rms(h, g_ple_ref[...]).astype(BF16), w_pg_ref[...]))
    h = h + gate * dot(p_ref[...].astype(BF16), w_pp_ref[...])
    y_ref[...] = _rms(h, g_fin_ref[...])


def _tail(h2d, attn, ret, p2d, w_out, g_ffn, w_ffn_in, w_ffn_out, g_ple, w_ple_gate, w_ple_proj, g_final, *, tm):
    rows, d = h2d.shape
    n_attn = attn.shape[1]
    assert w_ffn_in.shape[1] == 2 * w_ffn_out.shape[0] and w_ffn_out.shape[0] % LANES == 0
    ws = [w_out.astype(BF16), g_ffn.reshape(1, d), w_ffn_in.astype(BF16), w_ffn_out.astype(BF16),
          g_ple.reshape(1, d), w_ple_gate.astype(BF16), w_ple_proj.astype(BF16), g_final.reshape(1, d)]
    row = lambda w: pl.BlockSpec((tm, w), lambda i: (i, 0))
    return pl.pallas_call(
        _tail_kernel,
        out_shape=jax.ShapeDtypeStruct((rows, d), F32),
        grid=(rows // tm,),
        in_specs=[row(d), row(n_attn), row(ret.shape[1]), row(p2d.shape[1])] + [_const_spec(w.shape) for w in ws],
        out_specs=row(d),
        compiler_params=_cparams("parallel"),
        name="tail",
    )(h2d, attn, ret, p2d, *ws)


def _page_copies(pools, bufs, sems, pt_ref, b, c, slot, pages_per_chunk):
    copies = []
    for pool, buf in zip(pools, bufs):
        for p in range(pages_per_chunk):
            page = pt_ref[b, c * pages_per_chunk + p]
            copies.append(pltpu.make_async_copy(pool.at[page], buf.at[slot, p], sems.at[slot]))
    return copies


def _chunk_keys_on_lanes(pages):
    return jnp.concatenate([pages[p] for p in range(pages.shape[0])], axis=1).astype(BF16)


def _stream_chunks(pools, bufs, sems, pt_ref, n_chunks, pages_per_chunk, compute):
    b = pl.program_id(0)
    nb = pl.num_programs(0)

    def start(bb, cc, slot):
        for cp in _page_copies(pools, bufs, sems, pt_ref, bb, cc, slot, pages_per_chunk):
            cp.start()

    @pl.when(b == 0)
    def _():
        start(0, 0, 0)

    first = 0 if n_chunks % 2 == 0 else b % 2
    for c in range(n_chunks):
        slot = (first + c) % 2
        if c + 1 < n_chunks:
            start(b, c + 1, 1 - slot)
        else:
            @pl.when(b + 1 < nb)
            def _():
                start(b + 1, 0, 1 - slot)
        for cp in _page_copies(pools, bufs, sems, pt_ref, b, c, slot, pages_per_chunk):
            cp.wait()
        compute(c, slot)


def _tile_lanes(x, width):
    return x if width == LANES else jnp.concatenate([x] * (width // LANES), axis=1)


def _nt_dot(a, b):
    return lax.dot_general(a, b, (((1,), (1,)), ((), ())), preferred_element_type=F32)


def _head_sum(x, n_t):
    return jnp.sum(x.reshape(n_t, IDX_HEADS, x.shape[-1]), axis=1)


def _sample_scores_kernel(pt_ref, qi_ref, wi_ref, kinew_ref, pool_ref, s_ref, buf, sems,
                          *, n_t, n_chunks, pages_per_chunk, pages_per_dot):
    ck = pages_per_chunk * PAGE_SIZE
    qi = qi_ref[...]
    w = wi_ref[...]

    def compute(c, slot):
        for p0 in range(0, pages_per_chunk, pages_per_dot):
            kit = _chunk_keys_on_lanes(buf[slot, p0:p0 + pages_per_dot])
            a = jnp.dot(qi, kit, preferred_element_type=F32)
            col0 = c * ck + p0 * PAGE_SIZE
            s_ref[:, col0:col0 + pages_per_dot * PAGE_SIZE] = _head_sum(jnp.maximum(a, 0.0) * w, n_t)

    _stream_chunks([pool_ref], [buf], sems, pt_ref, n_chunks, pages_per_chunk, compute)

    a = jnp.dot(qi, kinew_ref[...], preferred_element_type=F32)
    s_new = _head_sum(jnp.maximum(a, 0.0) * w, n_t)
    t_q = lax.broadcasted_iota(jnp.int32, s_new.shape, 0)
    t_k = lax.broadcasted_iota(jnp.int32, s_new.shape, 1)
    s_ref[:, n_chunks * ck:] = jnp.where(t_k <= t_q, s_new, NEG)


def _sample_scores(page_table, qi_s, wi_s, kinew, pool_ik, *, n_t, pages_per_chunk):
    db, n_pages = page_table.shape
    n_chunks = n_pages // pages_per_chunk
    past = n_pages * PAGE_SIZE
    rows = n_t * IDX_HEADS
    grid_spec = pltpu.PrefetchScalarGridSpec(
        num_scalar_prefetch=1,
        grid=(db,),
        in_specs=[pl.BlockSpec((None, rows, IDX_DIM), lambda b, pt: (b, 0, 0)),
                  pl.BlockSpec((None, rows, 1), lambda b, pt: (b, 0, 0)),
                  pl.BlockSpec((None, IDX_DIM, PAGE_SIZE), lambda b, pt: (b, 0, 0)),
                  pl.BlockSpec(memory_space=pl.ANY)],
        out_specs=pl.BlockSpec((None, n_t, past + PAGE_SIZE), lambda b, pt: (b, 0, 0)),
        scratch_shapes=[pltpu.VMEM((2, pages_per_chunk, IDX_DIM, PAGE_SIZE), F32),
                        pltpu.SemaphoreType.DMA((2,))],
    )
    return pl.pallas_call(
        functools.partial(_sample_scores_kernel, n_t=n_t, n_chunks=n_chunks, pages_per_chunk=pages_per_chunk,
                          pages_per_dot=min(16, pages_per_chunk)),
        out_shape=jax.ShapeDtypeStruct((db, n_t, past + PAGE_SIZE), F32),
        grid_spec=grid_spec,
        compiler_params=_cparams("arbitrary"),
        name="sample_scores",
    )(page_table, qi_s, wi_s, kinew, pool_ik)


def _sample_select_kernel(s_in_ref, s_out_ref, thr_ref, st_ref, thr8_ref, tie_ref, *, ck, topk, n_fast):
    nq, nkeys = s_in_ref.shape
    nch = nkeys // ck + 0 * pl.program_id(0)
    chunk = lambda c: pl.ds(pl.multiple_of(c * ck, ck), ck)

    def to_lanes(c, _):
        st_ref[chunk(c), :] = s_in_ref[:, chunk(c)].T
        return 0
    lax.fori_loop(0, nch, to_lanes, 0)

    mn, mx = _fold_keys(
        st_ref, nch, ck, (jnp.full((SUBLANES, nq), POS, F32), jnp.full((SUBLANES, nq), NEG, F32)),
        lambda cr, blk, _r: (jnp.minimum(cr[0], jnp.where(blk > 0.5 * NEG, blk, POS)), jnp.maximum(cr[1], blk)),
        lambda x, y: (jnp.minimum(x[0], y[0]), jnp.maximum(x[1], y[1])))
    _select_threshold(st_ref, thr8_ref, tie_ref, nch, ck, _all_sublanes(mn, jnp.minimum),
                      _all_sublanes(mx, jnp.maximum), topk, n_fast)
    def to_rows(c, _):
        s_out_ref[:, chunk(c)] = st_ref[chunk(c), :].T
        return 0
    lax.fori_loop(0, nch, to_rows, 0)
    thr_ref[...] = jnp.broadcast_to(thr8_ref[0:1, :], (LANES, nq)).T


def _sample_select(s2d, *, topk, n_fast):
    rows, ncols = s2d.shape
    ck = max(w for w in (LANES, 2 * LANES, 3 * LANES, 4 * LANES) if ncols % w == 0)
    full = pl.BlockSpec((rows, ncols), lambda i: (0, 0), pipeline_mode=pl.Buffered(1))
    return pl.pallas_call(
        functools.partial(_sample_select_kernel, ck=ck, topk=topk, n_fast=n_fast),
        out_shape=(jax.ShapeDtypeStruct((rows, ncols), F32), jax.ShapeDtypeStruct((rows, LANES), F32)),
        grid=(1,),
        in_specs=[full],
        out_specs=(full, pl.BlockSpec((rows, LANES), lambda i: (0, 0))),
        scratch_shapes=[pltpu.VMEM((ncols, rows), F32), pltpu.VMEM((SUBLANES, rows), F32),
                        pltpu.VMEM((ncols, rows), F32)],
        compiler_params=_cparams("arbitrary"),
        name="sample_select",
    )(s2d)


def _sample_attend_kernel(pt_ref, q_ref, s_ref, thr_ref, knew_ref, vnew_ref, poolk_ref, poolv_ref, o_ref,
                          kbuf, vbuf, sems, *, n_t, n_chunks, pages_per_chunk):
    ck = pages_per_chunk * PAGE_SIZE
    rows = q_ref.shape[1]
    reps = rows // n_t
    thr = jnp.concatenate([thr_ref[...]] * reps, axis=0)
    state = [[jnp.full((rows, 1), SOFTMAX_NEG, F32), jnp.zeros((rows, 1), F32), jnp.zeros((rows, HEAD_DIM), F32)]
             for _ in range(KV_HEADS)]

    def attend(kt, vt, s_blk):
        width = s_blk.shape[1]
        sel = jnp.concatenate([s_blk] * reps, axis=0) >= _tile_lanes(thr, width)
        lgs = [jnp.where(sel, jnp.dot(q_ref[g], kt[g], preferred_element_type=F32), SOFTMAX_NEG)
               for g in range(KV_HEADS)]
        stats = []
        for g in range(KV_HEADS):
            m, l, _ = state[g]
            m_new = jnp.maximum(m, jnp.max(lgs[g], axis=1, keepdims=True))
            alpha = jnp.exp2(m - m_new)
            pr = jnp.exp2(lgs[g] - m_new)
            stats.append((m_new, alpha, alpha * l + jnp.sum(pr, axis=1, keepdims=True), pr.astype(BF16)))
        for g in range(KV_HEADS):
            m_new, alpha, l_new, pr = stats[g]
            state[g] = [m_new, l_new, alpha * state[g][2] + _nt_dot(pr, vt[g])]

    def compute(c, slot):
        attend([_chunk_keys_on_lanes(kbuf[slot, :, g]) for g in range(KV_HEADS)],
               [_chunk_keys_on_lanes(vbuf[slot, :, g]) for g in range(KV_HEADS)], s_ref[:, c * ck:(c + 1) * ck])

    _stream_chunks([poolk_ref, poolv_ref], [kbuf, vbuf], sems, pt_ref, n_chunks, pages_per_chunk, compute)
    attend(knew_ref[...], vnew_ref[...], s_ref[:, n_chunks * ck:])
    for g in range(KV_HEADS):
        o_ref[g] = state[g][2] / state[g][1]


def _sample_attend(page_table, q_bd, s3d, thr3d, knew, vnew, pool_k, pool_v, *, n_t, pages_per_chunk):
    db, n_pages = page_table.shape
    n_chunks = n_pages // pages_per_chunk
    rows = q_bd.shape[2]
    ncols = s3d.shape[2]
    per_b = lambda r, w: pl.BlockSpec((None, r, w), lambda b, pt: (b, 0, 0))
    per_b4 = lambda r, w: pl.BlockSpec((None, KV_HEADS, r, w), lambda b, pt: (b, 0, 0, 0))
    grid_spec = pltpu.PrefetchScalarGridSpec(
        num_scalar_prefetch=1,
        grid=(db,),
        in_specs=[per_b4(rows, HEAD_DIM), per_b(n_t, ncols), per_b(n_t, LANES), per_b4(HEAD_DIM, PAGE_SIZE),
                  per_b4(HEAD_DIM, PAGE_SIZE), pl.BlockSpec(memory_space=pl.ANY), pl.BlockSpec(memory_space=pl.ANY)],
        out_specs=per_b4(rows, HEAD_DIM),
        scratch_shapes=[pltpu.VMEM((2, pages_per_chunk, KV_HEADS, HEAD_DIM, PAGE_SIZE), F32),
                        pltpu.VMEM((2, pages_per_chunk, KV_HEADS, HEAD_DIM, PAGE_SIZE), F32),
                        pltpu.SemaphoreType.DMA((2,))],
    )
    return pl.pallas_call(
        functools.partial(_sample_attend_kernel, n_t=n_t, n_chunks=n_chunks, pages_per_chunk=pages_per_chunk),
        out_shape=jax.ShapeDtypeStruct((db, KV_HEADS, rows, HEAD_DIM), F32),
        grid_spec=grid_spec,
        compiler_params=_cparams("arbitrary"),
        name="sample_attend",
    )(page_table, q_bd, s3d, thr3d, knew, vnew, pool_k, pool_v)


def _retention_sample_kernel(rq_ref, rkt_ref, rv_ref, rg_ref, gain_ref, dfull_ref, qdec_ref, kdec_ref, cdec_ref,
                             st_in_ref, o_ref, st_out_ref, *, n_t, n_heads):
    b = pl.program_id(0)

    @pl.when(b == 0)
    def _():
        o_ref[...] = jnp.zeros_like(o_ref)

    n_rows = rq_ref.shape[0]
    row = lax.broadcasted_iota(jnp.int32, (n_rows, RET_DV), 0)
    mine = (row >= b * n_t) & (row < (b + 1) * n_t)
    for h in range(n_heads):
        cols = slice(h * RET_DV, (h + 1) * RET_DV)
        q = jnp.where(mine, rq_ref[:, cols], 0.0).astype(BF16)
        v = rv_ref[:, cols]
        kt = rkt_ref[cols, :]
        state = st_in_ref[h]
        inner = jnp.dot(q, kt, preferred_element_type=F32) * dfull_ref[h]
        o = jnp.dot(inner.astype(BF16), v, preferred_element_type=F32)
        o = o + jnp.dot(q, state.astype(BF16), preferred_element_type=F32) * qdec_ref[h]
        vd = jnp.where(mine, v.astype(F32) * kdec_ref[h], 0.0).astype(BF16)
        st_out_ref[h] = cdec_ref[h] * state + jnp.dot(kt, vd, preferred_element_type=F32)
        gated = _group_norm_gate(o, rg_ref[:, cols], gain_ref[:, cols])
        o_ref[:, cols] = jnp.where(mine, gated, o_ref[:, cols].astype(F32)).astype(o_ref.dtype)


def _retention_sample(rq, rkt, rv, rg, g_retn, state, *, db, n_t):
    n_rows, n_ret = rq.shape
    n_heads = n_ret // RET_DV
    decay, q_dec, k_dec, c_dec = _retention_tables(n_t, n_heads)
    same = np.kron(np.eye(db), np.ones((n_t, n_t)))
    dfull = jnp.asarray(same[None] * np.tile(decay, (1, db, db)), F32)
    qdec_b = jnp.asarray(np.broadcast_to(np.tile(q_dec, (1, db))[:, :, None], (n_heads, n_rows, RET_DV)), F32)
    kdec_b = jnp.asarray(np.broadcast_to(np.tile(k_dec, (1, db))[:, :, None], (n_heads, n_rows, RET_DV)), F32)
    cdec_b = jnp.asarray(np.broadcast_to(c_dec[:, None, None], (n_heads, 1, RET_DV)), F32)
    const = lambda shape: pl.BlockSpec(shape, lambda b: (0,) * len(shape))
    st_spec = pl.BlockSpec((None, n_heads, RET_DK, RET_DV), lambda b: (b, 0, 0, 0))
    return pl.pallas_call(
        functools.partial(_retention_sample_kernel, n_t=n_t, n_heads=n_heads),
        out_shape=(jax.ShapeDtypeStruct((n_rows, n_ret), BF16),
                   jax.ShapeDtypeStruct((db, n_heads, RET_DK, RET_DV), F32)),
        grid=(db,),
        in_specs=[const((n_rows, n_ret)), const((n_ret, n_rows)), const((n_rows, n_ret)), const((n_rows, n_ret)),
                  const((1, n_ret)), const((n_heads, n_rows, n_rows)), const((n_heads, n_rows, RET_DV)),
                  const((n_heads, n_rows, RET_DV)), const((n_heads, 1, RET_DV)), st_spec],
        out_specs=(const((n_rows, n_ret)), st_spec),
        compiler_params=_cparams("arbitrary"),
        name="retention_sample",
    )(rq, rkt, rv, rg, g_retn.reshape(1, n_ret), dfull, qdec_b, kdec_b, cdec_b, state)


def _mixer_dims(w_in, d_model):
    n_q = d_model // 2
    n_ret = d_model // 2
    n_kv = KV_HEADS * HEAD_DIM
    n_qi = IDX_HEADS * IDX_DIM
    assert n_kv == LANES and 2 * n_kv + n_q + n_qi + IDX_DIM + IDX_HEADS + 4 * n_ret == w_in.shape[1]
    return n_q, n_kv, n_qi, n_ret


def _prompt_layer(x, p, weights, *, n_fast=21):
    g_mix, w_in, g_retn, w_out, g_ffn, w_ffn_in, w_ffn_out, g_ple, w_ple_gate, w_ple_proj, g_final = weights
    nb, seq, d = x.shape
    dims = _mixer_dims(w_in, d)
    w_packed = _pack_w_in(w_in, *dims)
    x2d = x.reshape(nb * seq, d)
    tm = min(512, seq)
    (qt, kt, vt, qit, kit, wit, rq, rkt, rv, rg, kp, vtt, kip) = _inproj(
        x2d, np.arange(seq), g_mix, w_packed, seq=seq, tm=tm, dims=dims)
    topk = min(TOPK_MAX, seq // 4)
    attn = _dsa_prompt(kip, qit, wit, kp, qt, vtt, nb=nb, seq=seq, tq=min(256, seq), topk=topk, n_fast=n_fast)
    ret, st = _retention_prompt(rq, rkt, rv, rg, g_retn, nb=nb, seq=seq, tr=min(512, seq))
    y = _tail(x2d, attn, ret, p.reshape(nb * seq, -1), w_out, g_ffn, w_ffn_in, w_ffn_out,
              g_ple, w_ple_gate, w_ple_proj, g_final, tm=min(256, nb * seq))
    heads_last = lambda a: a.reshape(nb, KV_HEADS, HEAD_DIM, seq).transpose(0, 3, 1, 2)
    return y.reshape(nb, seq, d), heads_last(kt), heads_last(vt), kit.transpose(0, 2, 1), st


def _sample_layer(x, p, pool_k, pool_v, pool_ik, state, page_table, weights, *, n_fast=24):
    g_mix, w_in, g_retn, w_out, g_ffn, w_ffn_in, w_ffn_out, g_ple, w_ple_gate, w_ple_proj, g_final = weights
    db, n_t, d = x.shape
    n_rows = db * n_t
    n_pages = page_table.shape[1]
    past = n_pages * PAGE_SIZE
    dims = _mixer_dims(w_in, d)
    n_q = dims[0]
    w_packed = _pack_w_in(w_in, *dims)
    x2d = x.reshape(n_rows, d)
    pos = np.tile(past + np.arange(n_t), db)
    (qt, kt, vt, qit, kit, wit, rq, rkt, rv, rg, _, _, _) = _inproj(
        x2d, pos, g_mix, w_packed, seq=n_rows, tm=n_rows, dims=dims)
    q, qi, wi = qt[0].T, qit[0].T, wit[0].T

    def new_pages(at, heads):
        a = at[0].reshape(heads, -1, db, n_t).transpose(2, 0, 1, 3)
        return jnp.pad(a, ((0, 0), (0, 0), (0, 0), (0, PAGE_SIZE - n_t))).astype(BF16)
    group = n_q // HEAD_DIM // KV_HEADS
    qg = q.reshape(db, n_t, KV_HEADS, group, HEAD_DIM).transpose(0, 2, 3, 1, 4)
    qg = qg.reshape(db, KV_HEADS, group * n_t, HEAD_DIM)
    pages_per_chunk = max(1, min(64, n_pages // 2))

    s3d = _sample_scores(page_table, qi.reshape(db, n_t * IDX_HEADS, IDX_DIM), wi.reshape(db, n_t * IDX_HEADS, 1),
                         new_pages(kit, 1)[:, 0], pool_ik.transpose(0, 2, 1), n_t=n_t, pages_per_chunk=n_pages)
    topk = min(TOPK_MAX, (past + n_t) // 4)
    s_sel, thr = _sample_select(s3d.reshape(n_rows, -1), topk=topk, n_fast=n_fast)
    o = _sample_attend(page_table, qg, s_sel.reshape(db, n_t, -1), thr.reshape(db, n_t, LANES),
                       new_pages(kt, KV_HEADS), new_pages(vt, KV_HEADS), pool_k.transpose(0, 2, 3, 1),
                       pool_v.transpose(0, 2, 3, 1), n_t=n_t, pages_per_chunk=pages_per_chunk)
    attn = o.reshape(db, KV_HEADS, group, n_t, HEAD_DIM).transpose(0, 3, 1, 2, 4).reshape(n_rows, n_q).astype(BF16)

    ret, st = _retention_sample(rq, rkt[0], rv, rg, g_retn, state, db=db, n_t=n_t)
    y = _tail(x2d, attn, ret, p.reshape(n_rows, -1), w_out, g_ffn, w_ffn_in, w_ffn_out,
              g_ple, w_ple_gate, w_ple_proj, g_final, tm=n_rows)
    heads_last = lambda a: a[0].reshape(KV_HEADS, HEAD_DIM, db, n_t).transpose(2, 3, 0, 1)
    return y.reshape(db, n_t, d), heads_last(kt), heads_last(vt), kit[0].T.reshape(db, n_t, IDX_DIM), st


def kernel(x_prompt, x_sample, cache_k, cache_v, cache_idx_k, state_retn, page_table, p_prompt, p_sample, g_mix, w_in,
           g_retn, w_out, g_ffn, w_ffn_in, w_ffn_out, g_ple, w_ple_gate, w_ple_proj, g_final):
    depth = w_in.shape[0]
    assert depth == 1, "the final RMSNorm is fused into the single layer's tail kernel"
    weights = (g_mix[0], w_in[0], g_retn[0], w_out[0], g_ffn[0], w_ffn_in[0], w_ffn_out[0], g_ple[0], w_ple_gate[0],
               w_ple_proj[0], g_final)
    y_p, k_p, v_p, ik_p, st_p = _prompt_layer(x_prompt, p_prompt[0], weights)
    y_s, k_s, v_s, ik_s, st_s = _sample_layer(x_sample, p_sample[0], cache_k[0], cache_v[0], cache_idx_k[0],
                                              state_retn[0], page_table, weights)
    return (y_p, y_s, k_p[None], v_p[None], ik_p[None], st_p[None], k_s[None], v_s[None], ik_s[None], st_s[None])
```

```python
import functools

import jax
import jax.numpy as jnp
import numpy as np
from jax import lax
from jax.experimental import pallas as pl
from jax.experimental.pallas import tpu as pltpu

F32 = jnp.float32
BF16 = jnp.bfloat16

HEAD_DIM = 64
KV_HEADS = 2
IDX_HEADS = 8
IDX_DIM = 64
TOPK_MAX = 256
RET_DK = 128
RET_DV = 128
RET_CHUNK = 128
PAGE_SIZE = 128
ROPE_THETA = 10000.0
EPS = 1e-6
GN_EPS = 1e-5

LANES = 128
SUBLANES = 8
VMEM_LIMIT_BYTES = 56 * 1024 * 1024

Q_SCALE = HEAD_DIM ** -0.5 * 1.4426950408889634
ONES_ROWS = 16

NEG = -3.0e38
POS = 3.0e38
SOFTMAX_NEG = -1.0e30
NO_TIE = 1.0e9


def _cparams(*sem):
    return pltpu.CompilerParams(dimension_semantics=sem, vmem_limit_bytes=VMEM_LIMIT_BYTES)


def _const_spec(shape):
    zeros = (0,) * len(shape)
    return pl.BlockSpec(shape, lambda *_: zeros, pipeline_mode=pl.Buffered(1))


def _rope_tables(pos, dim):
    half = dim // 2
    inv = ROPE_THETA ** (-np.arange(half, dtype=np.float64) / half)
    ang = np.asarray(pos, np.float64)[:, None] * inv[None, :]
    cos, sin = np.cos(ang), np.sin(ang)
    reps = LANES // dim
    cos_t = np.tile(np.concatenate([cos, cos], axis=1), (1, reps))
    sin_t = np.tile(np.concatenate([-sin, sin], axis=1), (1, reps))
    return jnp.asarray(cos_t, F32), jnp.asarray(sin_t, F32)


def _swap_halves_64(x):
    lane = lax.broadcasted_iota(jnp.int32, x.shape, 1)
    first = (lane % 64) < 32
    return jnp.where(first, pltpu.roll(x, 96, 1), pltpu.roll(x, 32, 1))


def _pad_pair_slabs(x, swapped, low):
    return (jnp.where(low, x, 0.0), jnp.where(low, 0.0, swapped), jnp.where(low, swapped, 0.0), jnp.where(low, 0.0, x))


def _inproj_kernel(x_ref, g_ref, w_ref, c64_ref, s64_ref, c128_ref, s128_ref,
                   qt_ref, kt_ref, vt_ref, qit_ref, kit_ref, wit_ref, rq_ref, rkt_ref, rv_ref, rg_ref,
                   kp_ref, vtt_ref, kip_ref, *, n_q, n_qi, n_ret):
    x = x_ref[...]
    tm = x.shape[0]
    ms = jnp.mean(x * x, axis=-1, keepdims=True)
    a = ((x * lax.rsqrt(ms + EPS)) * g_ref[...]).astype(BF16)
    c64, s64 = c64_ref[...], s64_ref[...]
    c128, s128 = c128_ref[...], s128_ref[...]
    low = lax.broadcasted_iota(jnp.int32, (tm, LANES), 1) < HEAD_DIM

    col = [0]

    def segment(width):
        y = jnp.dot(a, w_ref[:, col[0]:col[0] + width], preferred_element_type=F32)
        col[0] += width
        return [y[:, j * LANES:(j + 1) * LANES] for j in range(width // LANES)]

    def rope64(y):
        return y * c64 + _swap_halves_64(y) * s64

    def rope128(y):
        return y * c128 + pltpu.roll(y, 64, 1) * s128

    for j, y in enumerate(segment(n_q)):
        qt_ref[j * LANES:(j + 1) * LANES, :] = (rope64(y) * Q_SCALE).T.astype(BF16)
    kk, vv = segment(2 * LANES)
    kk = rope64(kk)
    kt_ref[...] = kk.T
    for s, slab in enumerate(_pad_pair_slabs(kk, pltpu.roll(kk, HEAD_DIM, 1), low)):
        kp_ref[:, s * LANES:(s + 1) * LANES] = slab.astype(BF16)
    vvt = vv.T
    vt_ref[...] = vvt
    vtt_ref[...] = vvt.astype(BF16)
    for j, y in enumerate(segment(n_qi)):
        qit_ref[j * LANES:(j + 1) * LANES, :] = (rope64(y) * (IDX_DIM ** -0.5)).T.astype(BF16)
    y, _ = segment(2 * LANES)
    ki = rope64(y)
    kit_ref[...] = ki.T[:IDX_DIM, :]
    ki_lo, ki_hi, _, _ = _pad_pair_slabs(ki, pltpu.roll(ki, IDX_DIM, 1), low)
    kip_ref[:, 0:LANES] = ki_lo.astype(BF16)
    kip_ref[:, LANES:2 * LANES] = ki_hi.astype(BF16)
    wit_ref[...] = y.T[IDX_DIM:IDX_DIM + IDX_HEADS, :] * (IDX_HEADS ** -0.5)
    for j, y in enumerate(segment(n_ret)):
        rq_ref[:, j * LANES:(j + 1) * LANES] = rope128(y).astype(BF16)
    for j, y in enumerate(segment(n_ret)):
        rkt_ref[j * LANES:(j + 1) * LANES, :] = (rope128(y) * (RET_DK ** -0.5)).T.astype(BF16)
    for j, y in enumerate(segment(n_ret)):
        rv_ref[:, j * LANES:(j + 1) * LANES] = y.astype(BF16)
    for j, y in enumerate(segment(n_ret)):
        rg_ref[:, j * LANES:(j + 1) * LANES] = y


def _pack_w_in(w_in, n_q, n_kv, n_qi, n_ret):
    d = w_in.shape[0]
    sizes = (n_q, n_kv, n_kv, n_qi, IDX_DIM, IDX_HEADS, n_ret, n_ret, n_ret, n_ret)
    offs = np.concatenate([[0], np.cumsum(sizes)])
    seg = [w_in[:, offs[i]:offs[i + 1]] for i in range(len(sizes))]
    pad = jnp.zeros((d, 2 * LANES - IDX_DIM - IDX_HEADS), w_in.dtype)
    packed = jnp.concatenate(seg[:4] + [seg[4], seg[5], pad] + seg[6:], axis=1)
    return packed.astype(BF16)


def _inproj(x2d, pos, g_mix, w_packed, *, seq, tm, dims):
    n_q, n_kv, n_qi, n_ret = dims
    rows, d = x2d.shape
    nb = rows // seq
    spt = seq // tm
    c64, s64 = _rope_tables(pos, HEAD_DIM)
    c128, s128 = _rope_tables(pos, RET_DK)
    wcols = w_packed.shape[1]

    row = lambda w: pl.BlockSpec((tm, w), lambda i: (i, 0))
    tab = pl.BlockSpec((tm, LANES), lambda i: (i % spt, 0))
    tr = lambda h: pl.BlockSpec((None, h, tm), lambda i: (i // spt, 0, i % spt))
    out_shape = (
        jax.ShapeDtypeStruct((nb, n_q, seq), BF16),
        jax.ShapeDtypeStruct((nb, n_kv, seq), F32),
        jax.ShapeDtypeStruct((nb, n_kv, seq), F32),
        jax.ShapeDtypeStruct((nb, n_qi, seq), BF16),
        jax.ShapeDtypeStruct((nb, IDX_DIM, seq), F32),
        jax.ShapeDtypeStruct((nb, IDX_HEADS, seq), F32),
        jax.ShapeDtypeStruct((rows, n_ret), BF16),
        jax.ShapeDtypeStruct((nb, n_ret, seq), BF16),
        jax.ShapeDtypeStruct((rows, n_ret), BF16),
        jax.ShapeDtypeStruct((rows, n_ret), F32),
        jax.ShapeDtypeStruct((rows, 4 * n_kv), BF16),
        jax.ShapeDtypeStruct((nb, n_kv, seq), BF16),
        jax.ShapeDtypeStruct((rows, 2 * LANES), BF16),
    )
    out_specs = (tr(n_q), tr(n_kv), tr(n_kv), tr(n_qi), tr(IDX_DIM), tr(IDX_HEADS),
                 row(n_ret), tr(n_ret), row(n_ret), row(n_ret), row(4 * n_kv), tr(n_kv), row(2 * LANES))
    return pl.pallas_call(
        functools.partial(_inproj_kernel, n_q=n_q, n_qi=n_qi, n_ret=n_ret),
        out_shape=out_shape,
        grid=(rows // tm,),
        in_specs=[row(d), _const_spec((1, d)), _const_spec((d, wcols)), tab, tab, tab, tab],
        out_specs=out_specs,
        compiler_params=_cparams("parallel"),
        name="inproj",
    )(x2d, g_mix.reshape(1, d), w_packed, c64, s64, c128, s128)


N_ACC = 4


def _fold_keys(st_ref, nch, ck, init, fn, combine):
    def body(c, carries):
        carries = list(carries)
        base = pl.multiple_of(c * ck, ck)
        chunk = st_ref[pl.ds(base, ck), :]
        for j in range(ck // SUBLANES):
            blk = chunk[j * SUBLANES:(j + 1) * SUBLANES, :]
            carries[j % N_ACC] = fn(carries[j % N_ACC], blk, base + j * SUBLANES)
        return tuple(carries)
    carries = lax.fori_loop(0, nch, body, (init,) * N_ACC)
    out = carries[0]
    for other in carries[1:]:
        out = combine(out, other)
    return out


def _all_sublanes(x, op):
    for shift in (4, 2, 1):
        x = op(x, pltpu.roll(x, shift, 0))
    return x


def _any(mask):
    return jnp.max(jnp.where(mask, 1.0, 0.0)) > 0.5


def _count(st_ref, nch, ck, pred):
    nq = st_ref.shape[1]
    acc = _fold_keys(st_ref, nch, ck, jnp.zeros((SUBLANES, nq), F32),
                     lambda a, blk, row0: a + jnp.where(pred(blk, row0), 1.0, 0.0), jnp.add)
    return _all_sublanes(acc, jnp.add)


def _select_threshold(st_ref, thr_ref, tie_ref, nch, ck, rmin, rmax, topk, n_fast):
    nq = st_ref.shape[1]
    kf = float(topk)
    full = lambda v: jnp.full((SUBLANES, nq), v, F32)

    def fast_body(_, carry):
        lo, hi = carry
        mid = lo + (hi - lo) * 0.5
        up = _count(st_ref, nch, ck, lambda blk, _r: blk > mid) >= kf
        return jnp.where(up, mid, lo), jnp.where(up, hi, mid)

    lo, hi = lax.fori_loop(0, n_fast, fast_body, (rmin, rmax))
    unres = _count(st_ref, nch, ck, lambda blk, _r: blk >= lo) > kf
    thr_ref[...] = lo

    @pl.when(_any(unres))
    def _exact():
        pair_minmax = lambda x, y: (jnp.minimum(x[0], y[0]), jnp.maximum(x[1], y[1]))
        a, b = _fold_keys(
            st_ref, nch, ck, (full(POS), full(NEG)),
            lambda cr, blk, _r: (jnp.minimum(cr[0], jnp.where(blk >= lo, blk, POS)),
                                 jnp.maximum(cr[1], jnp.where(blk <= hi, blk, NEG))),
            pair_minmax)
        lo2 = jnp.where(unres, _all_sublanes(a, jnp.minimum), lo)
        ub = jnp.where(unres, _all_sublanes(b, jnp.maximum), lo)

        def body(carry):
            lo2, ub, _ = carry
            mid = lo2 + (ub - lo2) * 0.5
            mid = jnp.where(mid >= ub, lo2, mid)

            def step(cr, blk, _r):
                gt = blk > mid
                return (cr[0] + jnp.where(gt, 1.0, 0.0), jnp.minimum(cr[1], jnp.where(gt, blk, POS)),
                        jnp.maximum(cr[2], jnp.where(gt, NEG, blk)))
            cnt, a, b = _fold_keys(st_ref, nch, ck, (full(0.0), full(POS), full(NEG)), step,
                                   lambda x, y: (x[0] + y[0],) + pair_minmax(x[1:], y[1:]))
            up = _all_sublanes(cnt, jnp.add) >= kf
            active = lo2 < ub
            lo2n = jnp.where(active & up, _all_sublanes(a, jnp.minimum), lo2)
            ubn = jnp.where(active & jnp.logical_not(up), _all_sublanes(b, jnp.maximum), ub)
            return lo2n, ubn, _any(lo2n < ubn).astype(jnp.int32)

        v, _, _ = lax.while_loop(lambda carry: carry[2] > 0, body, (lo2, ub, _any(lo2 < ub).astype(jnp.int32)))
        thr_ref[...] = jnp.where(unres, v, lo)
        n_ge, n_gt = _fold_keys(
            st_ref, nch, ck, (full(0.0), full(0.0)),
            lambda cr, blk, _r: (cr[0] + jnp.where(blk >= v, 1.0, 0.0), cr[1] + jnp.where(blk > v, 1.0, 0.0)),
            lambda x, y: (x[0] + y[0], x[1] + y[1]))
        tied = unres & (_all_sublanes(n_ge, jnp.add) > kf)

        @pl.when(_any(tied))
        def _ties():
            _drop_excess_ties(st_ref, tie_ref, nch, ck, v, tied, kf - _all_sublanes(n_gt, jnp.add))


def _drop_excess_ties(st_ref, tie_ref, nch, ck, v, tied, need):
    nq = st_ref.shape[1]
    full = lambda x: jnp.full((SUBLANES, nq), x, F32)
    nkeys = st_ref.shape[0]
    key_in_chunk = lax.broadcasted_iota(jnp.int32, (ck, nq), 0)
    rep = lambda x: jnp.broadcast_to(x[0:1, :], (ck, nq))
    v_c = rep(v)

    def each_chunk(fn):
        def body(c, _):
            base = pl.multiple_of(c * ck, ck)
            fn(pl.ds(base, ck), base)
            return 0
        lax.fori_loop(0, nch, body, 0)

    def mark(rows, row0):
        tie_ref[rows, :] = jnp.where(st_ref[rows, :] == v_c, (key_in_chunk + row0).astype(F32), NO_TIE)
    each_chunk(mark)

    n_j = int(np.ceil(np.log2(nkeys))) + 1
    need_t = jnp.where(tied, need, 0.0)
    need_max = jnp.max(need_t)

    def by_walk():
        def body(carry):
            j_prev, k = carry
            nxt = _fold_keys(tie_ref, nch, ck, full(NO_TIE),
                             lambda a, blk, _r: jnp.minimum(a, jnp.where(blk > j_prev, blk, NO_TIE)), jnp.minimum)
            return jnp.where(need_t > k, _all_sublanes(nxt, jnp.minimum), j_prev), k + 1.0
        return lax.while_loop(lambda carry: carry[1] < need_max, body, (full(-1.0), jnp.float32(0.0)))[0]

    def by_bisection():
        def body(_, carry):
            lo_j, hi_j = carry
            mid_j = jnp.floor((lo_j + hi_j) * 0.5)
            ok = _count(tie_ref, nch, ck, lambda blk, _r: blk <= mid_j) >= need
            return jnp.where(ok, lo_j, mid_j), jnp.where(ok, mid_j, hi_j)
        return lax.fori_loop(0, n_j, body, (full(-1.0), full(float(nkeys - 1))))[1]

    j_last = lax.cond(need_max <= float(n_j), by_walk, by_bisection)

    j_c = rep(jnp.where(tied, j_last, NO_TIE))

    def drop(rows, _row0):
        t = tie_ref[rows, :]
        s = st_ref[rows, :]
        st_ref[rows, :] = jnp.where(t > j_c, jnp.where(t < NO_TIE, NEG, s), s)
    each_chunk(drop)


def _rows_min(x):
    return jnp.min(x.reshape(x.shape[0] // SUBLANES, SUBLANES, x.shape[1]), axis=0)


def _rows_max(x):
    return jnp.max(x.reshape(x.shape[0] // SUBLANES, SUBLANES, x.shape[1]), axis=0)


def _rows_sum(x):
    return jnp.sum(x.reshape(x.shape[0] // SUBLANES, SUBLANES, x.shape[1]), axis=0)


def _dsa_prompt_kernel(kip_ref, qit_ref, wit_ref, kp_ref, qt_ref, vtt_ref, o_ref, st_ref, thr_ref,
                       m_ref, l_ref, acc_ref, lg_ref, pr_ref, tie_ref, *, tq, topk, n_fast, rs):
    ck = tq
    i = pl.program_id(1)
    nch = i + 1
    n_pairs = qt_ref.shape[0] // LANES
    pairs_per_kv = n_pairs // KV_HEADS
    bcast = lambda row, n: jnp.broadcast_to(row, (n, tq))

    key_l = lax.broadcasted_iota(jnp.int32, (rs, tq), 0)
    qry_l = lax.broadcasted_iota(jnp.int32, (rs, tq), 1)
    w_rows = [bcast(wit_ref[h:h + 1, :], rs) for h in range(IDX_HEADS)]

    def score_body(c, carry):
        mn, mx = carry
        base = pl.multiple_of(c * ck, ck)
        slack = jnp.where(c < i, ck, 0)
        for r in range(ck // rs):
            rows = pl.ds(base + r * rs, rs)
            k_lo = kip_ref[rows, 0:LANES]
            k_hi = kip_ref[rows, LANES:2 * LANES]
            acc = jnp.zeros((rs, tq), F32)
            for p in range(IDX_HEADS // 2):
                rhs = qit_ref[p * LANES:(p + 1) * LANES, :]
                acc = acc + jnp.maximum(jnp.dot(k_lo, rhs, preferred_element_type=F32), 0.0) * w_rows[2 * p]
                acc = acc + jnp.maximum(jnp.dot(k_hi, rhs, preferred_element_type=F32), 0.0) * w_rows[2 * p + 1]
            valid = key_l + r * rs <= qry_l + slack
            st_ref[rows, :] = jnp.where(valid, acc, NEG)
            mn = jnp.minimum(mn, _rows_min(jnp.where(valid, acc, POS)))
            mx = jnp.maximum(mx, _rows_max(jnp.where(valid, acc, NEG)))
        return mn, mx

    mn, mx = lax.fori_loop(0, nch, score_body,
                           (jnp.full((SUBLANES, tq), POS, F32), jnp.full((SUBLANES, tq), NEG, F32)))
    _select_threshold(st_ref, thr_ref, tie_ref, nch, ck,
                      _all_sublanes(mn, jnp.minimum), _all_sublanes(mx, jnp.maximum), topk, n_fast)

    thr = bcast(thr_ref[0:1, :], ck)
    m_ref[...] = jnp.full(m_ref.shape, SOFTMAX_NEG, F32)
    l_ref[...] = jnp.zeros(l_ref.shape, F32)
    acc_ref[...] = jnp.zeros(acc_ref.shape, F32)

    def att_body(c, _):
        rows = pl.ds(pl.multiple_of(c * ck, ck), ck)
        sel = st_ref[rows, :] >= thr
        slab_of = lambda h: 2 * (h // 2 // pairs_per_kv) + h % 2
        n_heads = 2 * n_pairs
        for h in range(n_heads):
            slab = slab_of(h)
            kx = kp_ref[rows, slab * LANES:(slab + 1) * LANES]
            qt_pair = qt_ref[(h // 2) * LANES:(h // 2 + 1) * LANES, :]
            lg_ref[h] = jnp.where(sel, jnp.dot(kx, qt_pair, preferred_element_type=F32), SOFTMAX_NEG)
        alpha = []
        for h in range(n_heads):
            lg = lg_ref[h]
            m_old = m_ref[h]
            m_new = jnp.maximum(m_old, _all_sublanes(_rows_max(lg), jnp.maximum))
            alpha.append(jnp.exp2(m_old - m_new))
            m_ref[h] = m_new
            pr_ref[h] = jnp.exp2(lg - bcast(m_new[0:1, :], ck)).astype(BF16)
        ones = jnp.ones((ONES_ROWS, ck), BF16)
        for p in range(n_pairs):
            pv = []
            g = p // pairs_per_kv
            vt1 = jnp.concatenate([vtt_ref[g * HEAD_DIM:(g + 1) * HEAD_DIM, rows], ones], axis=0)
            for h in (2 * p, 2 * p + 1):
                pv1 = jnp.dot(vt1, pr_ref[h], preferred_element_type=F32)
                l_ref[h] = alpha[h] * l_ref[h] + pv1[HEAD_DIM:HEAD_DIM + SUBLANES, :]
                pv.append(pv1[:HEAD_DIM, :])
            scale = jnp.concatenate([bcast(alpha[2 * p][0:1, :], HEAD_DIM), bcast(alpha[2 * p + 1][0:1, :], HEAD_DIM)],
                                    axis=0)
            acc_ref[p] = acc_ref[p] * scale + jnp.concatenate(pv, axis=0)
        return 0

    lax.fori_loop(0, nch, att_body, 0)
    for p in range(n_pairs):
        denom = jnp.concatenate([bcast(l_ref[2 * p, 0:1, :], HEAD_DIM), bcast(l_ref[2 * p + 1, 0:1, :], HEAD_DIM)],
                                axis=0)
        o_ref[:, p * LANES:(p + 1) * LANES] = (acc_ref[p] / denom).T.astype(o_ref.dtype)


def _dsa_prompt(kip, qit, wit, kp, qt, vtt, *, nb, seq, tq, topk, n_fast):
    n_q = qt.shape[1]
    spb = seq // tq
    per_b_rows = lambda a: pl.BlockSpec((seq, a.shape[1]), lambda b, i: (b, 0))
    q_cols = lambda a: pl.BlockSpec((None, a.shape[1], tq), lambda b, i: (b, 0, i))
    return pl.pallas_call(
        functools.partial(_dsa_prompt_kernel, tq=tq, topk=topk, n_fast=n_fast, rs=min(tq, 128)),
        out_shape=jax.ShapeDtypeStruct((nb * seq, n_q), BF16),
        grid=(nb, spb),
        in_specs=[per_b_rows(kip), q_cols(qit), q_cols(wit), per_b_rows(kp), q_cols(qt),
                  pl.BlockSpec((None, vtt.shape[1], seq), lambda b, i: (b, 0, 0))],
        out_specs=pl.BlockSpec((tq, n_q), lambda b, i: (b * spb + i, 0)),
        scratch_shapes=[pltpu.VMEM((seq, tq), F32), pltpu.VMEM((SUBLANES, tq), F32),
                        pltpu.VMEM((n_q // HEAD_DIM, SUBLANES, tq), F32),
                        pltpu.VMEM((n_q // HEAD_DIM, SUBLANES, tq), F32),
                        pltpu.VMEM((n_q // LANES, LANES, tq), F32),
                        pltpu.VMEM((n_q // HEAD_DIM, tq, tq), F32),
                        pltpu.VMEM((n_q // HEAD_DIM, tq, tq), BF16),
                        pltpu.VMEM((seq, tq), F32)],
        compiler_params=_cparams("parallel", "arbitrary"),
        name="dsa_prompt",
    )(kip, qit, wit, kp, qt, vtt)


def _retention_tables(chunk, n_heads):
    log_g = np.log1p(-np.exp2(-5.0 - np.arange(n_heads, dtype=np.float64)))
    i = np.arange(chunk, dtype=np.float64)
    diff = i[:, None] - i[None, :]
    decay = np.where(diff >= 0, np.exp(log_g[:, None, None] * np.maximum(diff, 0.0)), 0.0)
    q_dec = np.exp(log_g[:, None] * (i[None, :] + 1.0))
    k_dec = np.exp(log_g[:, None] * (chunk - 1.0 - i)[None, :])
    c_dec = np.exp(log_g * chunk)
    return decay, q_dec, k_dec, c_dec


def _group_norm_gate(o, rg, gain):
    mu = jnp.mean(o, axis=-1, keepdims=True)
    var = jnp.mean(jnp.square(o - mu), axis=-1, keepdims=True)
    on = ((o - mu) * lax.rsqrt(var + GN_EPS)) * gain
    return jax.nn.silu(rg) * on


def _retention_prompt_kernel(rq_ref, rkt_ref, rv_ref, rg_ref, gain_ref, dmat_ref, qdec_ref, kdec_ref, cdec_ref,
                             o_ref, st_ref, state_ref, *, chunk, n_heads):
    j = pl.program_id(1)

    @pl.when(j == 0)
    def _():
        state_ref[...] = jnp.zeros_like(state_ref)

    n_cc = rq_ref.shape[0] // chunk
    dot = functools.partial(jnp.dot, preferred_element_type=F32)
    tiles = [(cc, h, slice(cc * chunk, (cc + 1) * chunk), slice(h * RET_DV, (h + 1) * RET_DV))
             for cc in range(n_cc) for h in range(n_heads)]
    inner, update = {}, {}
    for cc, h, rows, cols in tiles:
        kt = rkt_ref[cols, rows]
        inner[cc, h] = (dot(rq_ref[rows, cols], kt) * dmat_ref[h]).astype(BF16)
        update[cc, h] = dot((kt.astype(F32) * kdec_ref[h]).astype(BF16), rv_ref[rows, cols])
    cross = {}
    for h in range(n_heads):
        state = state_ref[h]
        for cc in range(n_cc):
            rows, cols = slice(cc * chunk, (cc + 1) * chunk), slice(h * RET_DV, (h + 1) * RET_DV)
            cross[cc, h] = dot(rq_ref[rows, cols], state.astype(BF16)) * qdec_ref[h]
            state = cdec_ref[h] * state + update[cc, h]
        state_ref[h] = state
    for cc, h, rows, cols in tiles:
        o = dot(inner[cc, h], rv_ref[rows, cols]) + cross[cc, h]
        o_ref[rows, cols] = _group_norm_gate(o, rg_ref[rows, cols], gain_ref[:, cols]).astype(o_ref.dtype)

    @pl.when(j == pl.num_programs(1) - 1)
    def _():
        st_ref[...] = state_ref[...]


def _retention_prompt(rq, rkt, rv, rg, g_retn, *, nb, seq, tr):
    n_ret = rq.shape[1]
    n_heads = n_ret // RET_DV
    chunk = RET_CHUNK
    decay, q_dec, k_dec, c_dec = _retention_tables(chunk, n_heads)
    decay = jnp.asarray(decay, F32)
    qdec_b = jnp.asarray(np.broadcast_to(q_dec[:, :, None], (n_heads, chunk, LANES)), F32)
    kdec_b = jnp.asarray(k_dec[:, None, :], F32)
    cdec_b = jnp.asarray(np.broadcast_to(c_dec[:, None, None], (n_heads, 1, LANES)), F32)
    spb = seq // tr
    row = lambda: pl.BlockSpec((tr, n_ret), lambda b, j: (b * spb + j, 0))
    return pl.pallas_call(
        functools.partial(_retention_prompt_kernel, chunk=chunk, n_heads=n_heads),
        out_shape=(jax.ShapeDtypeStruct((nb * seq, n_ret), BF16),
                   jax.ShapeDtypeStruct((nb, n_heads, RET_DK, RET_DV), F32)),
        grid=(nb, spb),
        in_specs=[row(), pl.BlockSpec((None, n_ret, tr), lambda b, j: (b, 0, j)), row(), row(),
                  _const_spec((1, n_ret)), _const_spec((n_heads, chunk, chunk)),
                  _const_spec((n_heads, chunk, LANES)), _const_spec((n_heads, 1, chunk)),
                  _const_spec((n_heads, 1, LANES))],
        out_specs=(row(), pl.BlockSpec((None, n_heads, RET_DK, RET_DV), lambda b, j: (b, 0, 0, 0))),
        scratch_shapes=[pltpu.VMEM((n_heads, RET_DK, RET_DV), F32)],
        compiler_params=_cparams("parallel", "arbitrary"),
        name="retention_prompt",
    )(rq, rkt, rv, rg, g_retn.reshape(1, n_ret), decay, qdec_b, kdec_b, cdec_b)


def _rms(x, g):
    return (x * lax.rsqrt(jnp.mean(x * x, axis=-1, keepdims=True) + EPS)) * g


def _tail_kernel(h_ref, attn_ref, ret_ref, p_ref, wo_ref, g_ffn_ref, w_ffn_in_ref, w_down_ref,
                 g_ple_ref, w_pg_ref, w_pp_ref, g_fin_ref, y_ref):
    dot = functools.partial(jnp.dot, preferred_element_type=F32)
    n_attn = attn_ref.shape[1]
    d_ff = w_down_ref.shape[0]
    h = h_ref[...] + dot(attn_ref[...], wo_ref[:n_attn, :]) + dot(ret_ref[...], wo_ref[n_attn:, :])
    f = _rms(h, g_ffn_ref[...]).astype(BF16)
    act = (jax.nn.silu(dot(f, w_ffn_in_ref[:, :d_ff])) * dot(f, w_ffn_in_ref[:, d_ff:])).astype(BF16)
    h = h + dot(act, w_down_ref[...])
    gate = jax.nn.sigmoid(dot(_rms(h, g_ple_ref[...]).astype(BF16), w_pg_ref[...]))
    h = h + gate * dot(p_ref[...].astype(BF16), w_pp_ref[...])
    y_ref[...] = _rms(h, g_fin_ref[...])


def _tail(h2d, attn, ret, p2d, w_out, g_ffn, w_ffn_in, w_ffn_out, g_ple, w_ple_gate, w_ple_proj, g_final, *, tm):
    rows, d = h2d.shape
    n_attn = attn.shape[1]
    assert w_ffn_in.shape[1] == 2 * w_ffn_out.shape[0] and w_ffn_out.shape[0] % LANES == 0
    ws = [w_out.astype(BF16), g_ffn.reshape(1, d), w_ffn_in.astype(BF16), w_ffn_out.astype(BF16),
          g_ple.reshape(1, d), w_ple_gate.astype(BF16), w_ple_proj.astype(BF16), g_final.reshape(1, d)]
    row = lambda w: pl.BlockSpec((tm, w), lambda i: (i, 0))
    return pl.pallas_call(
        _tail_kernel,
        out_shape=jax.ShapeDtypeStruct((rows, d), F32),
        grid=(rows // tm,),
        in_specs=[row(d), row(n_attn), row(ret.shape[1]), row(p2d.shape[1])] + [_const_spec(w.shape) for w in ws],
        out_specs=row(d),
        compiler_params=_cparams("parallel"),
        name="tail",
    )(h2d, attn, ret, p2d, *ws)


def _page_copies(pools, bufs, sems, pt_ref, b, c, slot, pages_per_chunk):
    copies = []
    for pool, buf in zip(pools, bufs):
        for p in range(pages_per_chunk):
            page = pt_ref[b, c * pages_per_chunk + p]
            copies.append(pltpu.make_async_copy(pool.at[page], buf.at[slot, p], sems.at[slot]))
    return copies


def _chunk_keys_on_lanes(pages):
    return jnp.concatenate([pages[p] for p in range(pages.shape[0])], axis=1).astype(BF16)


def _stream_chunks(pools, bufs, sems, pt_ref, n_chunks, pages_per_chunk, compute):
    b = pl.program_id(0)
    nb = pl.num_programs(0)

    def start(bb, cc, slot):
        for cp in _page_copies(pools, bufs, sems, pt_ref, bb, cc, slot, pages_per_chunk):
            cp.start()

    @pl.when(b == 0)
    def _():
        start(0, 0, 0)

    first = 0 if n_chunks % 2 == 0 else b % 2
    for c in range(n_chunks):
        slot = (first + c) % 2
        if c + 1 < n_chunks:
            start(b, c + 1, 1 - slot)
        else:
            @pl.when(b + 1 < nb)
            def _():
                start(b + 1, 0, 1 - slot)
        for cp in _page_copies(pools, bufs, sems, pt_ref, b, c, slot, pages_per_chunk):
            cp.wait()
        compute(c, slot)


def _tile_lanes(x, width):
    return x if width == LANES else jnp.concatenate([x] * (width // LANES), axis=1)


def _nt_dot(a, b):
    return lax.dot_general(a, b, (((1,), (1,)), ((), ())), preferred_element_type=F32)


def _head_sum(x, n_t):
    return jnp.sum(x.reshape(n_t, IDX_HEADS, x.shape[-1]), axis=1)


def _sample_scores_kernel(pt_ref, qi_ref, wi_ref, kinew_ref, pool_ref, s_ref, buf, sems,
                          *, n_t, n_chunks, pages_per_chunk, pages_per_dot):
    ck = pages_per_chunk * PAGE_SIZE
    qi = qi_ref[...]
    w = wi_ref[...]

    def compute(c, slot):
        for p0 in range(0, pages_per_chunk, pages_per_dot):
            kit = _chunk_keys_on_lanes(buf[slot, p0:p0 + pages_per_dot])
            a = jnp.dot(qi, kit, preferred_element_type=F32)
            col0 = c * ck + p0 * PAGE_SIZE
            s_ref[:, col0:col0 + pages_per_dot * PAGE_SIZE] = _head_sum(jnp.maximum(a, 0.0) * w, n_t)

    _stream_chunks([pool_ref], [buf], sems, pt_ref, n_chunks, pages_per_chunk, compute)

    a = jnp.dot(qi, kinew_ref[...], preferred_element_type=F32)
    s_new = _head_sum(jnp.maximum(a, 0.0) * w, n_t)
    t_q = lax.broadcasted_iota(jnp.int32, s_new.shape, 0)
    t_k = lax.broadcasted_iota(jnp.int32, s_new.shape, 1)
    s_ref[:, n_chunks * ck:] = jnp.where(t_k <= t_q, s_new, NEG)


def _sample_scores(page_table, qi_s, wi_s, kinew, pool_ik, *, n_t, pages_per_chunk):
    db, n_pages = page_table.shape
    n_chunks = n_pages // pages_per_chunk
    past = n_pages * PAGE_SIZE
    rows = n_t * IDX_HEADS
    grid_spec = pltpu.PrefetchScalarGridSpec(
        num_scalar_prefetch=1,
        grid=(db,),
        in_specs=[pl.BlockSpec((None, rows, IDX_DIM), lambda b, pt: (b, 0, 0)),
                  pl.BlockSpec((None, rows, 1), lambda b, pt: (b, 0, 0)),
                  pl.BlockSpec((None, IDX_DIM, PAGE_SIZE), lambda b, pt: (b, 0, 0)),
                  pl.BlockSpec(memory_space=pl.ANY)],
        out_specs=pl.BlockSpec((None, n_t, past + PAGE_SIZE), lambda b, pt: (b, 0, 0)),
        scratch_shapes=[pltpu.VMEM((2, pages_per_chunk, IDX_DIM, PAGE_SIZE), F32),
                        pltpu.SemaphoreType.DMA((2,))],
    )
    return pl.pallas_call(
        functools.partial(_sample_scores_kernel, n_t=n_t, n_chunks=n_chunks, pages_per_chunk=pages_per_chunk,
                          pages_per_dot=min(16, pages_per_chunk)),
        out_shape=jax.ShapeDtypeStruct((db, n_t, past + PAGE_SIZE), F32),
        grid_spec=grid_spec,
        compiler_params=_cparams("arbitrary"),
        name="sample_scores",
    )(page_table, qi_s, wi_s, kinew, pool_ik)


def _sample_select_kernel(s_in_ref, s_out_ref, thr_ref, st_ref, thr8_ref, tie_ref, *, ck, topk, n_fast):
    nq, nkeys = s_in_ref.shape
    nch = nkeys // ck + 0 * pl.program_id(0)
    chunk = lambda c: pl.ds(pl.multiple_of(c * ck, ck), ck)

    def to_lanes(c, _):
        st_ref[chunk(c), :] = s_in_ref[:, chunk(c)].T
        return 0
    lax.fori_loop(0, nch, to_lanes, 0)

    mn, mx = _fold_keys(
        st_ref, nch, ck, (jnp.full((SUBLANES, nq), POS, F32), jnp.full((SUBLANES, nq), NEG, F32)),
        lambda cr, blk, _r: (jnp.minimum(cr[0], jnp.where(blk > 0.5 * NEG, blk, POS)), jnp.maximum(cr[1], blk)),
        lambda x, y: (jnp.minimum(x[0], y[0]), jnp.maximum(x[1], y[1])))
    _select_threshold(st_ref, thr8_ref, tie_ref, nch, ck, _all_sublanes(mn, jnp.minimum),
                      _all_sublanes(mx, jnp.maximum), topk, n_fast)
    def to_rows(c, _):
        s_out_ref[:, chunk(c)] = st_ref[chunk(c), :].T
        return 0
    lax.fori_loop(0, nch, to_rows, 0)
    thr_ref[...] = jnp.broadcast_to(thr8_ref[0:1, :], (LANES, nq)).T


def _sample_select(s2d, *, topk, n_fast):
    rows, ncols = s2d.shape
    ck = max(w for w in (LANES, 2 * LANES, 3 * LANES, 4 * LANES) if ncols % w == 0)
    full = pl.BlockSpec((rows, ncols), lambda i: (0, 0), pipeline_mode=pl.Buffered(1))
    return pl.pallas_call(
        functools.partial(_sample_select_kernel, ck=ck, topk=topk, n_fast=n_fast),
        out_shape=(jax.ShapeDtypeStruct((rows, ncols), F32), jax.ShapeDtypeStruct((rows, LANES), F32)),
        grid=(1,),
        in_specs=[full],
        out_specs=(full, pl.BlockSpec((rows, LANES), lambda i: (0, 0))),
        scratch_shapes=[pltpu.VMEM((ncols, rows), F32), pltpu.VMEM((SUBLANES, rows), F32),
                        pltpu.VMEM((ncols, rows), F32)],
        compiler_params=_cparams("arbitrary"),
        name="sample_select",
    )(s2d)


def _sample_attend_kernel(pt_ref, q_ref, s_ref, thr_ref, knew_ref, vnew_ref, poolk_ref, poolv_ref, o_ref,
                          kbuf, vbuf, sems, *, n_t, n_chunks, pages_per_chunk):
    ck = pages_per_chunk * PAGE_SIZE
    rows = q_ref.shape[1]
    reps = rows // n_t
    thr = jnp.concatenate([thr_ref[...]] * reps, axis=0)
    state = [[jnp.full((rows, 1), SOFTMAX_NEG, F32), jnp.zeros((rows, 1), F32), jnp.zeros((rows, HEAD_DIM), F32)]
             for _ in range(KV_HEADS)]

    def attend(kt, vt, s_blk):
        width = s_blk.shape[1]
        sel = jnp.concatenate([s_blk] * reps, axis=0) >= _tile_lanes(thr, width)
        lgs = [jnp.where(sel, jnp.dot(q_ref[g], kt[g], preferred_element_type=F32), SOFTMAX_NEG)
               for g in range(KV_HEADS)]
        stats = []
        for g in range(KV_HEADS):
            m, l, _ = state[g]
            m_new = jnp.maximum(m, jnp.max(lgs[g], axis=1, keepdims=True))
            alpha = jnp.exp2(m - m_new)
            pr = jnp.exp2(lgs[g] - m_new)
            stats.append((m_new, alpha, alpha * l + jnp.sum(pr, axis=1, keepdims=True), pr.astype(BF16)))
        for g in range(KV_HEADS):
            m_new, alpha, l_new, pr = stats[g]
            state[g] = [m_new, l_new, alpha * state[g][2] + _nt_dot(pr, vt[g])]

    def compute(c, slot):
        attend([_chunk_keys_on_lanes(kbuf[slot, :, g]) for g in range(KV_HEADS)],
               [_chunk_keys_on_lanes(vbuf[slot, :, g]) for g in range(KV_HEADS)], s_ref[:, c * ck:(c + 1) * ck])

    _stream_chunks([poolk_ref, poolv_ref], [kbuf, vbuf], sems, pt_ref, n_chunks, pages_per_chunk, compute)
    attend(knew_ref[...], vnew_ref[...], s_ref[:, n_chunks * ck:])
    for g in range(KV_HEADS):
        o_ref[g] = state[g][2] / state[g][1]


def _sample_attend(page_table, q_bd, s3d, thr3d, knew, vnew, pool_k, pool_v, *, n_t, pages_per_chunk):
    db, n_pages = page_table.shape
    n_chunks = n_pages // pages_per_chunk
    rows = q_bd.shape[2]
    ncols = s3d.shape[2]
    per_b = lambda r, w: pl.BlockSpec((None, r, w), lambda b, pt: (b, 0, 0))
    per_b4 = lambda r, w: pl.BlockSpec((None, KV_HEADS, r, w), lambda b, pt: (b, 0, 0, 0))
    grid_spec = pltpu.PrefetchScalarGridSpec(
        num_scalar_prefetch=1,
        grid=(db,),
        in_specs=[per_b4(rows, HEAD_DIM), per_b(n_t, ncols), per_b(n_t, LANES), per_b4(HEAD_DIM, PAGE_SIZE),
                  per_b4(HEAD_DIM, PAGE_SIZE), pl.BlockSpec(memory_space=pl.ANY), pl.BlockSpec(memory_space=pl.ANY)],
        out_specs=per_b4(rows, HEAD_DIM),
        scratch_shapes=[pltpu.VMEM((2, pages_per_chunk, KV_HEADS, HEAD_DIM, PAGE_SIZE), F32),
                        pltpu.VMEM((2, pages_per_chunk, KV_HEADS, HEAD_DIM, PAGE_SIZE), F32),
                        pltpu.SemaphoreType.DMA((2,))],
    )
    return pl.pallas_call(
        functools.partial(_sample_attend_kernel, n_t=n_t, n_chunks=n_chunks, pages_per_chunk=pages_per_chunk),
        out_shape=jax.ShapeDtypeStruct((db, KV_HEADS, rows, HEAD_DIM), F32),
        grid_spec=grid_spec,
        compiler_params=_cparams("arbitrary"),
        name="sample_attend",
    )(page_table, q_bd, s3d, thr3d, knew, vnew, pool_k, pool_v)


def _retention_sample_kernel(rq_ref, rkt_ref, rv_ref, rg_ref, gain_ref, dfull_ref, qdec_ref, kdec_ref, cdec_ref,
                             st_in_ref, o_ref, st_out_ref, *, n_t, n_heads):
    b = pl.program_id(0)

    @pl.when(b == 0)
    def _():
        o_ref[...] = jnp.zeros_like(o_ref)

    n_rows = rq_ref.shape[0]
    row = lax.broadcasted_iota(jnp.int32, (n_rows, RET_DV), 0)
    mine = (row >= b * n_t) & (row < (b + 1) * n_t)
    for h in range(n_heads):
        cols = slice(h * RET_DV, (h + 1) * RET_DV)
        q = jnp.where(mine, rq_ref[:, cols], 0.0).astype(BF16)
        v = rv_ref[:, cols]
        kt = rkt_ref[cols, :]
        state = st_in_ref[h]
        inner = jnp.dot(q, kt, preferred_element_type=F32) * dfull_ref[h]
        o = jnp.dot(inner.astype(BF16), v, preferred_element_type=F32)
        o = o + jnp.dot(q, state.astype(BF16), preferred_element_type=F32) * qdec_ref[h]
        vd = jnp.where(mine, v.astype(F32) * kdec_ref[h], 0.0).astype(BF16)
        st_out_ref[h] = cdec_ref[h] * state + jnp.dot(kt, vd, preferred_element_type=F32)
        gated = _group_norm_gate(o, rg_ref[:, cols], gain_ref[:, cols])
        o_ref[:, cols] = jnp.where(mine, gated, o_ref[:, cols].astype(F32)).astype(o_ref.dtype)


def _retention_sample(rq, rkt, rv, rg, g_retn, state, *, db, n_t):
    n_rows, n_ret = rq.shape
    n_heads = n_ret // RET_DV
    decay, q_dec, k_dec, c_dec = _retention_tables(n_t, n_heads)
    same = np.kron(np.eye(db), np.ones((n_t, n_t)))
    dfull = jnp.asarray(same[None] * np.tile(decay, (1, db, db)), F32)
    qdec_b = jnp.asarray(np.broadcast_to(np.tile(q_dec, (1, db))[:, :, None], (n_heads, n_rows, RET_DV)), F32)
    kdec_b = jnp.asarray(np.broadcast_to(np.tile(k_dec, (1, db))[:, :, None], (n_heads, n_rows, RET_DV)), F32)
    cdec_b = jnp.asarray(np.broadcast_to(c_dec[:, None, None], (n_heads, 1, RET_DV)), F32)
    const = lambda shape: pl.BlockSpec(shape, lambda b: (0,) * len(shape))
    st_spec = pl.BlockSpec((None, n_heads, RET_DK, RET_DV), lambda b: (b, 0, 0, 0))
    return pl.pallas_call(
        functools.partial(_retention_sample_kernel, n_t=n_t, n_heads=n_heads),
        out_shape=(jax.ShapeDtypeStruct((n_rows, n_ret), BF16),
                   jax.ShapeDtypeStruct((db, n_heads, RET_DK, RET_DV), F32)),
        grid=(db,),
        in_specs=[const((n_rows, n_ret)), const((n_ret, n_rows)), const((n_rows, n_ret)), const((n_rows, n_ret)),
                  const((1, n_ret)), const((n_heads, n_rows, n_rows)), const((n_heads, n_rows, RET_DV)),
                  const((n_heads, n_rows, RET_DV)), const((n_heads, 1, RET_DV)), st_spec],
        out_specs=(const((n_rows, n_ret)), st_spec),
        compiler_params=_cparams("arbitrary"),
        name="retention_sample",
    )(rq, rkt, rv, rg, g_retn.reshape(1, n_ret), dfull, qdec_b, kdec_b, cdec_b, state)


def _mixer_dims(w_in, d_model):
    n_q = d_model // 2
    n_ret = d_model // 2
    n_kv = KV_HEADS * HEAD_DIM
    n_qi = IDX_HEADS * IDX_DIM
    assert n_kv == LANES and 2 * n_kv + n_q + n_qi + IDX_DIM + IDX_HEADS + 4 * n_ret == w_in.shape[1]
    return n_q, n_kv, n_qi, n_ret


def _tiles(seq, rows, n_pages):
    return dict(inproj=min(512, seq), dsa=min(256, seq), retention=min(512, seq), tail=min(256, rows),
                attend_pages=max(1, min(64, n_pages // 2)), score_pages=n_pages)


def _prompt_layer(x, p, weights, *, n_fast=21):
    g_mix, w_in, g_retn, w_out, g_ffn, w_ffn_in, w_ffn_out, g_ple, w_ple_gate, w_ple_proj, g_final = weights
    nb, seq, d = x.shape
    tiles = _tiles(seq, nb * seq, 0)
    dims = _mixer_dims(w_in, d)
    w_packed = _pack_w_in(w_in, *dims)
    x2d = x.reshape(nb * seq, d)
    (qt, kt, vt, qit, kit, wit, rq, rkt, rv, rg, kp, vtt, kip) = _inproj(
        x2d, np.arange(seq), g_mix, w_packed, seq=seq, tm=tiles["inproj"], dims=dims)
    topk = min(TOPK_MAX, seq // 4)
    attn = _dsa_prompt(kip, qit, wit, kp, qt, vtt, nb=nb, seq=seq, tq=tiles["dsa"], topk=topk, n_fast=n_fast)
    ret, st = _retention_prompt(rq, rkt, rv, rg, g_retn, nb=nb, seq=seq, tr=tiles["retention"])
    y = _tail(x2d, attn, ret, p.reshape(nb * seq, -1), w_out, g_ffn, w_ffn_in, w_ffn_out,
              g_ple, w_ple_gate, w_ple_proj, g_final, tm=tiles["tail"])
    heads_last = lambda a: a.reshape(nb, KV_HEADS, HEAD_DIM, seq).transpose(0, 3, 1, 2)
    return y.reshape(nb, seq, d), heads_last(kt), heads_last(vt), kit.transpose(0, 2, 1), st


def _sample_layer(x, p, pool_k, pool_v, pool_ik, state, page_table, weights, *, n_fast=24):
    g_mix, w_in, g_retn, w_out, g_ffn, w_ffn_in, w_ffn_out, g_ple, w_ple_gate, w_ple_proj, g_final = weights
    db, n_t, d = x.shape
    n_rows = db * n_t
    n_pages = page_table.shape[1]
    past = n_pages * PAGE_SIZE
    tiles = _tiles(n_rows, n_rows, n_pages)
    assert tiles["inproj"] == n_rows == tiles["tail"]
    dims = _mixer_dims(w_in, d)
    n_q = dims[0]
    w_packed = _pack_w_in(w_in, *dims)
    x2d = x.reshape(n_rows, d)
    pos = np.tile(past + np.arange(n_t), db)
    (qt, kt, vt, qit, kit, wit, rq, rkt, rv, rg, _, _, _) = _inproj(
        x2d, pos, g_mix, w_packed, seq=n_rows, tm=tiles["inproj"], dims=dims)
    q, qi, wi = qt[0].T, qit[0].T, wit[0].T

    def new_pages(at, heads):
        a = at[0].reshape(heads, -1, db, n_t).transpose(2, 0, 1, 3)
        return jnp.pad(a, ((0, 0), (0, 0), (0, 0), (0, PAGE_SIZE - n_t))).astype(BF16)
    group = n_q // HEAD_DIM // KV_HEADS
    qg = q.reshape(db, n_t, KV_HEADS, group, HEAD_DIM).transpose(0, 2, 3, 1, 4)
    qg = qg.reshape(db, KV_HEADS, group * n_t, HEAD_DIM)

    s3d = _sample_scores(page_table, qi.reshape(db, n_t * IDX_HEADS, IDX_DIM), wi.reshape(db, n_t * IDX_HEADS, 1),
                         new_pages(kit, 1)[:, 0], pool_ik.transpose(0, 2, 1), n_t=n_t,
                         pages_per_chunk=tiles["score_pages"])
    topk = min(TOPK_MAX, (past + n_t) // 4)
    s_sel, thr = _sample_select(s3d.reshape(n_rows, -1), topk=topk, n_fast=n_fast)
    o = _sample_attend(page_table, qg, s_sel.reshape(db, n_t, -1), thr.reshape(db, n_t, LANES),
                       new_pages(kt, KV_HEADS), new_pages(vt, KV_HEADS), pool_k.transpose(0, 2, 3, 1),
                       pool_v.transpose(0, 2, 3, 1), n_t=n_t, pages_per_chunk=tiles["attend_pages"])
    attn = o.reshape(db, KV_HEADS, group, n_t, HEAD_DIM).transpose(0, 3, 1, 2, 4).reshape(n_rows, n_q).astype(BF16)

    ret, st = _retention_sample(rq, rkt[0], rv, rg, g_retn, state, db=db, n_t=n_t)
    y = _tail(x2d, attn, ret, p.reshape(n_rows, -1), w_out, g_ffn, w_ffn_in, w_ffn_out,
              g_ple, w_ple_gate, w_ple_proj, g_final, tm=tiles["tail"])
    heads_last = lambda a: a[0].reshape(KV_HEADS, HEAD_DIM, db, n_t).transpose(2, 3, 0, 1)
    return y.reshape(db, n_t, d), heads_last(kt), heads_last(vt), kit[0].T.reshape(db, n_t, IDX_DIM), st


def kernel(x_prompt, x_sample, cache_k, cache_v, cache_idx_k, state_retn, page_table, p_prompt, p_sample, g_mix, w_in,
           g_retn, w_out, g_ffn, w_ffn_in, w_ffn_out, g_ple, w_ple_gate, w_ple_proj, g_final):
    depth = w_in.shape[0]
    assert depth == 1, "the final RMSNorm is fused into the single layer's tail kernel"
    weights = (g_mix[0], w_in[0], g_retn[0], w_out[0], g_ffn[0], w_ffn_in[0], w_ffn_out[0], g_ple[0], w_ple_gate[0],
               w_ple_proj[0], g_final)
    y_p, k_p, v_p, ik_p, st_p = _prompt_layer(x_prompt, p_prompt[0], weights)
    y_s, k_s, v_s, ik_s, st_s = _sample_layer(x_sample, p_sample[0], cache_k[0], cache_v[0], cache_idx_k[0],
                                              state_retn[0], page_table, weights)
    return (y_p, y_s, k_p[None], v_p[None], ik_p[None], st_p[None], k_s[None], v_s[None], ik_s[None], st_s[None])
```

```python
import functools

import jax
import jax.numpy as jnp
import numpy as np
from jax import lax
from jax.experimental import pallas as pl
from jax.experimental.pallas import tpu as pltpu

F32 = jnp.float32
BF16 = jnp.bfloat16

HEAD_DIM = 64
KV_HEADS = 2
IDX_HEADS = 8
IDX_DIM = 64
TOPK_MAX = 256
RET_DK = 128
RET_DV = 128
RET_CHUNK = 128
PAGE_SIZE = 128
ROPE_THETA = 10000.0
EPS = 1e-6
GN_EPS = 1e-5

LANES = 128
SUBLANES = 8
VMEM_LIMIT_BYTES = 56 * 1024 * 1024

Q_SCALE = HEAD_DIM ** -0.5 * 1.4426950408889634
ONES_ROWS = 16

NEG = -3.0e38
POS = 3.0e38
SOFTMAX_NEG = -1.0e30
NO_TIE = 1.0e9


def _cparams(*sem):
    return pltpu.CompilerParams(dimension_semantics=sem, vmem_limit_bytes=VMEM_LIMIT_BYTES)


def _const_spec(shape):
    zeros = (0,) * len(shape)
    return pl.BlockSpec(shape, lambda *_: zeros, pipeline_mode=pl.Buffered(1))


def _rope_tables(pos, dim):
    half = dim // 2
    inv = ROPE_THETA ** (-np.arange(half, dtype=np.float64) / half)
    ang = np.asarray(pos, np.float64)[:, None] * inv[None, :]
    cos, sin = np.cos(ang), np.sin(ang)
    reps = LANES // dim
    cos_t = np.tile(np.concatenate([cos, cos], axis=1), (1, reps))
    sin_t = np.tile(np.concatenate([-sin, sin], axis=1), (1, reps))
    return jnp.asarray(cos_t, F32), jnp.asarray(sin_t, F32)


def _swap_halves_64(x):
    lane = lax.broadcasted_iota(jnp.int32, x.shape, 1)
    first = (lane % 64) < 32
    return jnp.where(first, pltpu.roll(x, 96, 1), pltpu.roll(x, 32, 1))


def _pad_pair_slabs(x, swapped, low):
    return (jnp.where(low, x, 0.0), jnp.where(low, 0.0, swapped), jnp.where(low, swapped, 0.0), jnp.where(low, 0.0, x))


def _inproj_kernel(x_ref, g_ref, w_ref, c64_ref, s64_ref, c128_ref, s128_ref,
                   qt_ref, kt_ref, vt_ref, qit_ref, kit_ref, wit_ref, rq_ref, rkt_ref, rv_ref, rg_ref,
                   kp_ref, vtt_ref, kip_ref, *, n_q, n_qi, n_ret):
    x = x_ref[...]
    tm = x.shape[0]
    ms = jnp.mean(x * x, axis=-1, keepdims=True)
    a = ((x * lax.rsqrt(ms + EPS)) * g_ref[...]).astype(BF16)
    c64, s64 = c64_ref[...], s64_ref[...]
    c128, s128 = c128_ref[...], s128_ref[...]
    low = lax.broadcasted_iota(jnp.int32, (tm, LANES), 1) < HEAD_DIM

    col = [0]

    def segment(width):
        y = jnp.dot(a, w_ref[:, col[0]:col[0] + width], preferred_element_type=F32)
        col[0] += width
        return [y[:, j * LANES:(j + 1) * LANES] for j in range(width // LANES)]

    def rope64(y):
        return y * c64 + _swap_halves_64(y) * s64

    def rope128(y):
        return y * c128 + pltpu.roll(y, 64, 1) * s128

    for j, y in enumerate(segment(n_q)):
        qt_ref[j * LANES:(j + 1) * LANES, :] = (rope64(y) * Q_SCALE).T.astype(BF16)
    kk, vv = segment(2 * LANES)
    kk = rope64(kk)
    kt_ref[...] = kk.T
    for s, slab in enumerate(_pad_pair_slabs(kk, pltpu.roll(kk, HEAD_DIM, 1), low)):
        kp_ref[:, s * LANES:(s + 1) * LANES] = slab.astype(BF16)
    vvt = vv.T
    vt_ref[...] = vvt
    vtt_ref[...] = vvt.astype(BF16)
    for j, y in enumerate(segment(n_qi)):
        qit_ref[j * LANES:(j + 1) * LANES, :] = (rope64(y) * (IDX_DIM ** -0.5)).T.astype(BF16)
    y, _ = segment(2 * LANES)
    ki = rope64(y)
    kit_ref[...] = ki.T[:IDX_DIM, :]
    ki_lo, ki_hi, _, _ = _pad_pair_slabs(ki, pltpu.roll(ki, IDX_DIM, 1), low)
    kip_ref[:, 0:LANES] = ki_lo.astype(BF16)
    kip_ref[:, LANES:2 * LANES] = ki_hi.astype(BF16)
    wit_ref[...] = y.T[IDX_DIM:IDX_DIM + IDX_HEADS, :] * (IDX_HEADS ** -0.5)
    for j, y in enumerate(segment(n_ret)):
        rq_ref[:, j * LANES:(j + 1) * LANES] = rope128(y).astype(BF16)
    for j, y in enumerate(segment(n_ret)):
        rkt_ref[j * LANES:(j + 1) * LANES, :] = (rope128(y) * (RET_DK ** -0.5)).T.astype(BF16)
    for j, y in enumerate(segment(n_ret)):
        rv_ref[:, j * LANES:(j + 1) * LANES] = y.astype(BF16)
    for j, y in enumerate(segment(n_ret)):
        rg_ref[:, j * LANES:(j + 1) * LANES] = y


def _pack_w_in(w_in, n_q, n_kv, n_qi, n_ret):
    d = w_in.shape[0]
    sizes = (n_q, n_kv, n_kv, n_qi, IDX_DIM, IDX_HEADS, n_ret, n_ret, n_ret, n_ret)
    offs = np.concatenate([[0], np.cumsum(sizes)])
    seg = [w_in[:, offs[i]:offs[i + 1]] for i in range(len(sizes))]
    pad = jnp.zeros((d, 2 * LANES - IDX_DIM - IDX_HEADS), w_in.dtype)
    packed = jnp.concatenate(seg[:4] + [seg[4], seg[5], pad] + seg[6:], axis=1)
    return packed.astype(BF16)


def _inproj(x2d, pos, g_mix, w_packed, *, seq, tm, dims):
    n_q, n_kv, n_qi, n_ret = dims
    rows, d = x2d.shape
    nb = rows // seq
    spt = seq // tm
    c64, s64 = _rope_tables(pos, HEAD_DIM)
    c128, s128 = _rope_tables(pos, RET_DK)
    wcols = w_packed.shape[1]

    row = lambda w: pl.BlockSpec((tm, w), lambda i: (i, 0))
    tab = pl.BlockSpec((tm, LANES), lambda i: (i % spt, 0))
    tr = lambda h: pl.BlockSpec((None, h, tm), lambda i: (i // spt, 0, i % spt))
    out_shape = (
        jax.ShapeDtypeStruct((nb, n_q, seq), BF16),
        jax.ShapeDtypeStruct((nb, n_kv, seq), F32),
        jax.ShapeDtypeStruct((nb, n_kv, seq), F32),
        jax.ShapeDtypeStruct((nb, n_qi, seq), BF16),
        jax.ShapeDtypeStruct((nb, IDX_DIM, seq), F32),
        jax.ShapeDtypeStruct((nb, IDX_HEADS, seq), F32),
        jax.ShapeDtypeStruct((rows, n_ret), BF16),
        jax.ShapeDtypeStruct((nb, n_ret, seq), BF16),
        jax.ShapeDtypeStruct((rows, n_ret), BF16),
        jax.ShapeDtypeStruct((rows, n_ret), F32),
        jax.ShapeDtypeStruct((rows, 4 * n_kv), BF16),
        jax.ShapeDtypeStruct((nb, n_kv, seq), BF16),
        jax.ShapeDtypeStruct((rows, 2 * LANES), BF16),
    )
    out_specs = (tr(n_q), tr(n_kv), tr(n_kv), tr(n_qi), tr(IDX_DIM), tr(IDX_HEADS),
                 row(n_ret), tr(n_ret), row(n_ret), row(n_ret), row(4 * n_kv), tr(n_kv), row(2 * LANES))
    return pl.pallas_call(
        functools.partial(_inproj_kernel, n_q=n_q, n_qi=n_qi, n_ret=n_ret),
        out_shape=out_shape,
        grid=(rows // tm,),
        in_specs=[row(d), _const_spec((1, d)), _const_spec((d, wcols)), tab, tab, tab, tab],
        out_specs=out_specs,
        compiler_params=_cparams("parallel"),
        name="inproj",
    )(x2d, g_mix.reshape(1, d), w_packed, c64, s64, c128, s128)


N_ACC = 4


def _fold_keys(st_ref, nch, ck, init, fn, combine):
    def body(c, carries):
        carries = list(carries)
        base = pl.multiple_of(c * ck, ck)
        chunk = st_ref[pl.ds(base, ck), :]
        for j in range(ck // SUBLANES):
            blk = chunk[j * SUBLANES:(j + 1) * SUBLANES, :]
            carries[j % N_ACC] = fn(carries[j % N_ACC], blk, base + j * SUBLANES)
        return tuple(carries)
    carries = lax.fori_loop(0, nch, body, (init,) * N_ACC)
    out = carries[0]
    for other in carries[1:]:
        out = combine(out, other)
    return out


def _all_sublanes(x, op):
    for shift in (4, 2, 1):
        x = op(x, pltpu.roll(x, shift, 0))
    return x


def _any(mask):
    return jnp.max(jnp.where(mask, 1.0, 0.0)) > 0.5


def _count(st_ref, nch, ck, pred):
    nq = st_ref.shape[1]
    acc = _fold_keys(st_ref, nch, ck, jnp.zeros((SUBLANES, nq), F32),
                     lambda a, blk, row0: a + jnp.where(pred(blk, row0), 1.0, 0.0), jnp.add)
    return _all_sublanes(acc, jnp.add)


def _select_threshold(st_ref, thr_ref, tie_ref, nch, ck, rmin, rmax, topk, n_fast):
    nq = st_ref.shape[1]
    kf = float(topk)
    full = lambda v: jnp.full((SUBLANES, nq), v, F32)

    def fast_body(_, carry):
        lo, hi = carry
        mid = lo + (hi - lo) * 0.5
        up = _count(st_ref, nch, ck, lambda blk, _r: blk > mid) >= kf
        return jnp.where(up, mid, lo), jnp.where(up, hi, mid)

    lo, hi = lax.fori_loop(0, n_fast, fast_body, (rmin, rmax))
    unres = _count(st_ref, nch, ck, lambda blk, _r: blk >= lo) > kf
    thr_ref[...] = lo

    @pl.when(_any(unres))
    def _exact():
        pair_minmax = lambda x, y: (jnp.minimum(x[0], y[0]), jnp.maximum(x[1], y[1]))
        a, b = _fold_keys(
            st_ref, nch, ck, (full(POS), full(NEG)),
            lambda cr, blk, _r: (jnp.minimum(cr[0], jnp.where(blk >= lo, blk, POS)),
                                 jnp.maximum(cr[1], jnp.where(blk <= hi, blk, NEG))),
            pair_minmax)
        lo2 = jnp.where(unres, _all_sublanes(a, jnp.minimum), lo)
        ub = jnp.where(unres, _all_sublanes(b, jnp.maximum), lo)

        def body(carry):
            lo2, ub, _ = carry
            mid = lo2 + (ub - lo2) * 0.5
            mid = jnp.where(mid >= ub, lo2, mid)

            def step(cr, blk, _r):
                gt = blk > mid
                return (cr[0] + jnp.where(gt, 1.0, 0.0), jnp.minimum(cr[1], jnp.where(gt, blk, POS)),
                        jnp.maximum(cr[2], jnp.where(gt, NEG, blk)))
            cnt, a, b = _fold_keys(st_ref, nch, ck, (full(0.0), full(POS), full(NEG)), step,
                                   lambda x, y: (x[0] + y[0],) + pair_minmax(x[1:], y[1:]))
            up = _all_sublanes(cnt, jnp.add) >= kf
            active = lo2 < ub
            lo2n = jnp.where(active & up, _all_sublanes(a, jnp.minimum), lo2)
            ubn = jnp.where(active & jnp.logical_not(up), _all_sublanes(b, jnp.maximum), ub)
            return lo2n, ubn, _any(lo2n < ubn).astype(jnp.int32)

        v, _, _ = lax.while_loop(lambda carry: carry[2] > 0, body, (lo2, ub, _any(lo2 < ub).astype(jnp.int32)))
        thr_ref[...] = jnp.where(unres, v, lo)
        n_ge, n_gt = _fold_keys(
            st_ref, nch, ck, (full(0.0), full(0.0)),
            lambda cr, blk, _r: (cr[0] + jnp.where(blk >= v, 1.0, 0.0), cr[1] + jnp.where(blk > v, 1.0, 0.0)),
            lambda x, y: (x[0] + y[0], x[1] + y[1]))
        tied = unres & (_all_sublanes(n_ge, jnp.add) > kf)

        @pl.when(_any(tied))
        def _ties():
            _drop_excess_ties(st_ref, tie_ref, nch, ck, v, tied, kf - _all_sublanes(n_gt, jnp.add))


def _drop_excess_ties(st_ref, tie_ref, nch, ck, v, tied, need):
    nq = st_ref.shape[1]
    full = lambda x: jnp.full((SUBLANES, nq), x, F32)
    nkeys = st_ref.shape[0]
    key_in_chunk = lax.broadcasted_iota(jnp.int32, (ck, nq), 0)
    rep = lambda x: jnp.broadcast_to(x[0:1, :], (ck, nq))
    v_c = rep(v)

    def each_chunk(fn):
        def body(c, _):
            base = pl.multiple_of(c * ck, ck)
            fn(pl.ds(base, ck), base)
            return 0
        lax.fori_loop(0, nch, body, 0)

    def mark(rows, row0):
        tie_ref[rows, :] = jnp.where(st_ref[rows, :] == v_c, (key_in_chunk + row0).astype(F32), NO_TIE)
    each_chunk(mark)

    n_j = int(np.ceil(np.log2(nkeys))) + 1
    need_t = jnp.where(tied, need, 0.0)
    need_max = jnp.max(need_t)

    def by_walk():
        def body(carry):
            j_prev, k = carry
            nxt = _fold_keys(tie_ref, nch, ck, full(NO_TIE),
                             lambda a, blk, _r: jnp.minimum(a, jnp.where(blk > j_prev, blk, NO_TIE)), jnp.minimum)
            return jnp.where(need_t > k, _all_sublanes(nxt, jnp.minimum), j_prev), k + 1.0
        return lax.while_loop(lambda carry: carry[1] < need_max, body, (full(-1.0), jnp.float32(0.0)))[0]

    def by_bisection():
        def body(_, carry):
            lo_j, hi_j = carry
            mid_j = jnp.floor((lo_j + hi_j) * 0.5)
            ok = _count(tie_ref, nch, ck, lambda blk, _r: blk <= mid_j) >= need
            return jnp.where(ok, lo_j, mid_j), jnp.where(ok, mid_j, hi_j)
        return lax.fori_loop(0, n_j, body, (full(-1.0), full(float(nkeys - 1))))[1]

    j_last = lax.cond(need_max <= float(n_j), by_walk, by_bisection)

    j_c = rep(jnp.where(tied, j_last, NO_TIE))

    def drop(rows, _row0):
        t = tie_ref[rows, :]
        s = st_ref[rows, :]
        st_ref[rows, :] = jnp.where(t > j_c, jnp.where(t < NO_TIE, NEG, s), s)
    each_chunk(drop)


def _rows_min(x):
    return jnp.min(x.reshape(x.shape[0] // SUBLANES, SUBLANES, x.shape[1]), axis=0)


def _rows_max(x):
    return jnp.max(x.reshape(x.shape[0] // SUBLANES, SUBLANES, x.shape[1]), axis=0)


def _rows_sum(x):
    return jnp.sum(x.reshape(x.shape[0] // SUBLANES, SUBLANES, x.shape[1]), axis=0)


def _dsa_prompt_kernel(kip_ref, qit_ref, wit_ref, kp_ref, qt_ref, vtt_ref, o_ref, st_ref, thr_ref,
                       m_ref, l_ref, acc_ref, lg_ref, pr_ref, tie_ref, *, tq, topk, n_fast, rs, cps):
    ck = tq
    i = pl.program_id(1)
    nch = i + 1
    n_iter = (nch + cps - 1) // cps
    n_pairs = qt_ref.shape[0] // LANES
    pairs_per_kv = n_pairs // KV_HEADS
    bcast = lambda row, n: jnp.broadcast_to(row, (n, tq))

    key_l = lax.broadcasted_iota(jnp.int32, (rs, tq), 0)
    qry_l = lax.broadcasted_iota(jnp.int32, (rs, tq), 1)
    w_rows = [bcast(wit_ref[h:h + 1, :], rs) for h in range(IDX_HEADS)]

    def score_chunk(c, mn, mx):
        base = pl.multiple_of(c * ck, ck)
        slack = jnp.where(c < i, ck, jnp.where(c == i, 0, -ck))
        for r in range(ck // rs):
            rows = pl.ds(base + r * rs, rs)
            k_lo = kip_ref[rows, 0:LANES]
            k_hi = kip_ref[rows, LANES:2 * LANES]
            acc = jnp.zeros((rs, tq), F32)
            for p in range(IDX_HEADS // 2):
                rhs = qit_ref[p * LANES:(p + 1) * LANES, :]
                acc = acc + jnp.maximum(jnp.dot(k_lo, rhs, preferred_element_type=F32), 0.0) * w_rows[2 * p]
                acc = acc + jnp.maximum(jnp.dot(k_hi, rhs, preferred_element_type=F32), 0.0) * w_rows[2 * p + 1]
            valid = key_l + r * rs <= qry_l + slack
            st_ref[rows, :] = jnp.where(valid, acc, NEG)
            mn = jnp.minimum(mn, _rows_min(jnp.where(valid, acc, POS)))
            mx = jnp.maximum(mx, _rows_max(jnp.where(valid, acc, NEG)))
        return mn, mx

    def score_body(k, carry):
        for u in range(cps):
            carry = score_chunk(cps * k + u, *carry)
        return carry

    mn, mx = lax.fori_loop(0, n_iter, score_body,
                           (jnp.full((SUBLANES, tq), POS, F32), jnp.full((SUBLANES, tq), NEG, F32)))
    _select_threshold(st_ref, thr_ref, tie_ref, nch, ck,
                      _all_sublanes(mn, jnp.minimum), _all_sublanes(mx, jnp.maximum), topk, n_fast)

    thr = bcast(thr_ref[0:1, :], ck)
    m_ref[...] = jnp.full(m_ref.shape, SOFTMAX_NEG, F32)
    l_ref[...] = jnp.zeros(l_ref.shape, F32)
    acc_ref[...] = jnp.zeros(acc_ref.shape, F32)

    slab_of = lambda h: 2 * (h // 2 // pairs_per_kv) + h % 2
    n_heads = 2 * n_pairs
    ones = jnp.ones((ONES_ROWS, ck), BF16)

    def att_body(k, _):
        chunk_rows = [pl.ds(pl.multiple_of((cps * k + u) * ck, ck), ck) for u in range(cps)]
        for u, rows in enumerate(chunk_rows):
            sel = st_ref[rows, :] >= thr
            for h in range(n_heads):
                slab = slab_of(h)
                kx = kp_ref[rows, slab * LANES:(slab + 1) * LANES]
                qt_pair = qt_ref[(h // 2) * LANES:(h // 2 + 1) * LANES, :]
                lg_ref[u, h] = jnp.where(sel, jnp.dot(kx, qt_pair, preferred_element_type=F32), SOFTMAX_NEG)
        alpha = {}
        for u in range(cps):
            for h in range(n_heads):
                lg = lg_ref[u, h]
                m_old = m_ref[h]
                m_new = jnp.maximum(m_old, _all_sublanes(_rows_max(lg), jnp.maximum))
                alpha[u, h] = jnp.exp2(m_old - m_new)
                m_ref[h] = m_new
                pr_ref[u, h] = jnp.exp2(lg - bcast(m_new[0:1, :], ck)).astype(BF16)
        for u, rows in enumerate(chunk_rows):
            for p in range(n_pairs):
                pv = []
                g = p // pairs_per_kv
                vt1 = jnp.concatenate([vtt_ref[g * HEAD_DIM:(g + 1) * HEAD_DIM, rows], ones], axis=0)
                for h in (2 * p, 2 * p + 1):
                    pv1 = jnp.dot(vt1, pr_ref[u, h], preferred_element_type=F32)
                    l_ref[h] = alpha[u, h] * l_ref[h] + pv1[HEAD_DIM:HEAD_DIM + SUBLANES, :]
                    pv.append(pv1[:HEAD_DIM, :])
                scale = jnp.concatenate([bcast(alpha[u, 2 * p][0:1, :], HEAD_DIM),
                                         bcast(alpha[u, 2 * p + 1][0:1, :], HEAD_DIM)], axis=0)
                acc_ref[p] = acc_ref[p] * scale + jnp.concatenate(pv, axis=0)
        return 0

    lax.fori_loop(0, n_iter, att_body, 0)
    for p in range(n_pairs):
        denom = jnp.concatenate([bcast(l_ref[2 * p, 0:1, :], HEAD_DIM), bcast(l_ref[2 * p + 1, 0:1, :], HEAD_DIM)],
                                axis=0)
        o_ref[:, p * LANES:(p + 1) * LANES] = (acc_ref[p] / denom).T.astype(o_ref.dtype)


def _dsa_prompt(kip, qit, wit, kp, qt, vtt, *, nb, seq, tq, topk, n_fast):
    n_q = qt.shape[1]
    spb = seq // tq
    cps = 2 if spb % 2 == 0 else 1
    per_b_rows = lambda a: pl.BlockSpec((seq, a.shape[1]), lambda b, i: (b, 0))
    q_cols = lambda a: pl.BlockSpec((None, a.shape[1], tq), lambda b, i: (b, 0, i))
    return pl.pallas_call(
        functools.partial(_dsa_prompt_kernel, tq=tq, topk=topk, n_fast=n_fast, rs=min(tq, 128), cps=cps),
        out_shape=jax.ShapeDtypeStruct((nb * seq, n_q), BF16),
        grid=(nb, spb),
        in_specs=[per_b_rows(kip), q_cols(qit), q_cols(wit), per_b_rows(kp), q_cols(qt),
                  pl.BlockSpec((None, vtt.shape[1], seq), lambda b, i: (b, 0, 0))],
        out_specs=pl.BlockSpec((tq, n_q), lambda b, i: (b * spb + i, 0)),
        scratch_shapes=[pltpu.VMEM((seq, tq), F32), pltpu.VMEM((SUBLANES, tq), F32),
                        pltpu.VMEM((n_q // HEAD_DIM, SUBLANES, tq), F32),
                        pltpu.VMEM((n_q // HEAD_DIM, SUBLANES, tq), F32),
                        pltpu.VMEM((n_q // LANES, LANES, tq), F32),
                        pltpu.VMEM((cps, n_q // HEAD_DIM, tq, tq), F32),
                        pltpu.VMEM((cps, n_q // HEAD_DIM, tq, tq), BF16),
                        pltpu.VMEM((seq, tq), F32)],
        compiler_params=_cparams("parallel", "arbitrary"),
        name="dsa_prompt",
    )(kip, qit, wit, kp, qt, vtt)


def _retention_tables(chunk, n_heads):
    log_g = np.log1p(-np.exp2(-5.0 - np.arange(n_heads, dtype=np.float64)))
    i = np.arange(chunk, dtype=np.float64)
    diff = i[:, None] - i[None, :]
    decay = np.where(diff >= 0, np.exp(log_g[:, None, None] * np.maximum(diff, 0.0)), 0.0)
    q_dec = np.exp(log_g[:, None] * (i[None, :] + 1.0))
    k_dec = np.exp(log_g[:, None] * (chunk - 1.0 - i)[None, :])
    c_dec = np.exp(log_g * chunk)
    return decay, q_dec, k_dec, c_dec


def _group_norm_gate(o, rg, gain):
    mu = jnp.mean(o, axis=-1, keepdims=True)
    var = jnp.mean(jnp.square(o - mu), axis=-1, keepdims=True)
    on = ((o - mu) * lax.rsqrt(var + GN_EPS)) * gain
    return jax.nn.silu(rg) * on


def _retention_prompt_kernel(rq_ref, rkt_ref, rv_ref, rg_ref, gain_ref, dmat_ref, qdec_ref, kdec_ref, cdec_ref,
                             o_ref, st_ref, state_ref, *, chunk, n_heads):
    j = pl.program_id(1)

    @pl.when(j == 0)
    def _():
        state_ref[...] = jnp.zeros_like(state_ref)

    n_cc = rq_ref.shape[0] // chunk
    dot = functools.partial(jnp.dot, preferred_element_type=F32)
    tiles = [(cc, h, slice(cc * chunk, (cc + 1) * chunk), slice(h * RET_DV, (h + 1) * RET_DV))
             for cc in range(n_cc) for h in range(n_heads)]
    inner, update = {}, {}
    for cc, h, rows, cols in tiles:
        kt = rkt_ref[cols, rows]
        inner[cc, h] = (dot(rq_ref[rows, cols], kt) * dmat_ref[h]).astype(BF16)
        update[cc, h] = dot((kt.astype(F32) * kdec_ref[h]).astype(BF16), rv_ref[rows, cols])
    cross = {}
    for h in range(n_heads):
        state = state_ref[h]
        for cc in range(n_cc):
            rows, cols = slice(cc * chunk, (cc + 1) * chunk), slice(h * RET_DV, (h + 1) * RET_DV)
            cross[cc, h] = dot(rq_ref[rows, cols], state.astype(BF16)) * qdec_ref[h]
            state = cdec_ref[h] * state + update[cc, h]
        state_ref[h] = state
    for cc, h, rows, cols in tiles:
        o = dot(inner[cc, h], rv_ref[rows, cols]) + cross[cc, h]
        o_ref[rows, cols] = _group_norm_gate(o, rg_ref[rows, cols], gain_ref[:, cols]).astype(o_ref.dtype)

    @pl.when(j == pl.num_programs(1) - 1)
    def _():
        st_ref[...] = state_ref[...]


def _retention_prompt(rq, rkt, rv, rg, g_retn, *, nb, seq, tr):
    n_ret = rq.shape[1]
    n_heads = n_ret // RET_DV
    chunk = RET_CHUNK
    decay, q_dec, k_dec, c_dec = _retention_tables(chunk, n_heads)
    decay = jnp.asarray(decay, F32)
    qdec_b = jnp.asarray(np.broadcast_to(q_dec[:, :, None], (n_heads, chunk, LANES)), F32)
    kdec_b = jnp.asarray(k_dec[:, None, :], F32)
    cdec_b = jnp.asarray(np.broadcast_to(c_dec[:, None, None], (n_heads, 1, LANES)), F32)
    spb = seq // tr
    row = lambda: pl.BlockSpec((tr, n_ret), lambda b, j: (b * spb + j, 0))
    return pl.pallas_call(
        functools.partial(_retention_prompt_kernel, chunk=chunk, n_heads=n_heads),
        out_shape=(jax.ShapeDtypeStruct((nb * seq, n_ret), BF16),
                   jax.ShapeDtypeStruct((nb, n_heads, RET_DK, RET_DV), F32)),
        grid=(nb, spb),
        in_specs=[row(), pl.BlockSpec((None, n_ret, tr), lambda b, j: (b, 0, j)), row(), row(),
                  _const_spec((1, n_ret)), _const_spec((n_heads, chunk, chunk)),
                  _const_spec((n_heads, chunk, LANES)), _const_spec((n_heads, 1, chunk)),
                  _const_spec((n_heads, 1, LANES))],
        out_specs=(row(), pl.BlockSpec((None, n_heads, RET_DK, RET_DV), lambda b, j: (b, 0, 0, 0))),
        scratch_shapes=[pltpu.VMEM((n_heads, RET_DK, RET_DV), F32)],
        compiler_params=_cparams("parallel", "arbitrary"),
        name="retention_prompt",
    )(rq, rkt, rv, rg, g_retn.reshape(1, n_ret), decay, qdec_b, kdec_b, cdec_b)


def _rms(x, g):
    return (x * lax.rsqrt(jnp.mean(x * x, axis=-1, keepdims=True) + EPS)) * g


def _tail_kernel(h_ref, attn_ref, ret_ref, p_ref, wo_ref, g_ffn_ref, w_ffn_in_ref, w_down_ref,
                 g_ple_ref, w_pg_ref, w_pp_ref, g_fin_ref, y_ref):
    dot = functools.partial(jnp.dot, preferred_element_type=F32)
    n_attn = attn_ref.shape[1]
    d_ff = w_down_ref.shape[0]
    h = h_ref[...] + dot(attn_ref[...], wo_ref[:n_attn, :]) + dot(ret_ref[...], wo_ref[n_attn:, :])
    f = _rms(h, g_ffn_ref[...]).astype(BF16)
    act = (jax.nn.silu(dot(f, w_ffn_in_ref[:, :d_ff])) * dot(f, w_ffn_in_ref[:, d_ff:])).astype(BF16)
    h = h + dot(act, w_down_ref[...])
    gate = jax.nn.sigmoid(dot(_rms(h, g_ple_ref[...]).astype(BF16), w_pg_ref[...]))
    h = h + gate * dot(p_ref[...].astype(BF16), w_pp_ref[...])
    y_ref[...] = _rms(h, g_fin_ref[...])


def _tail(h2d, attn, ret, p2d, w_out, g_ffn, w_ffn_in, w_ffn_out, g_ple, w_ple_gate, w_ple_proj, g_final, *, tm):
    rows, d = h2d.shape
    n_attn = attn.shape[1]
    assert w_ffn_in.shape[1] == 2 * w_ffn_out.shape[0] and w_ffn_out.shape[0] % LANES == 0
    ws = [w_out.astype(BF16), g_ffn.reshape(1, d), w_ffn_in.astype(BF16), w_ffn_out.astype(BF16),
          g_ple.reshape(1, d), w_ple_gate.astype(BF16), w_ple_proj.astype(BF16), g_final.reshape(1, d)]
    row = lambda w: pl.BlockSpec((tm, w), lambda i: (i, 0))
    return pl.pallas_call(
        _tail_kernel,
        out_shape=jax.ShapeDtypeStruct((rows, d), F32),
        grid=(rows // tm,),
        in_specs=[row(d), row(n_attn), row(ret.shape[1]), row(p2d.shape[1])] + [_const_spec(w.shape) for w in ws],
        out_specs=row(d),
        compiler_params=_cparams("parallel"),
        name="tail",
    )(h2d, attn, ret, p2d, *ws)


def _page_copies(pools, bufs, sems, pt_ref, b, c, slot, pages_per_chunk):
    copies = []
    for pool, buf in zip(pools, bufs):
        for p in range(pages_per_chunk):
            page = pt_ref[b, c * pages_per_chunk + p]
            copies.append(pltpu.make_async_copy(pool.at[page], buf.at[slot, p], sems.at[slot]))
    return copies


def _chunk_keys_on_lanes(pages):
    return jnp.concatenate([pages[p] for p in range(pages.shape[0])], axis=1).astype(BF16)


def _stream_chunks(pools, bufs, sems, pt_ref, n_chunks, pages_per_chunk, compute):
    b = pl.program_id(0)
    nb = pl.num_programs(0)

    def start(bb, cc, slot):
        for cp in _page_copies(pools, bufs, sems, pt_ref, bb, cc, slot, pages_per_chunk):
            cp.start()

    @pl.when(b == 0)
    def _():
        start(0, 0, 0)

    first = 0 if n_chunks % 2 == 0 else b % 2
    for c in range(n_chunks):
        slot = (first + c) % 2
        if c + 1 < n_chunks:
            start(b, c + 1, 1 - slot)
        else:
            @pl.when(b + 1 < nb)
            def _():
                start(b + 1, 0, 1 - slot)
        for cp in _page_copies(pools, bufs, sems, pt_ref, b, c, slot, pages_per_chunk):
            cp.wait()
        compute(c, slot)


def _tile_lanes(x, width):
    return x if width == LANES else jnp.concatenate([x] * (width // LANES), axis=1)


def _nt_dot(a, b):
    return lax.dot_general(a, b, (((1,), (1,)), ((), ())), preferred_element_type=F32)


def _head_sum(x, n_t):
    return jnp.sum(x.reshape(n_t, IDX_HEADS, x.shape[-1]), axis=1)


def _sample_scores_kernel(pt_ref, qi_ref, wi_ref, kinew_ref, pool_ref, s_ref, buf, sems,
                          *, n_t, n_chunks, pages_per_chunk, pages_per_dot):
    ck = pages_per_chunk * PAGE_SIZE
    qi = qi_ref[...]
    w = wi_ref[...]

    def compute(c, slot):
        for p0 in range(0, pages_per_chunk, pages_per_dot):
            kit = _chunk_keys_on_lanes(buf[slot, p0:p0 + pages_per_dot])
            a = jnp.dot(qi, kit, preferred_element_type=F32)
            col0 = c * ck + p0 * PAGE_SIZE
            s_ref[:, col0:col0 + pages_per_dot * PAGE_SIZE] = _head_sum(jnp.maximum(a, 0.0) * w, n_t)

    _stream_chunks([pool_ref], [buf], sems, pt_ref, n_chunks, pages_per_chunk, compute)

    a = jnp.dot(qi, kinew_ref[...], preferred_element_type=F32)
    s_new = _head_sum(jnp.maximum(a, 0.0) * w, n_t)
    t_q = lax.broadcasted_iota(jnp.int32, s_new.shape, 0)
    t_k = lax.broadcasted_iota(jnp.int32, s_new.shape, 1)
    s_ref[:, n_chunks * ck:] = jnp.where(t_k <= t_q, s_new, NEG)


def _sample_scores(page_table, qi_s, wi_s, kinew, pool_ik, *, n_t, pages_per_chunk):
    db, n_pages = page_table.shape
    n_chunks = n_pages // pages_per_chunk
    past = n_pages * PAGE_SIZE
    rows = n_t * IDX_HEADS
    grid_spec = pltpu.PrefetchScalarGridSpec(
        num_scalar_prefetch=1,
        grid=(db,),
        in_specs=[pl.BlockSpec((None, rows, IDX_DIM), lambda b, pt: (b, 0, 0)),
                  pl.BlockSpec((None, rows, 1), lambda b, pt: (b, 0, 0)),
                  pl.BlockSpec((None, IDX_DIM, PAGE_SIZE), lambda b, pt: (b, 0, 0)),
                  pl.BlockSpec(memory_space=pl.ANY)],
        out_specs=pl.BlockSpec((None, n_t, past + PAGE_SIZE), lambda b, pt: (b, 0, 0)),
        scratch_shapes=[pltpu.VMEM((2, pages_per_chunk, IDX_DIM, PAGE_SIZE), F32),
                        pltpu.SemaphoreType.DMA((2,))],
    )
    return pl.pallas_call(
        functools.partial(_sample_scores_kernel, n_t=n_t, n_chunks=n_chunks, pages_per_chunk=pages_per_chunk,
                          pages_per_dot=min(16, pages_per_chunk)),
        out_shape=jax.ShapeDtypeStruct((db, n_t, past + PAGE_SIZE), F32),
        grid_spec=grid_spec,
        compiler_params=_cparams("arbitrary"),
        name="sample_scores",
    )(page_table, qi_s, wi_s, kinew, pool_ik)


def _sample_select_kernel(s_in_ref, s_out_ref, thr_ref, st_ref, thr8_ref, tie_ref, *, ck, topk, n_fast):
    nq, nkeys = s_in_ref.shape
    nch = nkeys // ck + 0 * pl.program_id(0)
    chunk = lambda c: pl.ds(pl.multiple_of(c * ck, ck), ck)

    def to_lanes(c, _):
        st_ref[chunk(c), :] = s_in_ref[:, chunk(c)].T
        return 0
    lax.fori_loop(0, nch, to_lanes, 0)

    mn, mx = _fold_keys(
        st_ref, nch, ck, (jnp.full((SUBLANES, nq), POS, F32), jnp.full((SUBLANES, nq), NEG, F32)),
        lambda cr, blk, _r: (jnp.minimum(cr[0], jnp.where(blk > 0.5 * NEG, blk, POS)), jnp.maximum(cr[1], blk)),
        lambda x, y: (jnp.minimum(x[0], y[0]), jnp.maximum(x[1], y[1])))
    _select_threshold(st_ref, thr8_ref, tie_ref, nch, ck, _all_sublanes(mn, jnp.minimum),
                      _all_sublanes(mx, jnp.maximum), topk, n_fast)
    def to_rows(c, _):
        s_out_ref[:, chunk(c)] = st_ref[chunk(c), :].T
        return 0
    lax.fori_loop(0, nch, to_rows, 0)
    thr_ref[...] = jnp.broadcast_to(thr8_ref[0:1, :], (LANES, nq)).T


def _sample_select(s2d, *, topk, n_fast):
    rows, ncols = s2d.shape
    ck = max(w for w in (LANES, 2 * LANES, 3 * LANES, 4 * LANES) if ncols % w == 0)
    full = pl.BlockSpec((rows, ncols), lambda i: (0, 0), pipeline_mode=pl.Buffered(1))
    return pl.pallas_call(
        functools.partial(_sample_select_kernel, ck=ck, topk=topk, n_fast=n_fast),
        out_shape=(jax.ShapeDtypeStruct((rows, ncols), F32), jax.ShapeDtypeStruct((rows, LANES), F32)),
        grid=(1,),
        in_specs=[full],
        out_specs=(full, pl.BlockSpec((rows, LANES), lambda i: (0, 0))),
        scratch_shapes=[pltpu.VMEM((ncols, rows), F32), pltpu.VMEM((SUBLANES, rows), F32),
                        pltpu.VMEM((ncols, rows), F32)],
        compiler_params=_cparams("arbitrary"),
        name="sample_select",
    )(s2d)


def _sample_attend_kernel(pt_ref, q_ref, s_ref, thr_ref, knew_ref, vnew_ref, poolk_ref, poolv_ref, o_ref,
                          kbuf, vbuf, sems, *, n_t, n_chunks, pages_per_chunk):
    ck = pages_per_chunk * PAGE_SIZE
    rows = q_ref.shape[1]
    reps = rows // n_t
    thr = jnp.concatenate([thr_ref[...]] * reps, axis=0)
    state = [[jnp.full((rows, 1), SOFTMAX_NEG, F32), jnp.zeros((rows, 1), F32), jnp.zeros((rows, HEAD_DIM), F32)]
             for _ in range(KV_HEADS)]

    def attend(kt, vt, s_blk):
        width = s_blk.shape[1]
        sel = jnp.concatenate([s_blk] * reps, axis=0) >= _tile_lanes(thr, width)
        lgs = [jnp.where(sel, jnp.dot(q_ref[g], kt[g], preferred_element_type=F32), SOFTMAX_NEG)
               for g in range(KV_HEADS)]
        stats = []
        for g in range(KV_HEADS):
            m, l, _ = state[g]
            m_new = jnp.maximum(m, jnp.max(lgs[g], axis=1, keepdims=True))
            alpha = jnp.exp2(m - m_new)
            pr = jnp.exp2(lgs[g] - m_new)
            stats.append((m_new, alpha, alpha * l + jnp.sum(pr, axis=1, keepdims=True), pr.astype(BF16)))
        for g in range(KV_HEADS):
            m_new, alpha, l_new, pr = stats[g]
            state[g] = [m_new, l_new, alpha * state[g][2] + _nt_dot(pr, vt[g])]

    def compute(c, slot):
        attend([_chunk_keys_on_lanes(kbuf[slot, :, g]) for g in range(KV_HEADS)],
               [_chunk_keys_on_lanes(vbuf[slot, :, g]) for g in range(KV_HEADS)], s_ref[:, c * ck:(c + 1) * ck])

    _stream_chunks([poolk_ref, poolv_ref], [kbuf, vbuf], sems, pt_ref, n_chunks, pages_per_chunk, compute)
    attend(knew_ref[...], vnew_ref[...], s_ref[:, n_chunks * ck:])
    for g in range(KV_HEADS):
        o_ref[g] = state[g][2] / state[g][1]


def _sample_attend(page_table, q_bd, s3d, thr3d, knew, vnew, pool_k, pool_v, *, n_t, pages_per_chunk):
    db, n_pages = page_table.shape
    n_chunks = n_pages // pages_per_chunk
    rows = q_bd.shape[2]
    ncols = s3d.shape[2]
    per_b = lambda r, w: pl.BlockSpec((None, r, w), lambda b, pt: (b, 0, 0))
    per_b4 = lambda r, w: pl.BlockSpec((None, KV_HEADS, r, w), lambda b, pt: (b, 0, 0, 0))
    grid_spec = pltpu.PrefetchScalarGridSpec(
        num_scalar_prefetch=1,
        grid=(db,),
        in_specs=[per_b4(rows, HEAD_DIM), per_b(n_t, ncols), per_b(n_t, LANES), per_b4(HEAD_DIM, PAGE_SIZE),
                  per_b4(HEAD_DIM, PAGE_SIZE), pl.BlockSpec(memory_space=pl.ANY), pl.BlockSpec(memory_space=pl.ANY)],
        out_specs=per_b4(rows, HEAD_DIM),
        scratch_shapes=[pltpu.VMEM((2, pages_per_chunk, KV_HEADS, HEAD_DIM, PAGE_SIZE), F32),
                        pltpu.VMEM((2, pages_per_chunk, KV_HEADS, HEAD_DIM, PAGE_SIZE), F32),
                        pltpu.SemaphoreType.DMA((2,))],
    )
    return pl.pallas_call(
        functools.partial(_sample_attend_kernel, n_t=n_t, n_chunks=n_chunks, pages_per_chunk=pages_per_chunk),
        out_shape=jax.ShapeDtypeStruct((db, KV_HEADS, rows, HEAD_DIM), F32),
        grid_spec=grid_spec,
        compiler_params=_cparams("arbitrary"),
        name="sample_attend",
    )(page_table, q_bd, s3d, thr3d, knew, vnew, pool_k, pool_v)


def _retention_sample_kernel(rq_ref, rkt_ref, rv_ref, rg_ref, gain_ref, dfull_ref, qdec_ref, kdec_ref, cdec_ref,
                             st_in_ref, o_ref, st_out_ref, *, n_t, n_heads):
    b = pl.program_id(0)

    @pl.when(b == 0)
    def _():
        o_ref[...] = jnp.zeros_like(o_ref)

    n_rows = rq_ref.shape[0]
    row = lax.broadcasted_iota(jnp.int32, (n_rows, RET_DV), 0)
    mine = (row >= b * n_t) & (row < (b + 1) * n_t)
    for h in range(n_heads):
        cols = slice(h * RET_DV, (h + 1) * RET_DV)
        q = jnp.where(mine, rq_ref[:, cols], 0.0).astype(BF16)
        v = rv_ref[:, cols]
        kt = rkt_ref[cols, :]
        state = st_in_ref[h]
        inner = jnp.dot(q, kt, preferred_element_type=F32) * dfull_ref[h]
        o = jnp.dot(inner.astype(BF16), v, preferred_element_type=F32)
        o = o + jnp.dot(q, state.astype(BF16), preferred_element_type=F32) * qdec_ref[h]
        vd = jnp.where(mine, v.astype(F32) * kdec_ref[h], 0.0).astype(BF16)
        st_out_ref[h] = cdec_ref[h] * state + jnp.dot(kt, vd, preferred_element_type=F32)
        gated = _group_norm_gate(o, rg_ref[:, cols], gain_ref[:, cols])
        o_ref[:, cols] = jnp.where(mine, gated, o_ref[:, cols].astype(F32)).astype(o_ref.dtype)


def _retention_sample(rq, rkt, rv, rg, g_retn, state, *, db, n_t):
    n_rows, n_ret = rq.shape
    n_heads = n_ret // RET_DV
    decay, q_dec, k_dec, c_dec = _retention_tables(n_t, n_heads)
    same = np.kron(np.eye(db), np.ones((n_t, n_t)))
    dfull = jnp.asarray(same[None] * np.tile(decay, (1, db, db)), F32)
    qdec_b = jnp.asarray(np.broadcast_to(np.tile(q_dec, (1, db))[:, :, None], (n_heads, n_rows, RET_DV)), F32)
    kdec_b = jnp.asarray(np.broadcast_to(np.tile(k_dec, (1, db))[:, :, None], (n_heads, n_rows, RET_DV)), F32)
    cdec_b = jnp.asarray(np.broadcast_to(c_dec[:, None, None], (n_heads, 1, RET_DV)), F32)
    const = lambda shape: pl.BlockSpec(shape, lambda b: (0,) * len(shape))
    st_spec = pl.BlockSpec((None, n_heads, RET_DK, RET_DV), lambda b: (b, 0, 0, 0))
    return pl.pallas_call(
        functools.partial(_retention_sample_kernel, n_t=n_t, n_heads=n_heads),
        out_shape=(jax.ShapeDtypeStruct((n_rows, n_ret), BF16),
                   jax.ShapeDtypeStruct((db, n_heads, RET_DK, RET_DV), F32)),
        grid=(db,),
        in_specs=[const((n_rows, n_ret)), const((n_ret, n_rows)), const((n_rows, n_ret)), const((n_rows, n_ret)),
                  const((1, n_ret)), const((n_heads, n_rows, n_rows)), const((n_heads, n_rows, RET_DV)),
                  const((n_heads, n_rows, RET_DV)), const((n_heads, 1, RET_DV)), st_spec],
        out_specs=(const((n_rows, n_ret)), st_spec),
        compiler_params=_cparams("arbitrary"),
        name="retention_sample",
    )(rq, rkt, rv, rg, g_retn.reshape(1, n_ret), dfull, qdec_b, kdec_b, cdec_b, state)


def _mixer_dims(w_in, d_model):
    n_q = d_model // 2
    n_ret = d_model // 2
    n_kv = KV_HEADS * HEAD_DIM
    n_qi = IDX_HEADS * IDX_DIM
    assert n_kv == LANES and 2 * n_kv + n_q + n_qi + IDX_DIM + IDX_HEADS + 4 * n_ret == w_in.shape[1]
    return n_q, n_kv, n_qi, n_ret


def _tiles(seq, rows, n_pages):
    return dict(inproj=min(512, seq), dsa=min(256, seq), retention=min(512, seq), tail=min(256, rows),
                attend_pages=max(1, min(64, n_pages // 2)), score_pages=n_pages)


def _prompt_layer(x, p, weights, *, n_fast=21):
    g_mix, w_in, g_retn, w_out, g_ffn, w_ffn_in, w_ffn_out, g_ple, w_ple_gate, w_ple_proj, g_final = weights
    nb, seq, d = x.shape
    tiles = _tiles(seq, nb * seq, 0)
    dims = _mixer_dims(w_in, d)
    w_packed = _pack_w_in(w_in, *dims)
    x2d = x.reshape(nb * seq, d)
    (qt, kt, vt, qit, kit, wit, rq, rkt, rv, rg, kp, vtt, kip) = _inproj(
        x2d, np.arange(seq), g_mix, w_packed, seq=seq, tm=tiles["inproj"], dims=dims)
    topk = min(TOPK_MAX, seq // 4)
    attn = _dsa_prompt(kip, qit, wit, kp, qt, vtt, nb=nb, seq=seq, tq=tiles["dsa"], topk=topk, n_fast=n_fast)
    ret, st = _retention_prompt(rq, rkt, rv, rg, g_retn, nb=nb, seq=seq, tr=tiles["retention"])
    y = _tail(x2d, attn, ret, p.reshape(nb * seq, -1), w_out, g_ffn, w_ffn_in, w_ffn_out,
              g_ple, w_ple_gate, w_ple_proj, g_final, tm=tiles["tail"])
    heads_last = lambda a: a.reshape(nb, KV_HEADS, HEAD_DIM, seq).transpose(0, 3, 1, 2)
    return y.reshape(nb, seq, d), heads_last(kt), heads_last(vt), kit.transpose(0, 2, 1), st


def _sample_layer(x, p, pool_k, pool_v, pool_ik, state, page_table, weights, *, n_fast=24):
    g_mix, w_in, g_retn, w_out, g_ffn, w_ffn_in, w_ffn_out, g_ple, w_ple_gate, w_ple_proj, g_final = weights
    db, n_t, d = x.shape
    n_rows = db * n_t
    n_pages = page_table.shape[1]
    past = n_pages * PAGE_SIZE
    tiles = _tiles(n_rows, n_rows, n_pages)
    assert tiles["inproj"] == n_rows == tiles["tail"]
    dims = _mixer_dims(w_in, d)
    n_q = dims[0]
    w_packed = _pack_w_in(w_in, *dims)
    x2d = x.reshape(n_rows, d)
    pos = np.tile(past + np.arange(n_t), db)
    (qt, kt, vt, qit, kit, wit, rq, rkt, rv, rg, _, _, _) = _inproj(
        x2d, pos, g_mix, w_packed, seq=n_rows, tm=tiles["inproj"], dims=dims)
    q, qi, wi = qt[0].T, qit[0].T, wit[0].T

    def new_pages(at, heads):
        a = at[0].reshape(heads, -1, db, n_t).transpose(2, 0, 1, 3)
        return jnp.pad(a, ((0, 0), (0, 0), (0, 0), (0, PAGE_SIZE - n_t))).astype(BF16)
    group = n_q // HEAD_DIM // KV_HEADS
    qg = q.reshape(db, n_t, KV_HEADS, group, HEAD_DIM).transpose(0, 2, 3, 1, 4)
    qg = qg.reshape(db, KV_HEADS, group * n_t, HEAD_DIM)

    s3d = _sample_scores(page_table, qi.reshape(db, n_t * IDX_HEADS, IDX_DIM), wi.reshape(db, n_t * IDX_HEADS, 1),
                         new_pages(kit, 1)[:, 0], pool_ik.transpose(0, 2, 1), n_t=n_t,
                         pages_per_chunk=tiles["score_pages"])
    topk = min(TOPK_MAX, (past + n_t) // 4)
    s_sel, thr = _sample_select(s3d.reshape(n_rows, -1), topk=topk, n_fast=n_fast)
    o = _sample_attend(page_table, qg, s_sel.reshape(db, n_t, -1), thr.reshape(db, n_t, LANES),
                       new_pages(kt, KV_HEADS), new_pages(vt, KV_HEADS), pool_k.transpose(0, 2, 3, 1),
                       pool_v.transpose(0, 2, 3, 1), n_t=n_t, pages_per_chunk=tiles["attend_pages"])
    attn = o.reshape(db, KV_HEADS, group, n_t, HEAD_DIM).transpose(0, 3, 1, 2, 4).reshape(n_rows, n_q).astype(BF16)

    ret, st = _retention_sample(rq, rkt[0], rv, rg, g_retn, state, db=db, n_t=n_t)
    y = _tail(x2d, attn, ret, p.reshape(n_rows, -1), w_out, g_ffn, w_ffn_in, w_ffn_out,
              g_ple, w_ple_gate, w_ple_proj, g_final, tm=tiles["tail"])
    heads_last = lambda a: a[0].reshape(KV_HEADS, HEAD_DIM, db, n_t).transpose(2, 3, 0, 1)
    return y.reshape(db, n_t, d), heads_last(kt), heads_last(vt), kit[0].T.reshape(db, n_t, IDX_DIM), st


def kernel(x_prompt, x_sample, cache_k, cache_v, cache_idx_k, state_retn, page_table, p_prompt, p_sample, g_mix, w_in,
           g_retn, w_out, g_ffn, w_ffn_in, w_ffn_out, g_ple, w_ple_gate, w_ple_proj, g_final):
    depth = w_in.shape[0]
    assert depth == 1, "the final RMSNorm is fused into the single layer's tail kernel"
    weights = (g_mix[0], w_in[0], g_retn[0], w_out[0], g_ffn[0], w_ffn_in[0], w_ffn_out[0], g_ple[0], w_ple_gate[0],
               w_ple_proj[0], g_final)
    y_p, k_p, v_p, ik_p, st_p = _prompt_layer(x_prompt, p_prompt[0], weights)
    y_s, k_s, v_s, ik_s, st_s = _sample_layer(x_sample, p_sample[0], cache_k[0], cache_v[0], cache_idx_k[0],
                                              state_retn[0], page_table, weights)
    return (y_p, y_s, k_p[None], v_p[None], ik_p[None], st_p[None], k_s[None], v_s[None], ik_s[None], st_s[None])
```

```python
import functools

import jax
import jax.numpy as jnp
import numpy as np
from jax import lax
from jax.experimental import pallas as pl
from jax.experimental.pallas import tpu as pltpu

F32 = jnp.float32
BF16 = jnp.bfloat16

HEAD_DIM = 64
KV_HEADS = 2
IDX_HEADS = 8
IDX_DIM = 64
TOPK_MAX = 256
RET_DK = 128
RET_DV = 128
RET_CHUNK = 128
PAGE_SIZE = 128
ROPE_THETA = 10000.0
EPS = 1e-6
GN_EPS = 1e-5

LANES = 128
SUBLANES = 8
VMEM_LIMIT_BYTES = 56 * 1024 * 1024

Q_SCALE = HEAD_DIM ** -0.5 * 1.4426950408889634
ONES_ROWS = 16

NEG = -3.0e38
POS = 3.0e38
SOFTMAX_NEG = -1.0e30
NO_TIE = 1.0e9


def _cparams(*sem):
    return pltpu.CompilerParams(dimension_semantics=sem, vmem_limit_bytes=VMEM_LIMIT_BYTES)


def _const_spec(shape):
    zeros = (0,) * len(shape)
    return pl.BlockSpec(shape, lambda *_: zeros, pipeline_mode=pl.Buffered(1))


def _rope_tables(pos, dim):
    half = dim // 2
    inv = ROPE_THETA ** (-np.arange(half, dtype=np.float64) / half)
    ang = np.asarray(pos, np.float64)[:, None] * inv[None, :]
    cos, sin = np.cos(ang), np.sin(ang)
    reps = LANES // dim
    cos_t = np.tile(np.concatenate([cos, cos], axis=1), (1, reps))
    sin_t = np.tile(np.concatenate([-sin, sin], axis=1), (1, reps))
    return jnp.asarray(cos_t, F32), jnp.asarray(sin_t, F32)


def _swap_halves_64(x):
    lane = lax.broadcasted_iota(jnp.int32, x.shape, 1)
    first = (lane % 64) < 32
    return jnp.where(first, pltpu.roll(x, 96, 1), pltpu.roll(x, 32, 1))


def _pad_pair_slabs(x, swapped, low):
    return (jnp.where(low, x, 0.0), jnp.where(low, 0.0, swapped), jnp.where(low, swapped, 0.0), jnp.where(low, 0.0, x))


def _inproj_kernel(x_ref, g_ref, w_ref, c64_ref, s64_ref, c128_ref, s128_ref,
                   qt_ref, kt_ref, vt_ref, qit_ref, kit_ref, wit_ref, rq_ref, rkt_ref, rv_ref, rg_ref,
                   kp_ref, vtt_ref, kip_ref, *, n_q, n_qi, n_ret):
    x = x_ref[...]
    tm = x.shape[0]
    ms = jnp.mean(x * x, axis=-1, keepdims=True)
    a = ((x * lax.rsqrt(ms + EPS)) * g_ref[...]).astype(BF16)
    c64, s64 = c64_ref[...], s64_ref[...]
    c128, s128 = c128_ref[...], s128_ref[...]
    low = lax.broadcasted_iota(jnp.int32, (tm, LANES), 1) < HEAD_DIM

    col = [0]

    def segment(width):
        y = jnp.dot(a, w_ref[:, col[0]:col[0] + width], preferred_element_type=F32)
        col[0] += width
        return [y[:, j * LANES:(j + 1) * LANES] for j in range(width // LANES)]

    def rope64(y):
        return y * c64 + _swap_halves_64(y) * s64

    def rope128(y):
        return y * c128 + pltpu.roll(y, 64, 1) * s128

    for j, y in enumerate(segment(n_q)):
        qt_ref[j * LANES:(j + 1) * LANES, :] = (rope64(y) * Q_SCALE).T.astype(BF16)
    kk, vv = segment(2 * LANES)
    kk = rope64(kk)
    kt_ref[...] = kk.T
    for s, slab in enumerate(_pad_pair_slabs(kk, pltpu.roll(kk, HEAD_DIM, 1), low)):
        kp_ref[:, s * LANES:(s + 1) * LANES] = slab.astype(BF16)
    vvt = vv.T
    vt_ref[...] = vvt
    vtt_ref[...] = vvt.astype(BF16)
    for j, y in enumerate(segment(n_qi)):
        qit_ref[j * LANES:(j + 1) * LANES, :] = (rope64(y) * (IDX_DIM ** -0.5)).T.astype(BF16)
    y, _ = segment(2 * LANES)
    ki = rope64(y)
    kit_ref[...] = ki.T[:IDX_DIM, :]
    ki_lo, ki_hi, _, _ = _pad_pair_slabs(ki, pltpu.roll(ki, IDX_DIM, 1), low)
    kip_ref[:, 0:LANES] = ki_lo.astype(BF16)
    kip_ref[:, LANES:2 * LANES] = ki_hi.astype(BF16)
    wit_ref[...] = y.T[IDX_DIM:IDX_DIM + IDX_HEADS, :] * (IDX_HEADS ** -0.5)
    for j, y in enumerate(segment(n_ret)):
        rq_ref[:, j * LANES:(j + 1) * LANES] = rope128(y).astype(BF16)
    for j, y in enumerate(segment(n_ret)):
        rkt_ref[j * LANES:(j + 1) * LANES, :] = (rope128(y) * (RET_DK ** -0.5)).T.astype(BF16)
    for j, y in enumerate(segment(n_ret)):
        rv_ref[:, j * LANES:(j + 1) * LANES] = y.astype(BF16)
    for j, y in enumerate(segment(n_ret)):
        rg_ref[:, j * LANES:(j + 1) * LANES] = y


def _pack_w_in(w_in, n_q, n_kv, n_qi, n_ret):
    d = w_in.shape[0]
    sizes = (n_q, n_kv, n_kv, n_qi, IDX_DIM, IDX_HEADS, n_ret, n_ret, n_ret, n_ret)
    offs = np.concatenate([[0], np.cumsum(sizes)])
    seg = [w_in[:, offs[i]:offs[i + 1]] for i in range(len(sizes))]
    pad = jnp.zeros((d, 2 * LANES - IDX_DIM - IDX_HEADS), w_in.dtype)
    packed = jnp.concatenate(seg[:4] + [seg[4], seg[5], pad] + seg[6:], axis=1)
    return packed.astype(BF16)


def _inproj(x2d, pos, g_mix, w_packed, *, seq, tm, dims):
    n_q, n_kv, n_qi, n_ret = dims
    rows, d = x2d.shape
    nb = rows // seq
    spt = seq // tm
    c64, s64 = _rope_tables(pos, HEAD_DIM)
    c128, s128 = _rope_tables(pos, RET_DK)
    wcols = w_packed.shape[1]

    row = lambda w: pl.BlockSpec((tm, w), lambda i: (i, 0))
    tab = pl.BlockSpec((tm, LANES), lambda i: (i % spt, 0))
    tr = lambda h: pl.BlockSpec((None, h, tm), lambda i: (i // spt, 0, i % spt))
    out_shape = (
        jax.ShapeDtypeStruct((nb, n_q, seq), BF16),
        jax.ShapeDtypeStruct((nb, n_kv, seq), F32),
        jax.ShapeDtypeStruct((nb, n_kv, seq), F32),
        jax.ShapeDtypeStruct((nb, n_qi, seq), BF16),
        jax.ShapeDtypeStruct((nb, IDX_DIM, seq), F32),
        jax.ShapeDtypeStruct((nb, IDX_HEADS, seq), F32),
        jax.ShapeDtypeStruct((rows, n_ret), BF16),
        jax.ShapeDtypeStruct((nb, n_ret, seq), BF16),
        jax.ShapeDtypeStruct((rows, n_ret), BF16),
        jax.ShapeDtypeStruct((rows, n_ret), F32),
        jax.ShapeDtypeStruct((rows, 4 * n_kv), BF16),
        jax.ShapeDtypeStruct((nb, n_kv, seq), BF16),
        jax.ShapeDtypeStruct((rows, 2 * LANES), BF16),
    )
    out_specs = (tr(n_q), tr(n_kv), tr(n_kv), tr(n_qi), tr(IDX_DIM), tr(IDX_HEADS),
                 row(n_ret), tr(n_ret), row(n_ret), row(n_ret), row(4 * n_kv), tr(n_kv), row(2 * LANES))
    return pl.pallas_call(
        functools.partial(_inproj_kernel, n_q=n_q, n_qi=n_qi, n_ret=n_ret),
        out_shape=out_shape,
        grid=(rows // tm,),
        in_specs=[row(d), _const_spec((1, d)), _const_spec((d, wcols)), tab, tab, tab, tab],
        out_specs=out_specs,
        compiler_params=_cparams("parallel"),
        name="inproj",
    )(x2d, g_mix.reshape(1, d), w_packed, c64, s64, c128, s128)


N_ACC = 4


def _fold_keys(st_ref, nch, ck, init, fn, combine):
    def body(c, carries):
        carries = list(carries)
        base = pl.multiple_of(c * ck, ck)
        chunk = st_ref[pl.ds(base, ck), :]
        for j in range(ck // SUBLANES):
            blk = chunk[j * SUBLANES:(j + 1) * SUBLANES, :]
            carries[j % N_ACC] = fn(carries[j % N_ACC], blk, base + j * SUBLANES)
        return tuple(carries)
    carries = lax.fori_loop(0, nch, body, (init,) * N_ACC)
    out = carries[0]
    for other in carries[1:]:
        out = combine(out, other)
    return out


def _all_sublanes(x, op):
    for shift in (4, 2, 1):
        x = op(x, pltpu.roll(x, shift, 0))
    return x


def _any(mask):
    return jnp.max(jnp.where(mask, 1.0, 0.0)) > 0.5


def _count(st_ref, nch, ck, pred):
    nq = st_ref.shape[1]
    acc = _fold_keys(st_ref, nch, ck, jnp.zeros((SUBLANES, nq), F32),
                     lambda a, blk, row0: a + jnp.where(pred(blk, row0), 1.0, 0.0), jnp.add)
    return _all_sublanes(acc, jnp.add)


def _select_threshold(st_ref, thr_ref, tie_ref, nch, ck, rmin, rmax, topk, n_fast):
    nq = st_ref.shape[1]
    kf = float(topk)
    full = lambda v: jnp.full((SUBLANES, nq), v, F32)

    def fast_body(_, carry):
        lo, hi = carry
        mid = lo + (hi - lo) * 0.5
        up = _count(st_ref, nch, ck, lambda blk, _r: blk > mid) >= kf
        return jnp.where(up, mid, lo), jnp.where(up, hi, mid)

    lo, hi = lax.fori_loop(0, n_fast, fast_body, (rmin, rmax))
    unres = _count(st_ref, nch, ck, lambda blk, _r: blk >= lo) > kf
    thr_ref[...] = lo

    @pl.when(_any(unres))
    def _exact():
        pair_minmax = lambda x, y: (jnp.minimum(x[0], y[0]), jnp.maximum(x[1], y[1]))
        a, b = _fold_keys(
            st_ref, nch, ck, (full(POS), full(NEG)),
            lambda cr, blk, _r: (jnp.minimum(cr[0], jnp.where(blk >= lo, blk, POS)),
                                 jnp.maximum(cr[1], jnp.where(blk <= hi, blk, NEG))),
            pair_minmax)
        lo2 = jnp.where(unres, _all_sublanes(a, jnp.minimum), lo)
        ub = jnp.where(unres, _all_sublanes(b, jnp.maximum), lo)

        def body(carry):
            lo2, ub, _ = carry
            mid = lo2 + (ub - lo2) * 0.5
            mid = jnp.where(mid >= ub, lo2, mid)

            def step(cr, blk, _r):
                gt = blk > mid
                return (cr[0] + jnp.where(gt, 1.0, 0.0), jnp.minimum(cr[1], jnp.where(gt, blk, POS)),
                        jnp.maximum(cr[2], jnp.where(gt, NEG, blk)))
            cnt, a, b = _fold_keys(st_ref, nch, ck, (full(0.0), full(POS), full(NEG)), step,
                                   lambda x, y: (x[0] + y[0],) + pair_minmax(x[1:], y[1:]))
            up = _all_sublanes(cnt, jnp.add) >= kf
            active = lo2 < ub
            lo2n = jnp.where(active & up, _all_sublanes(a, jnp.minimum), lo2)
            ubn = jnp.where(active & jnp.logical_not(up), _all_sublanes(b, jnp.maximum), ub)
            return lo2n, ubn, _any(lo2n < ubn).astype(jnp.int32)

        v, _, _ = lax.while_loop(lambda carry: carry[2] > 0, body, (lo2, ub, _any(lo2 < ub).astype(jnp.int32)))
        thr_ref[...] = jnp.where(unres, v, lo)
        n_ge, n_gt = _fold_keys(
            st_ref, nch, ck, (full(0.0), full(0.0)),
            lambda cr, blk, _r: (cr[0] + jnp.where(blk >= v, 1.0, 0.0), cr[1] + jnp.where(blk > v, 1.0, 0.0)),
            lambda x, y: (x[0] + y[0], x[1] + y[1]))
        tied = unres & (_all_sublanes(n_ge, jnp.add) > kf)

        @pl.when(_any(tied))
        def _ties():
            _drop_excess_ties(st_ref, tie_ref, nch, ck, v, tied, kf - _all_sublanes(n_gt, jnp.add))


def _drop_excess_ties(st_ref, tie_ref, nch, ck, v, tied, need):
    nq = st_ref.shape[1]
    full = lambda x: jnp.full((SUBLANES, nq), x, F32)
    nkeys = st_ref.shape[0]
    key_in_chunk = lax.broadcasted_iota(jnp.int32, (ck, nq), 0)
    rep = lambda x: jnp.broadcast_to(x[0:1, :], (ck, nq))
    v_c = rep(v)

    def each_chunk(fn):
        def body(c, _):
            base = pl.multiple_of(c * ck, ck)
            fn(pl.ds(base, ck), base)
            return 0
        lax.fori_loop(0, nch, body, 0)

    def mark(rows, row0):
        tie_ref[rows, :] = jnp.where(st_ref[rows, :] == v_c, (key_in_chunk + row0).astype(F32), NO_TIE)
    each_chunk(mark)

    n_j = int(np.ceil(np.log2(nkeys))) + 1
    need_t = jnp.where(tied, need, 0.0)
    need_max = jnp.max(need_t)

    def by_walk():
        def body(carry):
            j_prev, k = carry
            nxt = _fold_keys(tie_ref, nch, ck, full(NO_TIE),
                             lambda a, blk, _r: jnp.minimum(a, jnp.where(blk > j_prev, blk, NO_TIE)), jnp.minimum)
            return jnp.where(need_t > k, _all_sublanes(nxt, jnp.minimum), j_prev), k + 1.0
        return lax.while_loop(lambda carry: carry[1] < need_max, body, (full(-1.0), jnp.float32(0.0)))[0]

    def by_bisection():
        def body(_, carry):
            lo_j, hi_j = carry
            mid_j = jnp.floor((lo_j + hi_j) * 0.5)
            ok = _count(tie_ref, nch, ck, lambda blk, _r: blk <= mid_j) >= need
            return jnp.where(ok, lo_j, mid_j), jnp.where(ok, mid_j, hi_j)
        return lax.fori_loop(0, n_j, body, (full(-1.0), full(float(nkeys - 1))))[1]

    j_last = lax.cond(need_max <= float(n_j), by_walk, by_bisection)

    j_c = rep(jnp.where(tied, j_last, NO_TIE))

    def drop(rows, _row0):
        t = tie_ref[rows, :]
        s = st_ref[rows, :]
        st_ref[rows, :] = jnp.where(t > j_c, jnp.where(t < NO_TIE, NEG, s), s)
    each_chunk(drop)


def _rows_min(x):
    return jnp.min(x.reshape(x.shape[0] // SUBLANES, SUBLANES, x.shape[1]), axis=0)


def _rows_max(x):
    return jnp.max(x.reshape(x.shape[0] // SUBLANES, SUBLANES, x.shape[1]), axis=0)


def _rows_sum(x):
    return jnp.sum(x.reshape(x.shape[0] // SUBLANES, SUBLANES, x.shape[1]), axis=0)


def _dsa_prompt_kernel(kip_ref, qit_ref, wit_ref, kp_ref, qt_ref, vtt_ref, o_ref, st_ref, thr_ref,
                       m_ref, l_ref, acc_ref, lg_ref, pr_ref, tie_ref, *, tq, topk, n_fast, rs, cps):
    ck = tq
    i = pl.program_id(1)
    nch = i + 1
    n_iter = (nch + cps - 1) // cps
    n_pairs = qt_ref.shape[0] // LANES
    pairs_per_kv = n_pairs // KV_HEADS
    bcast = lambda row, n: jnp.broadcast_to(row, (n, tq))

    key_l = lax.broadcasted_iota(jnp.int32, (rs, tq), 0)
    qry_l = lax.broadcasted_iota(jnp.int32, (rs, tq), 1)
    w_rows = [bcast(wit_ref[h:h + 1, :], rs) for h in range(IDX_HEADS)]

    def score_chunk(c, mn, mx):
        base = pl.multiple_of(c * ck, ck)
        slack = jnp.where(c < i, ck, jnp.where(c == i, 0, -ck))
        for r in range(ck // rs):
            rows = pl.ds(base + r * rs, rs)
            k_lo = kip_ref[rows, 0:LANES]
            k_hi = kip_ref[rows, LANES:2 * LANES]
            acc = jnp.zeros((rs, tq), F32)
            for p in range(IDX_HEADS // 2):
                rhs = qit_ref[p * LANES:(p + 1) * LANES, :]
                acc = acc + jnp.maximum(jnp.dot(k_lo, rhs, preferred_element_type=F32), 0.0) * w_rows[2 * p]
                acc = acc + jnp.maximum(jnp.dot(k_hi, rhs, preferred_element_type=F32), 0.0) * w_rows[2 * p + 1]
            valid = key_l + r * rs <= qry_l + slack
            st_ref[rows, :] = jnp.where(valid, acc, NEG)
            mn = jnp.minimum(mn, _rows_min(jnp.where(valid, acc, POS)))
            mx = jnp.maximum(mx, _rows_max(jnp.where(valid, acc, NEG)))
        return mn, mx

    def score_body(k, carry):
        for u in range(cps):
            carry = score_chunk(cps * k + u, *carry)
        return carry

    mn, mx = lax.fori_loop(0, n_iter, score_body,
                           (jnp.full((SUBLANES, tq), POS, F32), jnp.full((SUBLANES, tq), NEG, F32)))
    _select_threshold(st_ref, thr_ref, tie_ref, nch, ck,
                      _all_sublanes(mn, jnp.minimum), _all_sublanes(mx, jnp.maximum), topk, n_fast)

    thr = bcast(thr_ref[0:1, :], ck)
    m_ref[...] = jnp.full(m_ref.shape, SOFTMAX_NEG, F32)
    l_ref[...] = jnp.zeros(l_ref.shape, F32)
    acc_ref[...] = jnp.zeros(acc_ref.shape, F32)

    slab_of = lambda h: 2 * (h // 2 // pairs_per_kv) + h % 2
    n_heads = 2 * n_pairs
    ones = jnp.ones((ONES_ROWS, ck), BF16)

    def att_body(k, _):
        chunk_rows = [pl.ds(pl.multiple_of((cps * k + u) * ck, ck), ck) for u in range(cps)]
        for u, rows in enumerate(chunk_rows):
            sel = st_ref[rows, :] >= thr
            for h in range(n_heads):
                slab = slab_of(h)
                kx = kp_ref[rows, slab * LANES:(slab + 1) * LANES]
                qt_pair = qt_ref[(h // 2) * LANES:(h // 2 + 1) * LANES, :]
                lg_ref[u, h] = jnp.where(sel, jnp.dot(kx, qt_pair, preferred_element_type=F32), SOFTMAX_NEG)
        alpha = {}
        for u in range(cps):
            for h in range(n_heads):
                lg = lg_ref[u, h]
                m_old = m_ref[h]
                m_new = jnp.maximum(m_old, _all_sublanes(_rows_max(lg), jnp.maximum))
                alpha[u, h] = jnp.exp2(m_old - m_new)
                m_ref[h] = m_new
                pr_ref[u, h] = jnp.exp2(lg - bcast(m_new[0:1, :], ck)).astype(BF16)
        for u, rows in enumerate(chunk_rows):
            for p in range(n_pairs):
                pv = []
                g = p // pairs_per_kv
                vt1 = jnp.concatenate([vtt_ref[g * HEAD_DIM:(g + 1) * HEAD_DIM, rows], ones], axis=0)
                for h in (2 * p, 2 * p + 1):
                    pv1 = jnp.dot(vt1, pr_ref[u, h], preferred_element_type=F32)
                    l_ref[h] = alpha[u, h] * l_ref[h] + pv1[HEAD_DIM:HEAD_DIM + SUBLANES, :]
                    pv.append(pv1[:HEAD_DIM, :])
                scale = jnp.concatenate([bcast(alpha[u, 2 * p][0:1, :], HEAD_DIM),
                                         bcast(alpha[u, 2 * p + 1][0:1, :], HEAD_DIM)], axis=0)
                acc_ref[p] = acc_ref[p] * scale + jnp.concatenate(pv, axis=0)
        return 0

    lax.fori_loop(0, n_iter, att_body, 0)
    for p in range(n_pairs):
        denom = jnp.concatenate([bcast(l_ref[2 * p, 0:1, :], HEAD_DIM), bcast(l_ref[2 * p + 1, 0:1, :], HEAD_DIM)],
                                axis=0)
        o_ref[:, p * LANES:(p + 1) * LANES] = (acc_ref[p] / denom).T.astype(o_ref.dtype)


def _dsa_prompt(kip, qit, wit, kp, qt, vtt, *, nb, seq, tq, topk, n_fast):
    n_q = qt.shape[1]
    spb = seq // tq
    cps = 2 if spb % 2 == 0 else 1
    per_b_rows = lambda a: pl.BlockSpec((seq, a.shape[1]), lambda b, i: (b, 0))
    q_cols = lambda a: pl.BlockSpec((None, a.shape[1], tq), lambda b, i: (b, 0, i))
    return pl.pallas_call(
        functools.partial(_dsa_prompt_kernel, tq=tq, topk=topk, n_fast=n_fast, rs=min(tq, 128), cps=cps),
        out_shape=jax.ShapeDtypeStruct((nb * seq, n_q), BF16),
        grid=(nb, spb),
        in_specs=[per_b_rows(kip), q_cols(qit), q_cols(wit), per_b_rows(kp), q_cols(qt),
                  pl.BlockSpec((None, vtt.shape[1], seq), lambda b, i: (b, 0, 0))],
        out_specs=pl.BlockSpec((tq, n_q), lambda b, i: (b * spb + i, 0)),
        scratch_shapes=[pltpu.VMEM((seq, tq), F32), pltpu.VMEM((SUBLANES, tq), F32),
                        pltpu.VMEM((n_q // HEAD_DIM, SUBLANES, tq), F32),
                        pltpu.VMEM((n_q // HEAD_DIM, SUBLANES, tq), F32),
                        pltpu.VMEM((n_q // LANES, LANES, tq), F32),
                        pltpu.VMEM((cps, n_q // HEAD_DIM, tq, tq), F32),
                        pltpu.VMEM((cps, n_q // HEAD_DIM, tq, tq), BF16),
                        pltpu.VMEM((seq, tq), F32)],
        compiler_params=_cparams("parallel", "arbitrary"),
        name="dsa_prompt",
    )(kip, qit, wit, kp, qt, vtt)


def _retention_tables(chunk, n_heads):
    log_g = np.log1p(-np.exp2(-5.0 - np.arange(n_heads, dtype=np.float64)))
    i = np.arange(chunk, dtype=np.float64)
    diff = i[:, None] - i[None, :]
    decay = np.where(diff >= 0, np.exp(log_g[:, None, None] * np.maximum(diff, 0.0)), 0.0)
    q_dec = np.exp(log_g[:, None] * (i[None, :] + 1.0))
    k_dec = np.exp(log_g[:, None] * (chunk - 1.0 - i)[None, :])
    c_dec = np.exp(log_g * chunk)
    return decay, q_dec, k_dec, c_dec


def _group_norm_gate(o, rg, gain):
    mu = jnp.mean(o, axis=-1, keepdims=True)
    var = jnp.mean(jnp.square(o - mu), axis=-1, keepdims=True)
    on = ((o - mu) * lax.rsqrt(var + GN_EPS)) * gain
    return jax.nn.silu(rg) * on


def _retention_prompt_kernel(rq_ref, rkt_ref, rv_ref, rg_ref, gain_ref, dmat_ref, qdec_ref, kdec_ref, cdec_ref,
                             o_ref, st_ref, state_ref, *, chunk, n_heads):
    j = pl.program_id(1)

    @pl.when(j == 0)
    def _():
        state_ref[...] = jnp.zeros_like(state_ref)

    n_cc = rq_ref.shape[0] // chunk
    dot = functools.partial(jnp.dot, preferred_element_type=F32)
    tiles = [(cc, h, slice(cc * chunk, (cc + 1) * chunk), slice(h * RET_DV, (h + 1) * RET_DV))
             for cc in range(n_cc) for h in range(n_heads)]
    inner, update = {}, {}
    for cc, h, rows, cols in tiles:
        kt = rkt_ref[cols, rows]
        inner[cc, h] = (dot(rq_ref[rows, cols], kt) * dmat_ref[h]).astype(BF16)
        update[cc, h] = dot((kt.astype(F32) * kdec_ref[h]).astype(BF16), rv_ref[rows, cols])
    cross = {}
    for h in range(n_heads):
        state = state_ref[h]
        for cc in range(n_cc):
            rows, cols = slice(cc * chunk, (cc + 1) * chunk), slice(h * RET_DV, (h + 1) * RET_DV)
            cross[cc, h] = dot(rq_ref[rows, cols], state.astype(BF16)) * qdec_ref[h]
            state = cdec_ref[h] * state + update[cc, h]
        state_ref[h] = state
    for cc, h, rows, cols in tiles:
        o = dot(inner[cc, h], rv_ref[rows, cols]) + cross[cc, h]
        o_ref[rows, cols] = _group_norm_gate(o, rg_ref[rows, cols], gain_ref[:, cols]).astype(o_ref.dtype)

    @pl.when(j == pl.num_programs(1) - 1)
    def _():
        st_ref[...] = state_ref[...]


def _retention_prompt(rq, rkt, rv, rg, g_retn, *, nb, seq, tr):
    n_ret = rq.shape[1]
    n_heads = n_ret // RET_DV
    chunk = RET_CHUNK
    decay, q_dec, k_dec, c_dec = _retention_tables(chunk, n_heads)
    decay = jnp.asarray(decay, F32)
    qdec_b = jnp.asarray(np.broadcast_to(q_dec[:, :, None], (n_heads, chunk, LANES)), F32)
    kdec_b = jnp.asarray(k_dec[:, None, :], F32)
    cdec_b = jnp.asarray(np.broadcast_to(c_dec[:, None, None], (n_heads, 1, LANES)), F32)
    spb = seq // tr
    row = lambda: pl.BlockSpec((tr, n_ret), lambda b, j: (b * spb + j, 0))
    return pl.pallas_call(
        functools.partial(_retention_prompt_kernel, chunk=chunk, n_heads=n_heads),
        out_shape=(jax.ShapeDtypeStruct((nb * seq, n_ret), BF16),
                   jax.ShapeDtypeStruct((nb, n_heads, RET_DK, RET_DV), F32)),
        grid=(nb, spb),
        in_specs=[row(), pl.BlockSpec((None, n_ret, tr), lambda b, j: (b, 0, j)), row(), row(),
                  _const_spec((1, n_ret)), _const_spec((n_heads, chunk, chunk)),
                  _const_spec((n_heads, chunk, LANES)), _const_spec((n_heads, 1, chunk)),
                  _const_spec((n_heads, 1, LANES))],
        out_specs=(row(), pl.BlockSpec((None, n_heads, RET_DK, RET_DV), lambda b, j: (b, 0, 0, 0))),
        scratch_shapes=[pltpu.VMEM((n_heads, RET_DK, RET_DV), F32)],
        compiler_params=_cparams("parallel", "arbitrary"),
        name="retention_prompt",
    )(rq, rkt, rv, rg, g_retn.reshape(1, n_ret), decay, qdec_b, kdec_b, cdec_b)


def _rms(x, g):
    return (x * lax.rsqrt(jnp.mean(x * x, axis=-1, keepdims=True) + EPS)) * g


def _tail_kernel(h_ref, attn_ref, ret_ref, p_ref, wo_ref, g_ffn_ref, w_ffn_in_ref, w_down_ref,
                 g_ple_ref, w_pg_ref, w_pp_ref, g_fin_ref, y_ref):
    dot = functools.partial(jnp.dot, preferred_element_type=F32)
    n_attn = attn_ref.shape[1]
    d_ff = w_down_ref.shape[0]
    h = h_ref[...] + dot(attn_ref[...], wo_ref[:n_attn, :]) + dot(ret_ref[...], wo_ref[n_attn:, :])
    f = _rms(h, g_ffn_ref[...]).astype(BF16)
    act = (jax.nn.silu(dot(f, w_ffn_in_ref[:, :d_ff])) * dot(f, w_ffn_in_ref[:, d_ff:])).astype(BF16)
    h = h + dot(act, w_down_ref[...])
    gate = jax.nn.sigmoid(dot(_rms(h, g_ple_ref[...]).astype(BF16), w_pg_ref[...]))
    h = h + gate * dot(p_ref[...].astype(BF16), w_pp_ref[...])
    y_ref[...] = _rms(h, g_fin_ref[...])


def _tail(h2d, attn, ret, p2d, w_out, g_ffn, w_ffn_in, w_ffn_out, g_ple, w_ple_gate, w_ple_proj, g_final, *, tm):
    rows, d = h2d.shape
    n_attn = attn.shape[1]
    assert w_ffn_in.shape[1] == 2 * w_ffn_out.shape[0] and w_ffn_out.shape[0] % LANES == 0
    ws = [w_out.astype(BF16), g_ffn.reshape(1, d), w_ffn_in.astype(BF16), w_ffn_out.astype(BF16),
          g_ple.reshape(1, d), w_ple_gate.astype(BF16), w_ple_proj.astype(BF16), g_final.reshape(1, d)]
    row = lambda w: pl.BlockSpec((tm, w), lambda i: (i, 0))
    return pl.pallas_call(
        _tail_kernel,
        out_shape=jax.ShapeDtypeStruct((rows, d), F32),
        grid=(rows // tm,),
        in_specs=[row(d), row(n_attn), row(ret.shape[1]), row(p2d.shape[1])] + [_const_spec(w.shape) for w in ws],
        out_specs=row(d),
        compiler_params=_cparams("parallel"),
        name="tail",
    )(h2d, attn, ret, p2d, *ws)


def _page_copies(pools, bufs, sems, pt_ref, b, c, slot, pages_per_chunk):
    copies = []
    for pool, buf in zip(pools, bufs):
        for p in range(pages_per_chunk):
            page = pt_ref[b, c * pages_per_chunk + p]
            copies.append(pltpu.make_async_copy(pool.at[page], buf.at[slot, p], sems.at[slot]))
    return copies


def _chunk_keys_on_lanes(pages):
    return jnp.concatenate([pages[p] for p in range(pages.shape[0])], axis=1).astype(BF16)


def _stream_chunks(pools, bufs, sems, pt_ref, n_chunks, pages_per_chunk, compute):
    b = pl.program_id(0)
    nb = pl.num_programs(0)

    def start(bb, cc, slot):
        for n, cp in enumerate(_page_copies(pools, bufs, sems, pt_ref, bb, cc, slot, pages_per_chunk)):
            cp.start(priority=n % 2)

    @pl.when(b == 0)
    def _():
        start(0, 0, 0)

    first = 0 if n_chunks % 2 == 0 else b % 2
    for c in range(n_chunks):
        slot = (first + c) % 2
        if c + 1 < n_chunks:
            start(b, c + 1, 1 - slot)
        else:
            @pl.when(b + 1 < nb)
            def _():
                start(b + 1, 0, 1 - slot)
        for cp in _page_copies(pools, bufs, sems, pt_ref, b, c, slot, pages_per_chunk):
            cp.wait()
        compute(c, slot)


def _tile_lanes(x, width):
    return x if width == LANES else jnp.concatenate([x] * (width // LANES), axis=1)


def _nt_dot(a, b):
    return lax.dot_general(a, b, (((1,), (1,)), ((), ())), preferred_element_type=F32)


def _head_sum(x, n_t):
    return jnp.sum(x.reshape(n_t, IDX_HEADS, x.shape[-1]), axis=1)


def _sample_scores_kernel(pt_ref, qi_ref, wi_ref, kinew_ref, pool_ref, s_ref, buf, sems,
                          *, n_t, n_chunks, pages_per_chunk, pages_per_dot):
    ck = pages_per_chunk * PAGE_SIZE
    qi = qi_ref[...]
    w = wi_ref[...]

    def compute(c, slot):
        for p0 in range(0, pages_per_chunk, pages_per_dot):
            kit = _chunk_keys_on_lanes(buf[slot, p0:p0 + pages_per_dot])
            a = jnp.dot(qi, kit, preferred_element_type=F32)
            col0 = c * ck + p0 * PAGE_SIZE
            s_ref[:, col0:col0 + pages_per_dot * PAGE_SIZE] = _head_sum(jnp.maximum(a, 0.0) * w, n_t)

    _stream_chunks([pool_ref], [buf], sems, pt_ref, n_chunks, pages_per_chunk, compute)

    a = jnp.dot(qi, kinew_ref[...], preferred_element_type=F32)
    s_new = _head_sum(jnp.maximum(a, 0.0) * w, n_t)
    t_q = lax.broadcasted_iota(jnp.int32, s_new.shape, 0)
    t_k = lax.broadcasted_iota(jnp.int32, s_new.shape, 1)
    s_ref[:, n_chunks * ck:] = jnp.where(t_k <= t_q, s_new, NEG)


def _sample_scores(page_table, qi_s, wi_s, kinew, pool_ik, *, n_t, pages_per_chunk):
    db, n_pages = page_table.shape
    n_chunks = n_pages // pages_per_chunk
    past = n_pages * PAGE_SIZE
    rows = n_t * IDX_HEADS
    grid_spec = pltpu.PrefetchScalarGridSpec(
        num_scalar_prefetch=1,
        grid=(db,),
        in_specs=[pl.BlockSpec((None, rows, IDX_DIM), lambda b, pt: (b, 0, 0)),
                  pl.BlockSpec((None, rows, 1), lambda b, pt: (b, 0, 0)),
                  pl.BlockSpec((None, IDX_DIM, PAGE_SIZE), lambda b, pt: (b, 0, 0)),
                  pl.BlockSpec(memory_space=pl.ANY)],
        out_specs=pl.BlockSpec((None, n_t, past + PAGE_SIZE), lambda b, pt: (b, 0, 0)),
        scratch_shapes=[pltpu.VMEM((2, pages_per_chunk, IDX_DIM, PAGE_SIZE), F32),
                        pltpu.SemaphoreType.DMA((2,))],
    )
    return pl.pallas_call(
        functools.partial(_sample_scores_kernel, n_t=n_t, n_chunks=n_chunks, pages_per_chunk=pages_per_chunk,
                          pages_per_dot=min(16, pages_per_chunk)),
        out_shape=jax.ShapeDtypeStruct((db, n_t, past + PAGE_SIZE), F32),
        grid_spec=grid_spec,
        compiler_params=_cparams("arbitrary"),
        name="sample_scores",
    )(page_table, qi_s, wi_s, kinew, pool_ik)


def _sample_select_kernel(s_in_ref, s_out_ref, thr_ref, st_ref, thr8_ref, tie_ref, *, ck, topk, n_fast):
    nq, nkeys = s_in_ref.shape
    nch = nkeys // ck + 0 * pl.program_id(0)
    chunk = lambda c: pl.ds(pl.multiple_of(c * ck, ck), ck)

    def to_lanes(c, _):
        st_ref[chunk(c), :] = s_in_ref[:, chunk(c)].T
        return 0
    lax.fori_loop(0, nch, to_lanes, 0)

    mn, mx = _fold_keys(
        st_ref, nch, ck, (jnp.full((SUBLANES, nq), POS, F32), jnp.full((SUBLANES, nq), NEG, F32)),
        lambda cr, blk, _r: (jnp.minimum(cr[0], jnp.where(blk > 0.5 * NEG, blk, POS)), jnp.maximum(cr[1], blk)),
        lambda x, y: (jnp.minimum(x[0], y[0]), jnp.maximum(x[1], y[1])))
    _select_threshold(st_ref, thr8_ref, tie_ref, nch, ck, _all_sublanes(mn, jnp.minimum),
                      _all_sublanes(mx, jnp.maximum), topk, n_fast)
    def to_rows(c, _):
        s_out_ref[:, chunk(c)] = st_ref[chunk(c), :].T
        return 0
    lax.fori_loop(0, nch, to_rows, 0)
    thr_ref[...] = jnp.broadcast_to(thr8_ref[0:1, :], (LANES, nq)).T


def _sample_select(s2d, *, topk, n_fast):
    rows, ncols = s2d.shape
    ck = max(w for w in (LANES, 2 * LANES, 3 * LANES, 4 * LANES) if ncols % w == 0)
    full = pl.BlockSpec((rows, ncols), lambda i: (0, 0), pipeline_mode=pl.Buffered(1))
    return pl.pallas_call(
        functools.partial(_sample_select_kernel, ck=ck, topk=topk, n_fast=n_fast),
        out_shape=(jax.ShapeDtypeStruct((rows, ncols), F32), jax.ShapeDtypeStruct((rows, LANES), F32)),
        grid=(1,),
        in_specs=[full],
        out_specs=(full, pl.BlockSpec((rows, LANES), lambda i: (0, 0))),
        scratch_shapes=[pltpu.VMEM((ncols, rows), F32), pltpu.VMEM((SUBLANES, rows), F32),
                        pltpu.VMEM((ncols, rows), F32)],
        compiler_params=_cparams("arbitrary"),
        name="sample_select",
    )(s2d)


def _sample_attend_kernel(pt_ref, q_ref, s_ref, thr_ref, knew_ref, vnew_ref, poolk_ref, poolv_ref, o_ref,
                          kbuf, vbuf, sems, *, n_t, n_chunks, pages_per_chunk):
    ck = pages_per_chunk * PAGE_SIZE
    rows = q_ref.shape[1]
    reps = rows // n_t
    thr = jnp.concatenate([thr_ref[...]] * reps, axis=0)
    state = [[jnp.full((rows, 1), SOFTMAX_NEG, F32), jnp.zeros((rows, 1), F32), jnp.zeros((rows, HEAD_DIM), F32)]
             for _ in range(KV_HEADS)]

    def attend(kt, vt, s_blk):
        width = s_blk.shape[1]
        sel = jnp.concatenate([s_blk] * reps, axis=0) >= _tile_lanes(thr, width)
        lgs = [jnp.where(sel, jnp.dot(q_ref[g], kt[g], preferred_element_type=F32), SOFTMAX_NEG)
               for g in range(KV_HEADS)]
        stats = []
        for g in range(KV_HEADS):
            m, l, _ = state[g]
            m_new = jnp.maximum(m, jnp.max(lgs[g], axis=1, keepdims=True))
            alpha = jnp.exp2(m - m_new)
            pr = jnp.exp2(lgs[g] - m_new)
            stats.append((m_new, alpha, alpha * l + jnp.sum(pr, axis=1, keepdims=True), pr.astype(BF16)))
        for g in range(KV_HEADS):
            m_new, alpha, l_new, pr = stats[g]
            state[g] = [m_new, l_new, alpha * state[g][2] + _nt_dot(pr, vt[g])]

    def compute(c, slot):
        attend([_chunk_keys_on_lanes(kbuf[slot, :, g]) for g in range(KV_HEADS)],
               [_chunk_keys_on_lanes(vbuf[slot, :, g]) for g in range(KV_HEADS)], s_ref[:, c * ck:(c + 1) * ck])

    _stream_chunks([poolk_ref, poolv_ref], [kbuf, vbuf], sems, pt_ref, n_chunks, pages_per_chunk, compute)
    attend(knew_ref[...], vnew_ref[...], s_ref[:, n_chunks * ck:])
    for g in range(KV_HEADS):
        o_ref[g] = state[g][2] / state[g][1]


def _sample_attend(page_table, q_bd, s3d, thr3d, knew, vnew, pool_k, pool_v, *, n_t, pages_per_chunk):
    db, n_pages = page_table.shape
    n_chunks = n_pages // pages_per_chunk
    rows = q_bd.shape[2]
    ncols = s3d.shape[2]
    per_b = lambda r, w: pl.BlockSpec((None, r, w), lambda b, pt: (b, 0, 0))
    per_b4 = lambda r, w: pl.BlockSpec((None, KV_HEADS, r, w), lambda b, pt: (b, 0, 0, 0))
    grid_spec = pltpu.PrefetchScalarGridSpec(
        num_scalar_prefetch=1,
        grid=(db,),
        in_specs=[per_b4(rows, HEAD_DIM), per_b(n_t, ncols), per_b(n_t, LANES), per_b4(HEAD_DIM, PAGE_SIZE),
                  per_b4(HEAD_DIM, PAGE_SIZE), pl.BlockSpec(memory_space=pl.ANY), pl.BlockSpec(memory_space=pl.ANY)],
        out_specs=per_b4(rows, HEAD_DIM),
        scratch_shapes=[pltpu.VMEM((2, pages_per_chunk, KV_HEADS, HEAD_DIM, PAGE_SIZE), F32),
                        pltpu.VMEM((2, pages_per_chunk, KV_HEADS, HEAD_DIM, PAGE_SIZE), F32),
                        pltpu.SemaphoreType.DMA((2,))],
    )
    return pl.pallas_call(
        functools.partial(_sample_attend_kernel, n_t=n_t, n_chunks=n_chunks, pages_per_chunk=pages_per_chunk),
        out_shape=jax.ShapeDtypeStruct((db, KV_HEADS, rows, HEAD_DIM), F32),
        grid_spec=grid_spec,
        compiler_params=_cparams("arbitrary"),
        name="sample_attend",
    )(page_table, q_bd, s3d, thr3d, knew, vnew, pool_k, pool_v)


def _retention_sample_kernel(rq_ref, rkt_ref, rv_ref, rg_ref, gain_ref, dfull_ref, qdec_ref, kdec_ref, cdec_ref,
                             st_in_ref, o_ref, st_out_ref, *, n_t, n_heads):
    b = pl.program_id(0)

    @pl.when(b == 0)
    def _():
        o_ref[...] = jnp.zeros_like(o_ref)

    n_rows = rq_ref.shape[0]
    row = lax.broadcasted_iota(jnp.int32, (n_rows, RET_DV), 0)
    mine = (row >= b * n_t) & (row < (b + 1) * n_t)
    for h in range(n_heads):
        cols = slice(h * RET_DV, (h + 1) * RET_DV)
        q = jnp.where(mine, rq_ref[:, cols], 0.0).astype(BF16)
        v = rv_ref[:, cols]
        kt = rkt_ref[cols, :]
        state = st_in_ref[h]
        inner = jnp.dot(q, kt, preferred_element_type=F32) * dfull_ref[h]
        o = jnp.dot(inner.astype(BF16), v, preferred_element_type=F32)
        o = o + jnp.dot(q, state.astype(BF16), preferred_element_type=F32) * qdec_ref[h]
        vd = jnp.where(mine, v.astype(F32) * kdec_ref[h], 0.0).astype(BF16)
        st_out_ref[h] = cdec_ref[h] * state + jnp.dot(kt, vd, preferred_element_type=F32)
        gated = _group_norm_gate(o, rg_ref[:, cols], gain_ref[:, cols])
        o_ref[:, cols] = jnp.where(mine, gated, o_ref[:, cols].astype(F32)).astype(o_ref.dtype)


def _retention_sample(rq, rkt, rv, rg, g_retn, state, *, db, n_t):
    n_rows, n_ret = rq.shape
    n_heads = n_ret // RET_DV
    decay, q_dec, k_dec, c_dec = _retention_tables(n_t, n_heads)
    same = np.kron(np.eye(db), np.ones((n_t, n_t)))
    dfull = jnp.asarray(same[None] * np.tile(decay, (1, db, db)), F32)
    qdec_b = jnp.asarray(np.broadcast_to(np.tile(q_dec, (1, db))[:, :, None], (n_heads, n_rows, RET_DV)), F32)
    kdec_b = jnp.asarray(np.broadcast_to(np.tile(k_dec, (1, db))[:, :, None], (n_heads, n_rows, RET_DV)), F32)
    cdec_b = jnp.asarray(np.broadcast_to(c_dec[:, None, None], (n_heads, 1, RET_DV)), F32)
    const = lambda shape: pl.BlockSpec(shape, lambda b: (0,) * len(shape))
    st_spec = pl.BlockSpec((None, n_heads, RET_DK, RET_DV), lambda b: (b, 0, 0, 0))
    return pl.pallas_call(
        functools.partial(_retention_sample_kernel, n_t=n_t, n_heads=n_heads),
        out_shape=(jax.ShapeDtypeStruct((n_rows, n_ret), BF16),
                   jax.ShapeDtypeStruct((db, n_heads, RET_DK, RET_DV), F32)),
        grid=(db,),
        in_specs=[const((n_rows, n_ret)), const((n_ret, n_rows)), const((n_rows, n_ret)), const((n_rows, n_ret)),
                  const((1, n_ret)), const((n_heads, n_rows, n_rows)), const((n_heads, n_rows, RET_DV)),
                  const((n_heads, n_rows, RET_DV)), const((n_heads, 1, RET_DV)), st_spec],
        out_specs=(const((n_rows, n_ret)), st_spec),
        compiler_params=_cparams("arbitrary"),
        name="retention_sample",
    )(rq, rkt, rv, rg, g_retn.reshape(1, n_ret), dfull, qdec_b, kdec_b, cdec_b, state)


def _mixer_dims(w_in, d_model):
    n_q = d_model // 2
    n_ret = d_model // 2
    n_kv = KV_HEADS * HEAD_DIM
    n_qi = IDX_HEADS * IDX_DIM
    assert n_kv == LANES and 2 * n_kv + n_q + n_qi + IDX_DIM + IDX_HEADS + 4 * n_ret == w_in.shape[1]
    return n_q, n_kv, n_qi, n_ret


def _tiles(seq, rows, n_pages):
    return dict(inproj=min(512, seq), dsa=min(256, seq), retention=min(512, seq), tail=min(256, rows),
                attend_pages=max(1, min(64, n_pages // 2)), score_pages=n_pages)


def _prompt_layer(x, p, weights, *, n_fast=21):
    g_mix, w_in, g_retn, w_out, g_ffn, w_ffn_in, w_ffn_out, g_ple, w_ple_gate, w_ple_proj, g_final = weights
    nb, seq, d = x.shape
    tiles = _tiles(seq, nb * seq, 0)
    dims = _mixer_dims(w_in, d)
    w_packed = _pack_w_in(w_in, *dims)
    x2d = x.reshape(nb * seq, d)
    (qt, kt, vt, qit, kit, wit, rq, rkt, rv, rg, kp, vtt, kip) = _inproj(
        x2d, np.arange(seq), g_mix, w_packed, seq=seq, tm=tiles["inproj"], dims=dims)
    topk = min(TOPK_MAX, seq // 4)
    attn = _dsa_prompt(kip, qit, wit, kp, qt, vtt, nb=nb, seq=seq, tq=tiles["dsa"], topk=topk, n_fast=n_fast)
    ret, st = _retention_prompt(rq, rkt, rv, rg, g_retn, nb=nb, seq=seq, tr=tiles["retention"])
    y = _tail(x2d, attn, ret, p.reshape(nb * seq, -1), w_out, g_ffn, w_ffn_in, w_ffn_out,
              g_ple, w_ple_gate, w_ple_proj, g_final, tm=tiles["tail"])
    heads_last = lambda a: a.reshape(nb, KV_HEADS, HEAD_DIM, seq).transpose(0, 3, 1, 2)
    return y.reshape(nb, seq, d), heads_last(kt), heads_last(vt), kit.transpose(0, 2, 1), st


def _sample_layer(x, p, pool_k, pool_v, pool_ik, state, page_table, weights, *, n_fast=24):
    g_mix, w_in, g_retn, w_out, g_ffn, w_ffn_in, w_ffn_out, g_ple, w_ple_gate, w_ple_proj, g_final = weights
    db, n_t, d = x.shape
    n_rows = db * n_t
    n_pages = page_table.shape[1]
    past = n_pages * PAGE_SIZE
    tiles = _tiles(n_rows, n_rows, n_pages)
    assert tiles["inproj"] == n_rows == tiles["tail"]
    dims = _mixer_dims(w_in, d)
    n_q = dims[0]
    w_packed = _pack_w_in(w_in, *dims)
    x2d = x.reshape(n_rows, d)
    pos = np.tile(past + np.arange(n_t), db)
    (qt, kt, vt, qit, kit, wit, rq, rkt, rv, rg, _, _, _) = _inproj(
        x2d, pos, g_mix, w_packed, seq=n_rows, tm=tiles["inproj"], dims=dims)
    q, qi, wi = qt[0].T, qit[0].T, wit[0].T

    def new_pages(at, heads):
        a = at[0].reshape(heads, -1, db, n_t).transpose(2, 0, 1, 3)
        return jnp.pad(a, ((0, 0), (0, 0), (0, 0), (0, PAGE_SIZE - n_t))).astype(BF16)
    group = n_q // HEAD_DIM // KV_HEADS
    qg = q.reshape(db, n_t, KV_HEADS, group, HEAD_DIM).transpose(0, 2, 3, 1, 4)
    qg = qg.reshape(db, KV_HEADS, group * n_t, HEAD_DIM)

    s3d = _sample_scores(page_table, qi.reshape(db, n_t * IDX_HEADS, IDX_DIM), wi.reshape(db, n_t * IDX_HEADS, 1),
                         new_pages(kit, 1)[:, 0], pool_ik.transpose(0, 2, 1), n_t=n_t,
                         pages_per_chunk=tiles["score_pages"])
    topk = min(TOPK_MAX, (past + n_t) // 4)
    s_sel, thr = _sample_select(s3d.reshape(n_rows, -1), topk=topk, n_fast=n_fast)
    o = _sample_attend(page_table, qg, s_sel.reshape(db, n_t, -1), thr.reshape(db, n_t, LANES),
                       new_pages(kt, KV_HEADS), new_pages(vt, KV_HEADS), pool_k.transpose(0, 2, 3, 1),
                       pool_v.transpose(0, 2, 3, 1), n_t=n_t, pages_per_chunk=tiles["attend_pages"])
    attn = o.reshape(db, KV_HEADS, group, n_t, HEAD_DIM).transpose(0, 3, 1, 2, 4).reshape(n_rows, n_q).astype(BF16)

    ret, st = _retention_sample(rq, rkt[0], rv, rg, g_retn, state, db=db, n_t=n_t)
    y = _tail(x2d, attn, ret, p.reshape(n_rows, -1), w_out, g_ffn, w_ffn_in, w_ffn_out,
              g_ple, w_ple_gate, w_ple_proj, g_final, tm=tiles["tail"])
    heads_last = lambda a: a[0].reshape(KV_HEADS, HEAD_DIM, db, n_t).transpose(2, 3, 0, 1)
    return y.reshape(db, n_t, d), heads_last(kt), heads_last(vt), kit[0].T.reshape(db, n_t, IDX_DIM), st


def kernel(x_prompt, x_sample, cache_k, cache_v, cache_idx_k, state_retn, page_table, p_prompt, p_sample, g_mix, w_in,
           g_retn, w_out, g_ffn, w_ffn_in, w_ffn_out, g_ple, w_ple_gate, w_ple_proj, g_final):
    depth = w_in.shape[0]
    assert depth == 1, "the final RMSNorm is fused into the single layer's tail kernel"
    weights = (g_mix[0], w_in[0], g_retn[0], w_out[0], g_ffn[0], w_ffn_in[0], w_ffn_out[0], g_ple[0], w_ple_gate[0],
               w_ple_proj[0], g_final)
    y_p, k_p, v_p, ik_p, st_p = _prompt_layer(x_prompt, p_prompt[0], weights)
    y_s, k_s, v_s, ik_s, st_s = _sample_layer(x_sample, p_sample[0], cache_k[0], cache_v[0], cache_idx_k[0],
                                              state_retn[0], page_table, weights)
    return (y_p, y_s, k_p[None], v_p[None], ik_p[None], st_p[None], k_s[None], v_s[None], ik_s[None], st_s[None])
```
